```python
import math
import jax, jax.numpy as jnp
from jax import lax
import numpy as np

D_MODEL = 2048
BATCH = 4
SEQ = 4096
DEPTH = 1

N_HEADS = 16
HEAD_DIM = 128
ATTN_WIDTH = N_HEADS * HEAD_DIM
DILATED_PATTERNS = ((128, 1), (512, 4), (2048, 16))
Q_BLOCK = 128

LRU_WIDTH = D_MODEL
LRU_BLOCKS = 16
LRU_BLOCK_DIM = LRU_WIDTH // LRU_BLOCKS
CONV_WIDTH = 4
LRU_C = 8.0

N_GROUPS = 4
EXPERTS_PER_GROUP = 8
N_EXPERTS = N_GROUPS * EXPERTS_PER_GROUP
TOP_K_IN_GROUP = 2
D_EXPERT = D_MODEL // 4

DEEPNORM_ALPHA = (2.0 * DEPTH) ** 0.25
DEEPNORM_BETA = (8.0 * DEPTH) ** -0.25
LN_EPS = 1e-5

IN_COLS = 3 * ATTN_WIDTH + 2 * LRU_WIDTH + 2 * D_MODEL
SPLITS = (ATTN_WIDTH, 2 * ATTN_WIDTH, 3 * ATTN_WIDTH,
          3 * ATTN_WIDTH + LRU_WIDTH, 3 * ATTN_WIDTH + 2 * LRU_WIDTH,
          3 * ATTN_WIDTH + 2 * LRU_WIDTH + D_MODEL)

kernel_name = "hybrid_dilated_attn_rglru_hmoe_deepnorm"


def _layer_norm(x, g, b):
    xf = x.astype(jnp.float32)
    mu = jnp.mean(xf, axis=-1, keepdims=True)
    var = jnp.mean(jnp.square(xf - mu), axis=-1, keepdims=True)
    y = (xf - mu) * lax.rsqrt(var + LN_EPS) * g.astype(jnp.float32) + b.astype(jnp.float32)
    return y.astype(x.dtype)


def _dilated_window_attention(q, k, v, window, dilation):
    b, s, h, e = q.shape
    n_back = window // dilation
    L = s // dilation
    nb = -(-L // Q_BLOCK)
    Lp = nb * Q_BLOCK

    def to_blocks(t):
        t = t.reshape(b, L, dilation, h, e)
        t = jnp.pad(t, ((0, 0), (0, Lp - L), (0, 0), (0, 0), (0, 0)))
        return t.reshape(b, nb, Q_BLOCK, dilation, h, e)

    def with_prev(t):
        prev = jnp.pad(t[:, :-1], ((0, 0), (1, 0), (0, 0), (0, 0), (0, 0), (0, 0)))
        return jnp.concatenate([prev, t], axis=2)

    qb = to_blocks(q)
    kw = with_prev(to_blocks(k))
    vw = with_prev(to_blocks(v))
    scores = jnp.einsum('bnqrhe,bnkrhe->bnqrhk', qb, kw,
                        preferred_element_type=jnp.float32) * (HEAD_DIM ** -0.5)
    qi = jnp.arange(Q_BLOCK)[:, None]
    kj = jnp.arange(2 * Q_BLOCK)[None, :]
    dist = Q_BLOCK + qi - kj
    band = (dist >= 0) & (dist <= n_back)
    first = (jnp.arange(nb) == 0)[:, None, None]
    valid = band[None] & ~(first & (kj[None] < Q_BLOCK))
    scores = jnp.where(valid[None, :, :, None, None, :], scores, -jnp.inf)
    lse = jax.nn.logsumexp(scores, axis=-1)
    p = jnp.exp(scores - lse[..., None])
    o = jnp.einsum('bnqrhk,bnkrhe->bnqrhe', p.astype(v.dtype), vw)
    o = o.reshape(b, Lp, dilation, h, e)[:, :L].reshape(b, s, h, e)
    lse = lse.reshape(b, Lp, dilation, h)[:, :L].reshape(b, s, h)
    return o, lse


def _mixture_of_dilations(q, k, v):
    outs, lses = [], []
    for window, dilation in DILATED_PATTERNS:
        o, lse = _dilated_window_attention(q, k, v, window, dilation)
        outs.append(o.astype(jnp.float32))
        lses.append(lse)
    wts = jax.nn.softmax(jnp.stack(lses), axis=0)
    o = jnp.sum(wts[..., None] * jnp.stack(outs), axis=0)
    return o.astype(q.dtype)


def _causal_depthwise_conv(x, w, bias):
    s = x.shape[1]
    xp = jnp.pad(x, ((0, 0), (CONV_WIDTH - 1, 0), (0, 0)))
    y = xp[:, 0:s] * w[0]
    for j in range(1, CONV_WIDTH):
        y = y + xp[:, j:j + s] * w[j]
    return y + bias


def _rg_lru(xc, w_a, b_a, w_x, b_x, lam):
    b, s, d = xc.shape
    blocks = xc.reshape(b, s, LRU_BLOCKS, LRU_BLOCK_DIM)
    gate_a = jnp.einsum('bsgi,gij->bsgj', blocks, w_a).reshape(b, s, d) + b_a
    gate_x = jnp.einsum('bsgi,gij->bsgj', blocks, w_x).reshape(b, s, d) + b_x
    r = jax.nn.sigmoid(gate_a.astype(jnp.float32))
    i = jax.nn.sigmoid(gate_x.astype(jnp.float32))
    log_a = -LRU_C * r * jax.nn.softplus(-lam.astype(jnp.float32))
    a = jnp.exp(log_a)
    u = jnp.sqrt(-jnp.expm1(2.0 * log_a)) * (i * xc.astype(jnp.float32))

    def combine(c1, c2):
        a1, b1 = c1
        a2, b2 = c2
        return a1 * a2, a2 * b1 + b2

    _, hs = lax.associative_scan(combine, (a, u), axis=1)
    return hs.astype(xc.dtype)


def setup_inputs(seed: int = 0) -> dict:
    key = jax.random.key(seed)
    ks = jax.random.split(key, 24)
    f32 = jnp.float32

    def nrm(k, shape, scale):
        return jax.random.normal(k, shape, f32) * scale

    col_scale = jnp.ones((IN_COLS,), f32).at[2 * ATTN_WIDTH:3 * ATTN_WIDTH].set(DEEPNORM_BETA)
    u = jax.random.uniform(ks[9], (DEPTH, LRU_WIDTH), f32, minval=0.9, maxval=0.999)
    s_lam = u ** (1.0 / LRU_C)
    lru_lambda = jnp.log(s_lam) - jnp.log1p(-s_lam)
    return {
        "x": nrm(ks[0], (BATCH, SEQ, D_MODEL), 1.0),
        "w_in": nrm(ks[1], (DEPTH, D_MODEL, IN_COLS), D_MODEL ** -0.5) * col_scale,
        "conv_w": nrm(ks[2], (DEPTH, CONV_WIDTH, LRU_WIDTH), CONV_WIDTH ** -0.5),
        "conv_b": nrm(ks[3], (DEPTH, LRU_WIDTH), 0.02),
        "lru_wa": nrm(ks[4], (DEPTH, LRU_BLOCKS, LRU_BLOCK_DIM, LRU_BLOCK_DIM), LRU_BLOCK_DIM ** -0.5),
        "lru_ba": nrm(ks[5], (DEPTH, LRU_WIDTH), 0.1),
        "lru_wx": nrm(ks[6], (DEPTH, LRU_BLOCKS, LRU_BLOCK_DIM, LRU_BLOCK_DIM), LRU_BLOCK_DIM ** -0.5),
        "lru_bx": nrm(ks[7], (DEPTH, LRU_WIDTH), 0.1),
        "lru_lambda": lru_lambda,
        "w_o_attn": nrm(ks[10], (DEPTH, ATTN_WIDTH, D_MODEL), ATTN_WIDTH ** -0.5),
        "w_o_lru": nrm(ks[11], (DEPTH, LRU_WIDTH, D_MODEL), LRU_WIDTH ** -0.5),
        "w_out": nrm(ks[12], (DEPTH, D_MODEL, D_MODEL), D_MODEL ** -0.5 * DEEPNORM_BETA),
        "ln1_g": 1.0 + nrm(ks[13], (DEPTH, D_MODEL), 0.02),
        "ln1_b": nrm(ks[14], (DEPTH, D_MODEL), 0.02),
        "router_group_w": nrm(ks[15], (DEPTH, D_MODEL, N_GROUPS), D_MODEL ** -0.5),
        "router_group_b": nrm(ks[16], (DEPTH, N_GROUPS), 0.01),
        "router_expert_w": nrm(ks[17], (DEPTH, D_MODEL, N_EXPERTS), D_MODEL ** -0.5),
        "router_expert_b": nrm(ks[18], (DEPTH, N_EXPERTS), 0.01),
        "w_e_gate": nrm(ks[19], (DEPTH, N_EXPERTS, D_MODEL, D_EXPERT), D_MODEL ** -0.5),
        "w_e_up": nrm(ks[20], (DEPTH, N_EXPERTS, D_MODEL, D_EXPERT), D_MODEL ** -0.5),
        "w_e_down": nrm(ks[21], (DEPTH, N_EXPERTS, D_EXPERT, D_MODEL), D_EXPERT ** -0.5 * DEEPNORM_BETA),
        "ln2_g": 1.0 + nrm(ks[22], (DEPTH, D_MODEL), 0.02),
        "ln2_b": nrm(ks[23], (DEPTH, D_MODEL), 0.02),
    }


def reference(x, w_in, conv_w, conv_b, lru_wa, lru_ba, lru_wx, lru_bx, lru_lambda,
              w_o_attn, w_o_lru, w_out, ln1_g, ln1_b, router_group_w, router_group_b,
              router_expert_w, router_expert_b, w_e_gate, w_e_up, w_e_down, ln2_g, ln2_b):
    b, s, _ = x.shape
    for l in range(DEPTH):
        proj = x @ w_in[l]
        q, k, v, xr, xg, g_attn, g_lru = jnp.split(proj, SPLITS, axis=-1)
        q = q.reshape(b, s, N_HEADS, HEAD_DIM)
        k = k.reshape(b, s, N_HEADS, HEAD_DIM)
        v = v.reshape(b, s, N_HEADS, HEAD_DIM)
        y_attn = _mixture_of_dilations(q, k, v).reshape(b, s, ATTN_WIDTH) @ w_o_attn[l]

        xc = _causal_depthwise_conv(xr, conv_w[l], conv_b[l])
        h = _rg_lru(xc, lru_wa[l], lru_ba[l], lru_wx[l], lru_bx[l], lru_lambda[l])
        y_lru = (h * jax.nn.gelu(xg)) @ w_o_lru[l]

        merged = jax.nn.sigmoid(g_attn) * y_attn + jax.nn.sigmoid(g_lru) * y_lru
        x = _layer_norm(DEEPNORM_ALPHA * x + merged @ w_out[l], ln1_g[l], ln1_b[l])

        xt = x.reshape(b * s, D_MODEL)
        grp_logits = (xt @ router_group_w[l] + router_group_b[l]).astype(jnp.float32)
        grp_prob = jax.nn.softmax(grp_logits, axis=-1)
        g_idx = jnp.argmax(grp_logits, axis=-1)
        p_grp = jnp.take_along_axis(grp_prob, g_idx[:, None], axis=-1)
        exp_logits = (xt @ router_expert_w[l] + router_expert_b[l]).astype(jnp.float32)
        exp_logits = exp_logits.reshape(-1, N_GROUPS, EXPERTS_PER_GROUP)
        in_grp = jnp.take_along_axis(exp_logits, g_idx[:, None, None], axis=1)[:, 0]
        top_vals, top_idx = lax.top_k(in_grp, TOP_K_IN_GROUP)
        wts = p_grp * jax.nn.softmax(top_vals, axis=-1)
        glob_idx = g_idx[:, None] * EXPERTS_PER_GROUP + top_idx
        gate = jnp.sum(jax.nn.one_hot(glob_idx, N_EXPERTS, dtype=jnp.float32) * wts[..., None],
                       axis=1).astype(xt.dtype)
        y = jnp.zeros_like(xt)
        for e in range(N_EXPERTS):
            he = jax.nn.silu(xt @ w_e_gate[l, e]) * (xt @ w_e_up[l, e])
            y = y + gate[:, e:e + 1] * (he @ w_e_down[l, e])
        x = _layer_norm(DEEPNORM_ALPHA * x + y.reshape(b, s, D_MODEL), ln2_g[l], ln2_b[l])
    return x
```

```python
import functools
import math

import numpy as np
import jax
import jax.numpy as jnp
from jax import lax
from jax.experimental import pallas as pl
from jax.experimental.pallas import tpu as pltpu

F32 = jnp.float32
BF16 = jnp.bfloat16

N_HEADS = 16
HEAD_DIM = 128
N_BACK = 128
CLASSES = 16
CONV_WIDTH = 4
LRU_BLOCK_DIM = 128
LRU_C = 8.0
N_GROUPS = 4
EXPERTS_PER_GROUP = 8
N_EXPERTS = N_GROUPS * EXPERTS_PER_GROUP
DEPTH = 1
DEEPNORM_ALPHA = (2.0 * DEPTH) ** 0.25
LN_EPS = 1e-5
ATTN_SCALE = HEAD_DIM ** -0.5

LANES = 128
SUBLANES = 8
VMEM_LIMIT_BYTES = 56 * 1024 * 1024

MASK_BIAS = -1e30
MOE_TILE = 256
ROUTER_LANES = LANES


def _cparams(sem):
    return pltpu.CompilerParams(dimension_semantics=sem, vmem_limit_bytes=VMEM_LIMIT_BYTES)


def _qkv_kernel(x_ref, w_ref, o_ref):
    b, i, d = x_ref.shape
    res = jnp.dot(x_ref[...].reshape(b * i, d), w_ref[...],
                  preferred_element_type=F32).astype(o_ref.dtype)
    for hh in range(o_ref.shape[0]):
        o_ref[hh, 0] = res[:, hh * HEAD_DIM:(hh + 1) * HEAD_DIM]


def _qkv_proj(x_cls, w_qkv, *, tn):
    b, i, cd = x_cls.shape
    d = cd // CLASSES
    n = w_qkv.shape[1]
    heads_per_step = tn // HEAD_DIM
    return pl.pallas_call(
        _qkv_kernel,
        grid=(CLASSES, n // tn),
        in_specs=[pl.BlockSpec((b, i, d), lambda r, j: (0, 0, r)),
                  pl.BlockSpec((d, tn), lambda r, j: (0, j))],
        out_specs=pl.BlockSpec((heads_per_step, 1, b * i, HEAD_DIM), lambda r, j: (j, r, 0, 0)),
        out_shape=jax.ShapeDtypeStruct((n // HEAD_DIM, CLASSES, b * i, HEAD_DIM), BF16),
        compiler_params=_cparams(("parallel", "parallel")),
        name="qkv_proj",
    )(x_cls, w_qkv)


def _mm_kernel(x_ref, w_ref, o_ref):
    o_ref[...] = jnp.dot(x_ref[...], w_ref[...], preferred_element_type=F32).astype(o_ref.dtype)


def _rest_proj(x2d, w_rest, *, tm, tn):
    t, d = x2d.shape
    n = w_rest.shape[1]
    return pl.pallas_call(
        _mm_kernel,
        grid=(t // tm, n // tn),
        in_specs=[pl.BlockSpec((tm, d), lambda i, j: (i, 0)),
                  pl.BlockSpec((d, tn), lambda i, j: (0, j))],
        out_specs=pl.BlockSpec((tm, tn), lambda i, j: (i, j)),
        out_shape=jax.ShapeDtypeStruct((t, n), BF16),
        compiler_params=_cparams(("parallel", "parallel")),
        name="rest_proj",
    )(x2d, w_rest)


def _band_bias(dist):
    return np.where((dist >= 0) & (dist <= N_BACK), 0.0, MASK_BIAS).astype(np.float32)


def _attn_biases(i_len):
    iq = np.arange(i_len)[:, None]
    ik = np.arange(i_len)[None, :]
    b3 = _band_bias(iq - ik)
    cq = np.repeat(np.arange(4), 32)[:, None]
    lq = np.tile(np.arange(32), 4)[:, None]
    ck = np.repeat(np.arange(4), 64)[None, :]
    lk = np.tile(np.arange(64), 4)[None, :]
    b2 = np.stack([_band_bias(4 * (lq - lk) + (cq - ck)),
                   _band_bias(4 * (32 + lq - lk) + (cq - ck))])
    rq = np.repeat(np.arange(16), 16)[:, None]
    lq = np.tile(np.arange(16), 16)[:, None]
    rk = np.repeat(np.arange(16), 32)[None, :]
    lk = np.tile(np.arange(32), 16)[None, :]
    b1 = np.stack([_band_bias(16 * (lq - lk) + (rq - rk)),
                   _band_bias(16 * (16 + lq - lk) + (rq - rk))])
    return b1, b2, b3


def _scores(q, k, bias):
    s = lax.dot_general(q, k, (((1,), (1,)), ((), ())), preferred_element_type=F32)
    return s * ATTN_SCALE + bias


def _attn_kernel(q_ref, k_ref, v_ref, b1_ref, b2_ref, b3_ref, o_ref, m_sc, l_sc, acc_sc):
    i_len = q_ref.shape[2]

    def d16_body(r, carry):
        q, k, v = q_ref[0, r], k_ref[0, r], v_ref[0, r]
        s = _scores(q, k, b3_ref[...])
        m = jnp.max(s, axis=1, keepdims=True)
        p = jnp.exp(s - m)
        m_sc[r] = m
        l_sc[r] = jnp.sum(p, axis=1, keepdims=True)
        acc_sc[r] = jnp.dot(p.astype(BF16), v, preferred_element_type=F32)
        return carry

    lax.fori_loop(0, CLASSES, d16_body, 0)

    def online_update(s, v, m_old, l_old, acc_old):
        m_new = jnp.maximum(m_old, jnp.max(s, axis=1, keepdims=True))
        alpha = jnp.exp(m_old - m_new)
        p = jnp.exp(s - m_new)
        l_new = alpha * l_old + jnp.sum(p, axis=1, keepdims=True)
        acc_new = alpha * acc_old + jnp.dot(p.astype(BF16), v, preferred_element_type=F32)
        return m_new, l_new, acc_new

    def d4_body(idx, carry):
        r4 = idx // (i_len // 32)
        n = idx % (i_len // 32)
        q0 = pl.multiple_of(n * 32, 32)
        k0 = pl.multiple_of(jnp.maximum(n - 1, 0) * 32, 32)
        cls = [r4 + 4 * c for c in range(4)]
        q = jnp.concatenate([q_ref[0, r, pl.ds(q0, 32), :] for r in cls], axis=0)
        k = jnp.concatenate([k_ref[0, r, pl.ds(k0, 64), :] for r in cls], axis=0)
        v = jnp.concatenate([v_ref[0, r, pl.ds(k0, 64), :] for r in cls], axis=0)
        s = _scores(q, k, b2_ref[jnp.minimum(n, 1)])
        m_old = jnp.concatenate([m_sc[r, pl.ds(q0, 32), :] for r in cls], axis=0)
        l_old = jnp.concatenate([l_sc[r, pl.ds(q0, 32), :] for r in cls], axis=0)
        a_old = jnp.concatenate([acc_sc[r, pl.ds(q0, 32), :] for r in cls], axis=0)
        m_new, l_new, a_new = online_update(s, v, m_old, l_old, a_old)
        for c, r in enumerate(cls):
            m_sc[r, pl.ds(q0, 32), :] = m_new[32 * c:32 * (c + 1)]
            l_sc[r, pl.ds(q0, 32), :] = l_new[32 * c:32 * (c + 1)]
            acc_sc[r, pl.ds(q0, 32), :] = a_new[32 * c:32 * (c + 1)]
        return carry

    lax.fori_loop(0, 4 * (i_len // 32), d4_body, 0)

    def d1_body(m, carry):
        q0 = pl.multiple_of(m * 16, 16)
        k0 = pl.multiple_of(jnp.maximum(m - 1, 0) * 16, 16)
        q = q_ref[0, :, pl.ds(q0, 16), :].reshape(CLASSES * 16, HEAD_DIM)
        k = k_ref[0, :, pl.ds(k0, 32), :].reshape(CLASSES * 32, HEAD_DIM)
        v = v_ref[0, :, pl.ds(k0, 32), :].reshape(CLASSES * 32, HEAD_DIM)
        s = _scores(q, k, b1_ref[jnp.minimum(m, 1)])
        m_old = m_sc[:, pl.ds(q0, 16), :].reshape(CLASSES * 16, 1)
        l_old = l_sc[:, pl.ds(q0, 16), :].reshape(CLASSES * 16, 1)
        a_old = acc_sc[:, pl.ds(q0, 16), :].reshape(CLASSES * 16, HEAD_DIM)
        m_new, l_new, a_new = online_update(s, v, m_old, l_old, a_old)
        m_sc[:, pl.ds(q0, 16), :] = m_new.reshape(CLASSES, 16, 1)
        l_sc[:, pl.ds(q0, 16), :] = l_new.reshape(CLASSES, 16, 1)
        acc_sc[:, pl.ds(q0, 16), :] = a_new.reshape(CLASSES, 16, HEAD_DIM)
        return carry

    lax.fori_loop(0, i_len // 16, d1_body, 0)

    o_ref[0] = (acc_sc[...] / l_sc[...]).astype(o_ref.dtype)


def _dilated_attention(qkv_hm, batch):
    n3, _, bi, _ = qkv_hm.shape
    h = n3 // 3
    i_len = bi // batch
    b1, b2, b3 = (jnp.asarray(a) for a in _attn_biases(i_len))
    blk = (1, CLASSES, i_len, HEAD_DIM)
    const = lambda a: pl.BlockSpec(a.shape, lambda hh, bb: (0,) * a.ndim)
    return pl.pallas_call(
        _attn_kernel,
        grid=(h, batch),
        in_specs=[pl.BlockSpec(blk, lambda hh, bb: (hh, 0, bb, 0)),
                  pl.BlockSpec(blk, lambda hh, bb: (h + hh, 0, bb, 0)),
                  pl.BlockSpec(blk, lambda hh, bb: (2 * h + hh, 0, bb, 0)),
                  const(b1), const(b2), const(b3)],
        out_specs=pl.BlockSpec(blk, lambda hh, bb: (hh, 0, bb, 0)),
        out_shape=jax.ShapeDtypeStruct((h, CLASSES, bi, HEAD_DIM), BF16),
        scratch_shapes=[pltpu.VMEM((CLASSES, i_len, 1), F32),
                        pltpu.VMEM((CLASSES, i_len, 1), F32),
                        pltpu.VMEM((CLASSES, i_len, HEAD_DIM), F32)],
        compiler_params=_cparams(("parallel", "parallel")),
        name="dilated_attn",
    )(qkv_hm, qkv_hm, qkv_hm, b1, b2, b3)


def _softplus(z):
    return jnp.maximum(z, 0.0) + jnp.log1p(jnp.exp(-jnp.abs(z)))


def _lru_kernel(xr_ref, xg_ref, cw_ref, cb_ref, wa_ref, ba_ref, wx_ref, bx_ref, lam_ref,
                o_ref, h_sc, tail_sc):
    ts = xr_ref.shape[1]
    cw = xr_ref.shape[2]
    nv = ts // SUBLANES

    @pl.when(pl.program_id(2) == 0)
    def _():
        h_sc[...] = jnp.zeros_like(h_sc)
        tail_sc[...] = jnp.zeros_like(tail_sc)

    row = lax.broadcasted_iota(jnp.int32, (nv, SUBLANES, LANES), 1)
    for g in range(cw // LANES):
        sl = slice(g * LANES, (g + 1) * LANES)
        xr = xr_ref[0, :, sl].astype(F32)
        xfull = jnp.concatenate([tail_sc[:, sl], xr], axis=0)
        taps = cw_ref[:, sl]
        xc = xr * taps[3:4]
        for j in range(1, CONV_WIDTH):
            shifted = pltpu.roll(xfull, j, axis=0)[SUBLANES:]
            xc = xc + shifted * taps[3 - j:4 - j]
        xc = xc + cb_ref[:, sl]
        tail_sc[:, sl] = xr[ts - SUBLANES:]

        xcb = xc.astype(BF16)
        gate_a = jnp.dot(xcb, wa_ref[g], preferred_element_type=F32) + ba_ref[:, sl]
        gate_x = jnp.dot(xcb, wx_ref[g], preferred_element_type=F32) + bx_ref[:, sl]
        r = jax.nn.sigmoid(gate_a)
        i = jax.nn.sigmoid(gate_x)
        log_a = (-LRU_C * _softplus(-lam_ref[:, sl])) * r
        a = jnp.exp(log_a)
        u = jnp.sqrt(1.0 - a * a) * (i * xc)

        a3 = a.reshape(nv, SUBLANES, LANES)
        b3 = u.reshape(nv, SUBLANES, LANES)
        for k in (1, 2, 4):
            a_sh = jnp.where(row >= k, pltpu.roll(a3, k, axis=1), 1.0)
            b_sh = jnp.where(row >= k, pltpu.roll(b3, k, axis=1), 0.0)
            b3 = a3 * b_sh + b3
            a3 = a3 * a_sh
        h = h_sc[:, sl]
        outs = []
        for vi in range(nv):
            hv = a3[vi] * h + b3[vi]
            outs.append(hv)
            h = hv[SUBLANES - 1:SUBLANES]
        h_sc[:, sl] = h
        hs = jnp.concatenate(outs, axis=0)
        o_ref[0, :, sl] = (hs * jax.nn.gelu(xg_ref[0, :, sl].astype(F32))).astype(o_ref.dtype)


def _rglru(rest3, conv_w, conv_b, wa, ba, wx, bx, lam, *, d, ts, cw):
    b, s, _ = rest3.shape
    ncw = d // cw
    gpb = cw // LRU_BLOCK_DIM
    vec = lambda: pl.BlockSpec((1, cw), lambda bb, c, t: (0, c))
    return pl.pallas_call(
        _lru_kernel,
        grid=(b, ncw, s // ts),
        in_specs=[pl.BlockSpec((1, ts, cw), lambda bb, c, t: (bb, t, c)),
                  pl.BlockSpec((1, ts, cw), lambda bb, c, t: (bb, t, ncw + c)),
                  pl.BlockSpec((CONV_WIDTH, cw), lambda bb, c, t: (0, c)),
                  vec(),
                  pl.BlockSpec((gpb, LRU_BLOCK_DIM, LRU_BLOCK_DIM), lambda bb, c, t: (c, 0, 0)),
                  vec(),
                  pl.BlockSpec((gpb, LRU_BLOCK_DIM, LRU_BLOCK_DIM), lambda bb, c, t: (c, 0, 0)),
                  vec(), vec()],
        out_specs=pl.BlockSpec((1, ts, cw), lambda bb, c, t: (bb, t, c)),
        out_shape=jax.ShapeDtypeStruct((b, s, d), BF16),
        scratch_shapes=[pltpu.VMEM((1, cw), F32), pltpu.VMEM((SUBLANES, cw), F32)],
        compiler_params=_cparams(("parallel", "parallel", "arbitrary")),
        name="rglru",
    )(rest3, rest3, conv_w, conv_b, wa, ba, wx, bx, lam)


def _merge_kernel(a_ref, h_ref, ga_ref, gl_ref, wa_ref, wl_ref, o_ref, a_sc):
    rows = a_sc.shape[0]

    @pl.when(pl.program_id(1) == 0)
    def _():
        for hh in range(a_ref.shape[0]):
            a_sc[:, hh * HEAD_DIM:(hh + 1) * HEAD_DIM] = a_ref[hh, 0]

    tn = o_ref.shape[2]
    y_attn = jnp.dot(a_sc[...], wa_ref[...], preferred_element_type=F32)
    y_lru = jnp.dot(h_ref[...].reshape(rows, h_ref.shape[2]), wl_ref[...], preferred_element_type=F32)
    g_a = jax.nn.sigmoid(ga_ref[...].reshape(rows, tn).astype(F32))
    g_l = jax.nn.sigmoid(gl_ref[...].reshape(rows, tn).astype(F32))
    o_ref[0] = (g_a * y_attn + g_l * y_lru).astype(o_ref.dtype)


def _merge(attn_hm, hg_cls, rest_cls, w_o_attn, w_o_lru, *, batch, tn):
    h, _, rows, _ = attn_hm.shape
    i_len = rows // batch
    d = w_o_attn.shape[1]
    nj = d // tn
    gcols = 4 * d // tn
    return pl.pallas_call(
        _merge_kernel,
        grid=(CLASSES, nj),
        in_specs=[pl.BlockSpec((h, 1, rows, HEAD_DIM), lambda r, j: (0, r, 0, 0)),
                  pl.BlockSpec((batch, i_len, d), lambda r, j: (0, 0, r)),
                  pl.BlockSpec((batch, i_len, tn), lambda r, j: (0, 0, r * gcols + 2 * nj + j)),
                  pl.BlockSpec((batch, i_len, tn), lambda r, j: (0, 0, r * gcols + 3 * nj + j)),
                  pl.BlockSpec((h * HEAD_DIM, tn), lambda r, j: (0, j)),
                  pl.BlockSpec((d, tn), lambda r, j: (0, j))],
        out_specs=pl.BlockSpec((1, rows, tn), lambda r, j: (r, 0, j)),
        out_shape=jax.ShapeDtypeStruct((CLASSES, rows, d), BF16),
        scratch_shapes=[pltpu.VMEM((rows, h * HEAD_DIM), BF16)],
        compiler_params=_cparams(("parallel", "arbitrary")),
        name="merge",
    )(attn_hm, hg_cls, rest_cls, rest_cls, w_o_attn, w_o_lru)


def _layer_norm(z, g, b):
    mu = jnp.mean(z, axis=-1, keepdims=True)
    zc = z - mu
    var = jnp.mean(zc * zc, axis=-1, keepdims=True)
    return zc * lax.rsqrt(var + LN_EPS) * g + b


def _out_ln_route_kernel(m_ref, x_ref, w_ref, g_ref, b_ref, rwh_ref, rwl_ref, rb_ref,
                         x1_ref, route_ref):
    rows = m_ref.shape[1]
    d = w_ref.shape[1]
    y = jnp.dot(m_ref[0], w_ref[...], preferred_element_type=F32)
    z = DEEPNORM_ALPHA * x_ref[...].reshape(rows, d) + y
    x1 = _layer_norm(z, g_ref[...], b_ref[...])
    x1_ref[...] = x1

    hi = x1.astype(BF16)
    lo = (x1 - hi.astype(F32)).astype(BF16)
    lg = (jnp.dot(hi, rwh_ref[...], preferred_element_type=F32)
          + jnp.dot(lo, rwh_ref[...], preferred_element_type=F32)
          + jnp.dot(hi, rwl_ref[...], preferred_element_type=F32)) + rb_ref[...]

    lane = lax.broadcasted_iota(jnp.int32, lg.shape, 1).astype(F32)
    first = lambda hit: jnp.min(jnp.where(hit, lane, float(ROUTER_LANES)), axis=1, keepdims=True)
    in_grp = lane < N_GROUPS
    lgg = jnp.where(in_grp, lg, MASK_BIAS)
    mg = jnp.max(lgg, axis=1, keepdims=True)
    g_idx = first(lgg == mg)
    p_grp = 1.0 / jnp.sum(jnp.where(in_grp, jnp.exp(lg - mg), 0.0), axis=1, keepdims=True)
    lo_lane = N_GROUPS + EXPERTS_PER_GROUP * g_idx
    in_exp = (lane >= lo_lane) & (lane < lo_lane + EXPERTS_PER_GROUP)
    le = jnp.where(in_exp, lg, MASK_BIAS)
    v1 = jnp.max(le, axis=1, keepdims=True)
    i1 = first((le == v1) & in_exp)
    rest = in_exp & (lane != i1)
    le2 = jnp.where(rest, lg, MASK_BIAS)
    v2 = jnp.max(le2, axis=1, keepdims=True)
    i2 = first((le2 == v2) & rest)
    t = jnp.exp(v2 - v1)
    w1 = p_grp / (1.0 + t)
    w2 = p_grp * t / (1.0 + t)
    route_ref[...] = jnp.where(lane == 0.0, i1 - N_GROUPS,
                     jnp.where(lane == 1.0, i2 - N_GROUPS,
                     jnp.where(lane == 2.0, w1,
                     jnp.where(lane == 3.0, w2, 0.0))))


def _out_ln_route(merged, x_cls, w_out, ln_g, ln_b, rw_hi, rw_lo, rb, *, batch, bsub):
    _, rows, d = merged.shape
    i_len = rows // batch
    nb = batch // bsub
    tm = bsub * i_len
    const = lambda a: pl.BlockSpec(a.shape, lambda r, bb: (0,) * a.ndim)
    return pl.pallas_call(
        _out_ln_route_kernel,
        grid=(CLASSES, nb),
        in_specs=[pl.BlockSpec((1, tm, d), lambda r, bb: (r, bb, 0)),
                  pl.BlockSpec((bsub, i_len, d), lambda r, bb: (bb, 0, r)),
                  const(w_out), const(ln_g), const(ln_b), const(rw_hi), const(rw_lo), const(rb)],
        out_specs=[pl.BlockSpec((tm, d), lambda r, bb: (r * nb + bb, 0)),
                   pl.BlockSpec((tm, ROUTER_LANES), lambda r, bb: (r * nb + bb, 0))],
        out_shape=[jax.ShapeDtypeStruct((CLASSES * rows, d), F32),
                   jax.ShapeDtypeStruct((CLASSES * rows, ROUTER_LANES), F32)],
        compiler_params=_cparams(("parallel", "parallel")),
        name="out_ln_route",
    )(merged, x_cls, w_out, ln_g, ln_b, rw_hi, rw_lo, rb)


def _moe_kernel(te_ref, tv_ref, nu_ref, g0_ref, gn_ref, dst_ref, x_hbm, wg_ref, wu_ref, wd_ref,
                y_hbm, xbuf, ybuf, gsem, ssem):
    i = pl.program_id(0)
    n_used = nu_ref[0]
    slot = lax.rem(i, 2)
    tm = xbuf.shape[1]

    def gather_copy(tok, j, s):
        return pltpu.make_async_copy(x_hbm.at[pl.ds(tok, 1)], xbuf.at[s, pl.ds(j, 1)], gsem.at[s])

    def scatter_copy(row, j, s):
        return pltpu.make_async_copy(ybuf.at[s, pl.ds(j, 1)], y_hbm.at[pl.ds(row, 1)], ssem.at[s])

    def start_gather(idx_ref, s):
        def body(j, c):
            gather_copy(idx_ref[0, 0, j], j, s).start()
            return c
        lax.fori_loop(0, tm, body, 0, unroll=8)

    def wait_gather(s):
        def body(j, c):
            gather_copy(0, j, s).wait()
            return c
        lax.fori_loop(0, tm, body, 0, unroll=8)

    def start_scatter(s, n_rows):
        def body(j, c):
            scatter_copy(dst_ref[0, 0, j], j, s).start()
            return c
        lax.fori_loop(0, n_rows, body, 0)

    def wait_scatter(s, n_rows):
        def body(j, c):
            scatter_copy(0, j, s).wait()
            return c
        lax.fori_loop(0, n_rows, body, 0)

    @pl.when(i == 0)
    def _():
        start_gather(g0_ref, 0)

    @pl.when(i + 1 < n_used)
    def _():
        start_gather(gn_ref, 1 - slot)

    @pl.when(i < n_used)
    def _():
        wait_gather(slot)
        xb = xbuf[slot].astype(BF16)
        gate = jnp.dot(xb, wg_ref[0], preferred_element_type=F32)
        up = jnp.dot(xb, wu_ref[0], preferred_element_type=F32)
        he = (jax.nn.silu(gate) * up).astype(BF16)
        ybuf[slot] = jnp.dot(he, wd_ref[0], preferred_element_type=F32)
        start_scatter(slot, tv_ref[i])

        @pl.when(i >= 1)
        def _():
            wait_scatter(1 - slot, tv_ref[jnp.maximum(i - 1, 0)])

        @pl.when(i == n_used - 1)
        def _():
            wait_scatter(slot, tv_ref[i])


def _moe_experts(x1, tile_expert, tile_valid, n_used, gidx, dest, wg, wu, wd):
    t, d = x1.shape
    nt = gidx.shape[0]
    f = wg.shape[2]
    idx_spec = lambda fn: pl.BlockSpec((1, 1, MOE_TILE), fn, memory_space=pltpu.SMEM)
    grid_spec = pltpu.PrefetchScalarGridSpec(
        num_scalar_prefetch=3,
        grid=(nt,),
        in_specs=[idx_spec(lambda i, te, tv, nu: (0, 0, 0)),
                  idx_spec(lambda i, te, tv, nu: (jnp.minimum(i + 1, nt - 1), 0, 0)),
                  idx_spec(lambda i, te, tv, nu: (i, 0, 0)),
                  pl.BlockSpec(memory_space=pl.ANY),
                  pl.BlockSpec((1, d, f), lambda i, te, tv, nu: (te[i], 0, 0)),
                  pl.BlockSpec((1, d, f), lambda i, te, tv, nu: (te[i], 0, 0)),
                  pl.BlockSpec((1, f, d), lambda i, te, tv, nu: (te[i], 0, 0))],
        out_specs=pl.BlockSpec(memory_space=pl.ANY),
        scratch_shapes=[pltpu.VMEM((2, MOE_TILE, d), F32),
                        pltpu.VMEM((2, MOE_TILE, d), F32),
                        pltpu.SemaphoreType.DMA((2,)),
                        pltpu.SemaphoreType.DMA((2,))],
    )
    return pl.pallas_call(
        _moe_kernel,
        grid_spec=grid_spec,
        out_shape=jax.ShapeDtypeStruct((2 * t, d), F32),
        compiler_params=_cparams(("arbitrary",)),
        name="moe_experts",
    )(tile_expert, tile_valid, n_used, gidx, gidx, dest, x1, wg, wu, wd)


def _moe_plan(route, n_tiles):
    t = route.shape[0]
    p = 2 * t
    e_flat = route[:, 0:2].astype(jnp.int32).reshape(p)
    skeys = jnp.sort(e_flat * p + jnp.arange(p, dtype=jnp.int32))
    sorted_pair = skeys % p
    experts = jnp.arange(N_EXPERTS, dtype=jnp.int32)
    counts = jnp.sum((e_flat[:, None] == experts[None, :]).astype(jnp.int32), axis=0)
    tiles_per = (counts + MOE_TILE - 1) // MOE_TILE
    tile_end = jnp.cumsum(tiles_per)
    tile_start = tile_end - tiles_per
    row_start = jnp.cumsum(counts) - counts
    n_used = tile_end[-1]
    tile_ids = jnp.arange(n_tiles, dtype=jnp.int32)
    tile_expert = jnp.sum((tile_ids[:, None] >= tile_end[None, :]).astype(jnp.int32), axis=1)
    tile_expert = jnp.minimum(tile_expert, N_EXPERTS - 1)
    last_expert = jnp.take(tile_expert, jnp.maximum(n_used - 1, 0))
    tile_expert = jnp.where(tile_ids < n_used, tile_expert, last_expert)
    slot = jnp.arange(n_tiles * MOE_TILE, dtype=jnp.int32)
    s_tile = slot // MOE_TILE
    s_exp = jnp.take(tile_expert, s_tile)
    rank = slot - jnp.take(tile_start, s_exp) * MOE_TILE
    valid = (s_tile < n_used) & (rank < jnp.take(counts, s_exp))
    pos = jnp.clip(jnp.take(row_start, s_exp) + rank, 0, p - 1)
    pair = jnp.take(sorted_pair, pos)
    gidx = jnp.where(valid, pair // 2, 0)
    dest = jnp.where(valid, pair, 0)
    shape3 = (n_tiles, 1, MOE_TILE)
    tile_valid = jnp.sum(valid.reshape(n_tiles, MOE_TILE).astype(jnp.int32), axis=1)
    return tile_expert, tile_valid, n_used.reshape(1), gidx.reshape(shape3), dest.reshape(shape3)


def _final_kernel(x1_ref, y_ref, route_ref, g_ref, b_ref, o_ref):
    d = x1_ref.shape[1]
    w0 = route_ref[:, 2:3]
    w1 = route_ref[:, 3:4]
    y = w0 * y_ref[:, :d] + w1 * y_ref[:, d:]
    z = DEEPNORM_ALPHA * x1_ref[...] + y
    o_ref[...] = _layer_norm(z, g_ref[...], b_ref[...]).reshape(o_ref.shape)


def _final_ln(x1, y_pairs, route, ln_g, ln_b, *, batch, seq, bsub):
    t, d = x1.shape
    i_len = seq // CLASSES
    nb = batch // bsub
    tm = bsub * i_len
    const = lambda a: pl.BlockSpec(a.shape, lambda r, bb: (0,) * a.ndim)
    out = pl.pallas_call(
        _final_kernel,
        grid=(CLASSES, nb),
        in_specs=[pl.BlockSpec((tm, d), lambda r, bb: (r * nb + bb, 0)),
                  pl.BlockSpec((tm, 2 * d), lambda r, bb: (r * nb + bb, 0)),
                  pl.BlockSpec((tm, ROUTER_LANES), lambda r, bb: (r * nb + bb, 0)),
                  const(ln_g), const(ln_b)],
        out_specs=pl.BlockSpec((bsub, i_len, d), lambda r, bb: (bb, 0, r)),
        out_shape=jax.ShapeDtypeStruct((batch, i_len, CLASSES * d), F32),
        compiler_params=_cparams(("parallel", "parallel")),
        name="final_ln",
    )(x1, y_pairs, route, ln_g, ln_b)
    return out.reshape(batch, seq, d)


def _router_weights(router_group_w, router_group_b, router_expert_w, router_expert_b):
    d = router_group_w.shape[0]
    pad = ROUTER_LANES - N_GROUPS - N_EXPERTS
    w = jnp.concatenate([router_group_w, router_expert_w, jnp.zeros((d, pad), F32)], axis=1)
    b = jnp.concatenate([router_group_b, router_expert_b, jnp.zeros((pad,), F32)]).reshape(1, ROUTER_LANES)
    w_hi = w.astype(BF16)
    w_lo = (w - w_hi.astype(F32)).astype(BF16)
    return w_hi, w_lo, b


def kernel(x, w_in, conv_w, conv_b, lru_wa, lru_ba, lru_wx, lru_bx, lru_lambda, w_o_attn, w_o_lru, w_out, ln1_g, ln1_b, router_group_w, router_group_b, router_expert_w, router_expert_b, w_e_gate, w_e_up, w_e_down, ln2_g, ln2_b):
    batch, seq, d = x.shape
    i_len = seq // CLASSES
    t = batch * seq
    attn_w = N_HEADS * HEAD_DIM
    for l in range(DEPTH):
        x_bf = x.astype(BF16)
        w_in_bf = w_in[l].astype(BF16)
        row = lambda a: a.reshape(1, -1)

        qkv_hm = _qkv_proj(x_bf.reshape(batch, i_len, CLASSES * d), w_in_bf[:, :3 * attn_w], tn=1024)
        rest = _rest_proj(x_bf.reshape(t, d), w_in_bf[:, 3 * attn_w:], tm=1024, tn=1024)
        attn_hm = _dilated_attention(qkv_hm, batch)
        hg = _rglru(rest.reshape(batch, seq, 4 * d), conv_w[l], row(conv_b[l]),
                    lru_wa[l].astype(BF16), row(lru_ba[l]), lru_wx[l].astype(BF16), row(lru_bx[l]),
                    row(lru_lambda[l]), d=d, ts=256, cw=512)
        merged = _merge(attn_hm, hg.reshape(batch, i_len, CLASSES * d),
                        rest.reshape(batch, i_len, CLASSES * 4 * d),
                        w_o_attn[l].astype(BF16), w_o_lru[l].astype(BF16), batch=batch, tn=512)
        rw_hi, rw_lo, rb = _router_weights(router_group_w[l], router_group_b[l],
                                           router_expert_w[l], router_expert_b[l])
        x1, route = _out_ln_route(merged, x.reshape(batch, i_len, CLASSES * d), w_out[l].astype(BF16),
                                  row(ln1_g[l]), row(ln1_b[l]), rw_hi, rw_lo, rb, batch=batch, bsub=2)

        n_tiles = 2 * t // MOE_TILE + N_EXPERTS
        tile_expert, tile_valid, n_used, gidx, dest = _moe_plan(route, n_tiles)
        y_rows = _moe_experts(x1, tile_expert, tile_valid, n_used, gidx, dest,
                              w_e_gate[l].astype(BF16), w_e_up[l].astype(BF16), w_e_down[l].astype(BF16))
        x = _final_ln(x1, y_rows.reshape(t, 2 * d), route,
                      row(ln2_g[l]), row(ln2_b[l]), batch=batch, seq=seq, bsub=2)
    return x
```

```python
import functools
import math

import numpy as np
import jax
import jax.numpy as jnp
from jax import lax
from jax.experimental import pallas as pl
from jax.experimental.pallas import tpu as pltpu

F32 = jnp.float32
BF16 = jnp.bfloat16

N_HEADS = 16
HEAD_DIM = 128
N_BACK = 128
CLASSES = 16
CONV_WIDTH = 4
LRU_BLOCK_DIM = 128
LRU_C = 8.0
N_GROUPS = 4
EXPERTS_PER_GROUP = 8
N_EXPERTS = N_GROUPS * EXPERTS_PER_GROUP
DEPTH = 1
DEEPNORM_ALPHA = (2.0 * DEPTH) ** 0.25
LN_EPS = 1e-5
ATTN_SCALE = HEAD_DIM ** -0.5

LANES = 128
SUBLANES = 8
VMEM_LIMIT_BYTES = 56 * 1024 * 1024

MASK_BIAS = -1e30
MOE_TILE = 256
ROUTER_LANES = LANES


def _cparams(sem):
    return pltpu.CompilerParams(dimension_semantics=sem, vmem_limit_bytes=VMEM_LIMIT_BYTES)


def _qkv_kernel(x_ref, w_ref, o_ref):
    res = jnp.dot(x_ref[0], w_ref[...], preferred_element_type=F32).astype(o_ref.dtype)
    for hh in range(o_ref.shape[0]):
        o_ref[hh, 0] = res[:, hh * HEAD_DIM:(hh + 1) * HEAD_DIM]


def _qkv_proj(x_cls, w_qkv, *, tn):
    _, rows, d = x_cls.shape
    n = w_qkv.shape[1]
    heads_per_step = tn // HEAD_DIM
    return pl.pallas_call(
        _qkv_kernel,
        grid=(CLASSES, n // tn),
        in_specs=[pl.BlockSpec((1, rows, d), lambda r, j: (r, 0, 0)),
                  pl.BlockSpec((d, tn), lambda r, j: (0, j))],
        out_specs=pl.BlockSpec((heads_per_step, 1, rows, HEAD_DIM), lambda r, j: (j, r, 0, 0)),
        out_shape=jax.ShapeDtypeStruct((n // HEAD_DIM, CLASSES, rows, HEAD_DIM), BF16),
        compiler_params=_cparams(("parallel", "parallel")),
        name="qkv_proj",
    )(x_cls, w_qkv)


def _mm_kernel(x_ref, w_ref, o_ref):
    o_ref[...] = jnp.dot(x_ref[...], w_ref[...], preferred_element_type=F32).astype(o_ref.dtype)


def _rest_proj(x2d, w_rest, *, tm, tn):
    t, d = x2d.shape
    n = w_rest.shape[1]
    return pl.pallas_call(
        _mm_kernel,
        grid=(t // tm, n // tn),
        in_specs=[pl.BlockSpec((tm, d), lambda i, j: (i, 0)),
                  pl.BlockSpec((d, tn), lambda i, j: (0, j))],
        out_specs=pl.BlockSpec((tm, tn), lambda i, j: (i, j)),
        out_shape=jax.ShapeDtypeStruct((t, n), BF16),
        compiler_params=_cparams(("parallel", "parallel")),
        name="rest_proj",
    )(x2d, w_rest)


def _band_bias(dist):
    return np.where((dist >= 0) & (dist <= N_BACK), 0.0, MASK_BIAS).astype(np.float32)


def _attn_biases(i_len):
    iq = np.arange(i_len)[:, None]
    ik = np.arange(i_len)[None, :]
    b3 = _band_bias(iq - ik)
    cq = np.repeat(np.arange(4), 32)[:, None]
    lq = np.tile(np.arange(32), 4)[:, None]
    ck = np.repeat(np.arange(4), 64)[None, :]
    lk = np.tile(np.arange(64), 4)[None, :]
    b2 = np.stack([_band_bias(4 * (lq - lk) + (cq - ck)),
                   _band_bias(4 * (32 + lq - lk) + (cq - ck))])
    rq = np.repeat(np.arange(16), 16)[:, None]
    lq = np.tile(np.arange(16), 16)[:, None]
    rk = np.repeat(np.arange(16), 32)[None, :]
    lk = np.tile(np.arange(32), 16)[None, :]
    b1 = np.stack([_band_bias(16 * (lq - lk) + (rq - rk)),
                   _band_bias(16 * (16 + lq - lk) + (rq - rk))])
    return b1, b2, b3


def _softmax_block(q, k, v, bias, state):
    s = lax.dot_general(q, k, (((1,), (1,)), ((), ())), preferred_element_type=F32) * ATTN_SCALE + bias
    rows, n = s.shape
    chunks = [s[:, c * LANES:(c + 1) * LANES] for c in range(n // LANES)]
    mx = functools.reduce(jnp.maximum, chunks)
    m_blk = jnp.broadcast_to(jnp.max(mx, axis=1, keepdims=True), (rows, LANES))
    m_new = m_blk if state is None else jnp.maximum(state[0], m_blk)
    ps = [jnp.exp(c - m_new) for c in chunks]
    l_blk = jnp.broadcast_to(jnp.sum(functools.reduce(jnp.add, ps), axis=1, keepdims=True), (rows, LANES))
    pv = jnp.dot(jnp.concatenate(ps, axis=1).astype(BF16), v, preferred_element_type=F32)
    if state is None:
        return m_new, l_blk, pv
    alpha = jnp.exp(state[0] - m_new)
    return m_new, alpha * state[1] + l_blk, alpha * state[2] + pv


D16_PER_ITER = 8
D4_BLOCKS_PER_ITER = 2
D1_BLOCKS_PER_ITER = 4


def _attn_kernel(q_ref, k_ref, v_ref, b1_ref, b2_ref, b3_ref, o_ref, m_sc, l_sc, acc_sc):
    i_len = q_ref.shape[2]

    def d16_body(it, carry):
        for u in range(D16_PER_ITER):
            r = it * D16_PER_ITER + u
            m, l, acc = _softmax_block(q_ref[0, r], k_ref[0, r], v_ref[0, r], b3_ref[...], None)
            m_sc[r] = m
            l_sc[r] = l
            acc_sc[r] = acc
        return carry

    lax.fori_loop(0, CLASSES // D16_PER_ITER, d16_body, 0)

    def d4_body(it, carry):
        for u in range(D4_BLOCKS_PER_ITER):
            n = it * D4_BLOCKS_PER_ITER + u
            q0 = pl.multiple_of(n * 32, 32)
            k0 = pl.multiple_of(jnp.maximum(n - 1, 0) * 32, 32)
            bias = b2_ref[jnp.minimum(n, 1)]
            for r4 in range(4):
                cls = [r4 + 4 * c for c in range(4)]
                cat = lambda ref, start, size: jnp.concatenate(
                    [ref[r, pl.ds(start, size), :] for r in cls], axis=0)
                q = jnp.concatenate([q_ref[0, r, pl.ds(q0, 32), :] for r in cls], axis=0)
                k = jnp.concatenate([k_ref[0, r, pl.ds(k0, 64), :] for r in cls], axis=0)
                v = jnp.concatenate([v_ref[0, r, pl.ds(k0, 64), :] for r in cls], axis=0)
                state = (cat(m_sc, q0, 32), cat(l_sc, q0, 32), cat(acc_sc, q0, 32))
                m, l, acc = _softmax_block(q, k, v, bias, state)
                for c, r in enumerate(cls):
                    m_sc[r, pl.ds(q0, 32), :] = m[32 * c:32 * (c + 1)]
                    l_sc[r, pl.ds(q0, 32), :] = l[32 * c:32 * (c + 1)]
                    acc_sc[r, pl.ds(q0, 32), :] = acc[32 * c:32 * (c + 1)]
        return carry

    lax.fori_loop(0, i_len // 32 // D4_BLOCKS_PER_ITER, d4_body, 0)

    def d1_body(it, carry):
        for u in range(D1_BLOCKS_PER_ITER):
            mb = it * D1_BLOCKS_PER_ITER + u
            q0 = pl.multiple_of(mb * 16, 16)
            k0 = pl.multiple_of(jnp.maximum(mb - 1, 0) * 16, 16)
            q = q_ref[0, :, pl.ds(q0, 16), :].reshape(CLASSES * 16, HEAD_DIM)
            k = k_ref[0, :, pl.ds(k0, 32), :].reshape(CLASSES * 32, HEAD_DIM)
            v = v_ref[0, :, pl.ds(k0, 32), :].reshape(CLASSES * 32, HEAD_DIM)
            flat = lambda ref: ref[:, pl.ds(q0, 16), :].reshape(CLASSES * 16, LANES)
            m, l, acc = _softmax_block(q, k, v, b1_ref[jnp.minimum(mb, 1)],
                                       (flat(m_sc), flat(l_sc), flat(acc_sc)))
            m_sc[:, pl.ds(q0, 16), :] = m.reshape(CLASSES, 16, LANES)
            l_sc[:, pl.ds(q0, 16), :] = l.reshape(CLASSES, 16, LANES)
            acc_sc[:, pl.ds(q0, 16), :] = acc.reshape(CLASSES, 16, LANES)
        return carry

    lax.fori_loop(0, i_len // 16 // D1_BLOCKS_PER_ITER, d1_body, 0)

    o_ref[0] = (acc_sc[...] / l_sc[...]).astype(o_ref.dtype)


def _dilated_attention(qkv_hm, batch):
    n3, _, bi, _ = qkv_hm.shape
    h = n3 // 3
    i_len = bi // batch
    b1, b2, b3 = (jnp.asarray(a) for a in _attn_biases(i_len))
    blk = (1, CLASSES, i_len, HEAD_DIM)
    const = lambda a: pl.BlockSpec(a.shape, lambda hh, bb: (0,) * a.ndim)
    return pl.pallas_call(
        _attn_kernel,
        grid=(h, batch),
        in_specs=[pl.BlockSpec(blk, lambda hh, bb: (hh, 0, bb, 0)),
                  pl.BlockSpec(blk, lambda hh, bb: (h + hh, 0, bb, 0)),
                  pl.BlockSpec(blk, lambda hh, bb: (2 * h + hh, 0, bb, 0)),
                  const(b1), const(b2), const(b3)],
        out_specs=pl.BlockSpec(blk, lambda hh, bb: (hh, 0, bb, 0)),
        out_shape=jax.ShapeDtypeStruct((h, CLASSES, bi, HEAD_DIM), BF16),
        scratch_shapes=[pltpu.VMEM((CLASSES, i_len, LANES), F32),
                        pltpu.VMEM((CLASSES, i_len, LANES), F32),
                        pltpu.VMEM((CLASSES, i_len, HEAD_DIM), F32)],
        compiler_params=_cparams(("parallel", "parallel")),
        name="dilated_attn",
    )(qkv_hm, qkv_hm, qkv_hm, b1, b2, b3)


def _softplus(z):
    return jnp.maximum(z, 0.0) + jnp.log1p(jnp.exp(-jnp.abs(z)))


def _lru_kernel(xr_ref, xg_ref, cw_ref, cb_ref, wa_ref, ba_ref, wx_ref, bx_ref, lam_ref,
                o_ref, h_sc, tail_sc):
    ts = xr_ref.shape[1]
    cw = xr_ref.shape[2]
    nv = ts // SUBLANES

    @pl.when(pl.program_id(2) == 0)
    def _():
        h_sc[...] = jnp.zeros_like(h_sc)
        tail_sc[...] = jnp.zeros_like(tail_sc)

    row = lax.broadcasted_iota(jnp.int32, (nv, SUBLANES, LANES), 1)
    for g in range(cw // LANES):
        sl = slice(g * LANES, (g + 1) * LANES)
        xr = xr_ref[0, :, sl].astype(F32)
        xfull = jnp.concatenate([tail_sc[:, sl], xr], axis=0)
        taps = cw_ref[:, sl]
        xc = xr * taps[3:4]
        for j in range(1, CONV_WIDTH):
            shifted = pltpu.roll(xfull, j, axis=0)[SUBLANES:]
            xc = xc + shifted * taps[3 - j:4 - j]
        xc = xc + cb_ref[:, sl]
        tail_sc[:, sl] = xr[ts - SUBLANES:]

        xcb = xc.astype(BF16)
        gate_a = jnp.dot(xcb, wa_ref[g], preferred_element_type=F32) + ba_ref[:, sl]
        gate_x = jnp.dot(xcb, wx_ref[g], preferred_element_type=F32) + bx_ref[:, sl]
        r = jax.nn.sigmoid(gate_a)
        i = jax.nn.sigmoid(gate_x)
        log_a = (-LRU_C * _softplus(-lam_ref[:, sl])) * r
        a = jnp.exp(log_a)
        u = jnp.sqrt(1.0 - a * a) * (i * xc)

        a3 = a.reshape(nv, SUBLANES, LANES)
        b3 = u.reshape(nv, SUBLANES, LANES)
        for k in (1, 2, 4):
            a_sh = jnp.where(row >= k, pltpu.roll(a3, k, axis=1), 1.0)
            b_sh = jnp.where(row >= k, pltpu.roll(b3, k, axis=1), 0.0)
            b3 = a3 * b_sh + b3
            a3 = a3 * a_sh
        h = h_sc[:, sl]
        outs = []
        for vi in range(nv):
            hv = a3[vi] * h + b3[vi]
            outs.append(hv)
            h = hv[SUBLANES - 1:SUBLANES]
        h_sc[:, sl] = h
        hs = jnp.concatenate(outs, axis=0)
        o_ref[0, :, sl] = (hs * jax.nn.gelu(xg_ref[0, :, sl].astype(F32))).astype(o_ref.dtype)


def _rglru(rest3, conv_w, conv_b, wa, ba, wx, bx, lam, *, d, ts, cw):
    b, s, _ = rest3.shape
    ncw = d // cw
    gpb = cw // LRU_BLOCK_DIM
    vec = lambda: pl.BlockSpec((1, cw), lambda bb, c, t: (0, c))
    return pl.pallas_call(
        _lru_kernel,
        grid=(b, ncw, s // ts),
        in_specs=[pl.BlockSpec((1, ts, cw), lambda bb, c, t: (bb, t, c)),
                  pl.BlockSpec((1, ts, cw), lambda bb, c, t: (bb, t, ncw + c)),
                  pl.BlockSpec((CONV_WIDTH, cw), lambda bb, c, t: (0, c)),
                  vec(),
                  pl.BlockSpec((gpb, LRU_BLOCK_DIM, LRU_BLOCK_DIM), lambda bb, c, t: (c, 0, 0)),
                  vec(),
                  pl.BlockSpec((gpb, LRU_BLOCK_DIM, LRU_BLOCK_DIM), lambda bb, c, t: (c, 0, 0)),
                  vec(), vec()],
        out_specs=pl.BlockSpec((1, ts, cw), lambda bb, c, t: (bb, t, c)),
        out_shape=jax.ShapeDtypeStruct((b, s, d), BF16),
        scratch_shapes=[pltpu.VMEM((1, cw), F32), pltpu.VMEM((SUBLANES, cw), F32)],
        compiler_params=_cparams(("parallel", "parallel", "arbitrary")),
        name="rglru",
    )(rest3, rest3, conv_w, conv_b, wa, ba, wx, bx, lam)


def _merge_kernel(a_ref, h_ref, ga_ref, gl_ref, wa_ref, wl_ref, o_ref):
    y_attn = jnp.dot(a_ref[...], wa_ref[...], preferred_element_type=F32)
    y_lru = jnp.dot(h_ref[...], wl_ref[...], preferred_element_type=F32)
    g_a = jax.nn.sigmoid(ga_ref[...].astype(F32))
    g_l = jax.nn.sigmoid(gl_ref[...].astype(F32))
    o_ref[...] = (g_a * y_attn + g_l * y_lru).astype(o_ref.dtype)


def _merge(attn, hg, rest, w_o_attn, w_o_lru, *, tm, tn):
    t, aw = attn.shape
    d = w_o_attn.shape[1]
    nj = d // tn
    return pl.pallas_call(
        _merge_kernel,
        grid=(t // tm, nj),
        in_specs=[pl.BlockSpec((tm, aw), lambda i, j: (i, 0)),
                  pl.BlockSpec((tm, d), lambda i, j: (i, 0)),
                  pl.BlockSpec((tm, tn), lambda i, j: (i, 2 * nj + j)),
                  pl.BlockSpec((tm, tn), lambda i, j: (i, 3 * nj + j)),
                  pl.BlockSpec((aw, tn), lambda i, j: (0, j)),
                  pl.BlockSpec((d, tn), lambda i, j: (0, j))],
        out_specs=pl.BlockSpec((tm, tn), lambda i, j: (i, j)),
        out_shape=jax.ShapeDtypeStruct((t, d), BF16),
        compiler_params=_cparams(("parallel", "parallel")),
        name="merge",
    )(attn, hg, rest, rest, w_o_attn, w_o_lru)


def _layer_norm(z, g, b):
    mu = jnp.mean(z, axis=-1, keepdims=True)
    zc = z - mu
    var = jnp.mean(zc * zc, axis=-1, keepdims=True)
    return zc * lax.rsqrt(var + LN_EPS) * g + b


def _out_ln_route_kernel(m_ref, x_ref, w_ref, g_ref, b_ref, rwh_ref, rwl_ref, rb_ref,
                         x1_ref, route_ref):
    y = jnp.dot(m_ref[...], w_ref[...], preferred_element_type=F32)
    z = DEEPNORM_ALPHA * x_ref[...] + y
    x1 = _layer_norm(z, g_ref[...], b_ref[...])
    x1_ref[...] = x1

    hi = x1.astype(BF16)
    lo = (x1 - hi.astype(F32)).astype(BF16)
    lg = (jnp.dot(hi, rwh_ref[...], preferred_element_type=F32)
          + jnp.dot(lo, rwh_ref[...], preferred_element_type=F32)
          + jnp.dot(hi, rwl_ref[...], preferred_element_type=F32)) + rb_ref[...]

    lane = lax.broadcasted_iota(jnp.int32, lg.shape, 1).astype(F32)
    first = lambda hit: jnp.min(jnp.where(hit, lane, float(ROUTER_LANES)), axis=1, keepdims=True)
    in_grp = lane < N_GROUPS
    lgg = jnp.where(in_grp, lg, MASK_BIAS)
    mg = jnp.max(lgg, axis=1, keepdims=True)
    g_idx = first(lgg == mg)
    p_grp = 1.0 / jnp.sum(jnp.where(in_grp, jnp.exp(lg - mg), 0.0), axis=1, keepdims=True)
    lo_lane = N_GROUPS + EXPERTS_PER_GROUP * g_idx
    in_exp = (lane >= lo_lane) & (lane < lo_lane + EXPERTS_PER_GROUP)
    le = jnp.where(in_exp, lg, MASK_BIAS)
    v1 = jnp.max(le, axis=1, keepdims=True)
    i1 = first((le == v1) & in_exp)
    rest = in_exp & (lane != i1)
    le2 = jnp.where(rest, lg, MASK_BIAS)
    v2 = jnp.max(le2, axis=1, keepdims=True)
    i2 = first((le2 == v2) & rest)
    t = jnp.exp(v2 - v1)
    w1 = p_grp / (1.0 + t)
    w2 = p_grp * t / (1.0 + t)
    route_ref[...] = jnp.where(lane == 0.0, i1 - N_GROUPS,
                     jnp.where(lane == 1.0, i2 - N_GROUPS,
                     jnp.where(lane == 2.0, w1,
                     jnp.where(lane == 3.0, w2, 0.0))))


def _out_ln_route(merged, x2d, w_out, ln_g, ln_b, rw_hi, rw_lo, rb, *, tm):
    t, d = merged.shape
    const = lambda a: pl.BlockSpec(a.shape, lambda i: (0,) * a.ndim)
    return pl.pallas_call(
        _out_ln_route_kernel,
        grid=(t // tm,),
        in_specs=[pl.BlockSpec((tm, d), lambda i: (i, 0)),
                  pl.BlockSpec((tm, d), lambda i: (i, 0)),
                  const(w_out), const(ln_g), const(ln_b), const(rw_hi), const(rw_lo), const(rb)],
        out_specs=[pl.BlockSpec((tm, d), lambda i: (i, 0)),
                   pl.BlockSpec((tm, ROUTER_LANES), lambda i: (i, 0))],
        out_shape=[jax.ShapeDtypeStruct((t, d), F32),
                   jax.ShapeDtypeStruct((t, ROUTER_LANES), F32)],
        compiler_params=_cparams(("parallel",)),
        name="out_ln_route",
    )(merged, x2d, w_out, ln_g, ln_b, rw_hi, rw_lo, rb)


def _moe_kernel(te_ref, tv_ref, nu_ref, g0_ref, gn_ref, dst_ref, x_hbm, wg_ref, wu_ref, wd_ref,
                y_hbm, xbuf, ybuf, gsem, ssem):
    i = pl.program_id(0)
    n_used = nu_ref[0]
    slot = lax.rem(i, 2)
    tm = xbuf.shape[1]

    def gather_copy(tok, j, s):
        return pltpu.make_async_copy(x_hbm.at[pl.ds(tok, 1)], xbuf.at[s, pl.ds(j, 1)], gsem.at[s])

    def scatter_copy(row, j, s):
        return pltpu.make_async_copy(ybuf.at[s, pl.ds(j, 1)], y_hbm.at[pl.ds(row, 1)], ssem.at[s])

    def start_gather(idx_ref, s):
        def body(j, c):
            gather_copy(idx_ref[0, 0, j], j, s).start()
            return c
        lax.fori_loop(0, tm, body, 0, unroll=8)

    def wait_gather(s):
        def body(j, c):
            gather_copy(0, j, s).wait()
            return c
        lax.fori_loop(0, tm, body, 0, unroll=8)

    def start_scatter(s, n_rows):
        def body(j, c):
            scatter_copy(dst_ref[0, 0, j], j, s).start()
            return c
        lax.fori_loop(0, n_rows, body, 0)

    def wait_scatter(s, n_rows):
        def body(j, c):
            scatter_copy(0, j, s).wait()
            return c
        lax.fori_loop(0, n_rows, body, 0)

    @pl.when(i == 0)
    def _():
        start_gather(g0_ref, 0)

    @pl.when(i + 1 < n_used)
    def _():
        start_gather(gn_ref, 1 - slot)

    @pl.when(i < n_used)
    def _():
        wait_gather(slot)
        xb = xbuf[slot].astype(BF16)
        gate = jnp.dot(xb, wg_ref[0], preferred_element_type=F32)
        up = jnp.dot(xb, wu_ref[0], preferred_element_type=F32)
        he = (jax.nn.silu(gate) * up).astype(BF16)
        ybuf[slot] = jnp.dot(he, wd_ref[0], preferred_element_type=F32)
        start_scatter(slot, tv_ref[i])

        @pl.when(i >= 1)
        def _():
            wait_scatter(1 - slot, tv_ref[jnp.maximum(i - 1, 0)])

        @pl.when(i == n_used - 1)
        def _():
            wait_scatter(slot, tv_ref[i])


def _moe_experts(x1, tile_expert, tile_valid, n_used, gidx, dest, wg, wu, wd):
    t, d = x1.shape
    nt = gidx.shape[0]
    f = wg.shape[2]
    idx_spec = lambda fn: pl.BlockSpec((1, 1, MOE_TILE), fn, memory_space=pltpu.SMEM)
    grid_spec = pltpu.PrefetchScalarGridSpec(
        num_scalar_prefetch=3,
        grid=(nt,),
        in_specs=[idx_spec(lambda i, te, tv, nu: (0, 0, 0)),
                  idx_spec(lambda i, te, tv, nu: (jnp.minimum(i + 1, nt - 1), 0, 0)),
                  idx_spec(lambda i, te, tv, nu: (i, 0, 0)),
                  pl.BlockSpec(memory_space=pl.ANY),
                  pl.BlockSpec((1, d, f), lambda i, te, tv, nu: (te[i], 0, 0)),
                  pl.BlockSpec((1, d, f), lambda i, te, tv, nu: (te[i], 0, 0)),
                  pl.BlockSpec((1, f, d), lambda i, te, tv, nu: (te[i], 0, 0))],
        out_specs=pl.BlockSpec(memory_space=pl.ANY),
        scratch_shapes=[pltpu.VMEM((2, MOE_TILE, d), F32),
                        pltpu.VMEM((2, MOE_TILE, d), F32),
                        pltpu.SemaphoreType.DMA((2,)),
                        pltpu.SemaphoreType.DMA((2,))],
    )
    return pl.pallas_call(
        _moe_kernel,
        grid_spec=grid_spec,
        out_shape=jax.ShapeDtypeStruct((2 * t, d), F32),
        compiler_params=_cparams(("arbitrary",)),
        name="moe_experts",
    )(tile_expert, tile_valid, n_used, gidx, gidx, dest, x1, wg, wu, wd)


def _moe_plan(route, n_tiles):
    t = route.shape[0]
    p = 2 * t
    e_flat = jnp.concatenate([route[:, 0], route[:, 1]]).astype(jnp.int32)
    skeys = jnp.sort(e_flat * p + jnp.arange(p, dtype=jnp.int32))
    sorted_pair = jnp.concatenate([skeys % p, jnp.zeros((MOE_TILE,), jnp.int32)])
    experts = jnp.arange(N_EXPERTS, dtype=jnp.int32)
    counts = jnp.sum((e_flat[:, None] == experts[None, :]).astype(jnp.int32), axis=0)
    tiles_per = (counts + MOE_TILE - 1) // MOE_TILE
    tile_end = jnp.cumsum(tiles_per)
    tile_start = tile_end - tiles_per
    row_start = jnp.cumsum(counts) - counts
    n_used = tile_end[-1]
    tile_ids = jnp.arange(n_tiles, dtype=jnp.int32)
    tile_expert = jnp.sum((tile_ids[:, None] >= tile_end[None, :]).astype(jnp.int32), axis=1)
    tile_expert = jnp.minimum(tile_expert, N_EXPERTS - 1)
    onehot = (tile_expert[:, None] == experts[None, :]).astype(jnp.int32)
    of_tile = lambda per_expert: jnp.sum(onehot * per_expert[None, :], axis=1)
    rank0 = (tile_ids - of_tile(tile_start)) * MOE_TILE
    tile_valid = jnp.where(tile_ids < n_used, jnp.clip(of_tile(counts) - rank0, 0, MOE_TILE), 0)
    pos0 = jnp.clip(of_tile(row_start) + rank0, 0, p)
    windows = jax.vmap(lambda s: lax.dynamic_slice(sorted_pair, (s,), (MOE_TILE,)))(pos0)
    valid = jnp.arange(MOE_TILE, dtype=jnp.int32)[None, :] < tile_valid[:, None]
    dest = jnp.where(valid, windows, 0)
    gidx = jnp.where(dest >= t, dest - t, dest)
    shape3 = (n_tiles, 1, MOE_TILE)
    return tile_expert, tile_valid, n_used.reshape(1), gidx.reshape(shape3), dest.reshape(shape3)


def _final_kernel(x1_ref, y0_ref, y1_ref, route_ref, g_ref, b_ref, o_ref):
    y = route_ref[:, 2:3] * y0_ref[0] + route_ref[:, 3:4] * y1_ref[0]
    z = DEEPNORM_ALPHA * x1_ref[...] + y
    o_ref[...] = _layer_norm(z, g_ref[...], b_ref[...])


def _final_ln(x1, y_pairs, route, ln_g, ln_b, *, tm):
    t, d = x1.shape
    const = lambda a: pl.BlockSpec(a.shape, lambda i: (0,) * a.ndim)
    return pl.pallas_call(
        _final_kernel,
        grid=(t // tm,),
        in_specs=[pl.BlockSpec((tm, d), lambda i: (i, 0)),
                  pl.BlockSpec((1, tm, d), lambda i: (0, i, 0)),
                  pl.BlockSpec((1, tm, d), lambda i: (1, i, 0)),
                  pl.BlockSpec((tm, ROUTER_LANES), lambda i: (i, 0)),
                  const(ln_g), const(ln_b)],
        out_specs=pl.BlockSpec((tm, d), lambda i: (i, 0)),
        out_shape=jax.ShapeDtypeStruct((t, d), F32),
        compiler_params=_cparams(("parallel",)),
        name="final_ln",
    )(x1, y_pairs, y_pairs, route, ln_g, ln_b)


def _router_weights(router_group_w, router_group_b, router_expert_w, router_expert_b):
    d = router_group_w.shape[0]
    pad = ROUTER_LANES - N_GROUPS - N_EXPERTS
    w = jnp.concatenate([router_group_w, router_expert_w, jnp.zeros((d, pad), F32)], axis=1)
    b = jnp.concatenate([router_group_b, router_expert_b, jnp.zeros((pad,), F32)]).reshape(1, ROUTER_LANES)
    w_hi = w.astype(BF16)
    w_lo = (w - w_hi.astype(F32)).astype(BF16)
    return w_hi, w_lo, b


def kernel(x, w_in, conv_w, conv_b, lru_wa, lru_ba, lru_wx, lru_bx, lru_lambda, w_o_attn, w_o_lru, w_out, ln1_g, ln1_b, router_group_w, router_group_b, router_expert_w, router_expert_b, w_e_gate, w_e_up, w_e_down, ln2_g, ln2_b):
    batch, seq, d = x.shape
    i_len = seq // CLASSES
    t = batch * seq
    attn_w = N_HEADS * HEAD_DIM
    for l in range(DEPTH):
        x2d = x.reshape(t, d)
        x_bf = x2d.astype(BF16)
        x_cls = x.reshape(batch, i_len, CLASSES, d).transpose(2, 0, 1, 3).astype(BF16)
        x_cls = x_cls.reshape(CLASSES, batch * i_len, d)
        row = lambda a: a.reshape(1, -1)

        qkv_hm = _qkv_proj(x_cls, w_in[l, :, :3 * attn_w].astype(BF16), tn=1024)
        rest = _rest_proj(x_bf, w_in[l, :, 3 * attn_w:].astype(BF16), tm=1024, tn=1024)
        attn_hm = _dilated_attention(qkv_hm, batch)
        attn = attn_hm.reshape(N_HEADS, CLASSES, batch, i_len, HEAD_DIM)
        attn = attn.transpose(2, 3, 1, 0, 4).reshape(t, attn_w)
        hg = _rglru(rest.reshape(batch, seq, 4 * d), conv_w[l], row(conv_b[l]),
                    lru_wa[l].astype(BF16), row(lru_ba[l]), lru_wx[l].astype(BF16), row(lru_bx[l]),
                    row(lru_lambda[l]), d=d, ts=256, cw=512)
        merged = _merge(attn, hg.reshape(t, d), rest, w_o_attn[l].astype(BF16), w_o_lru[l].astype(BF16),
                        tm=1024, tn=512)
        rw_hi, rw_lo, rb = _router_weights(router_group_w[l], router_group_b[l],
                                           router_expert_w[l], router_expert_b[l])
        x1, route = _out_ln_route(merged, x2d, w_out[l].astype(BF16),
                                  row(ln1_g[l]), row(ln1_b[l]), rw_hi, rw_lo, rb, tm=512)

        n_tiles = 2 * t // MOE_TILE + N_EXPERTS
        tile_expert, tile_valid, n_used, gidx, dest = _moe_plan(route, n_tiles)
        y_rows = _moe_experts(x1, tile_expert, tile_valid, n_used, gidx, dest,
                              w_e_gate[l].astype(BF16), w_e_up[l].astype(BF16), w_e_down[l].astype(BF16))
        x = _final_ln(x1, y_rows.reshape(2, t, d), route, row(ln2_g[l]), row(ln2_b[l]), tm=512)
        x = x.reshape(batch, seq, d)
    return x
```

```python
import functools
import math

import numpy as np
import jax
import jax.numpy as jnp
from jax import lax
from jax.experimental import pallas as pl
from jax.experimental.pallas import tpu as pltpu

F32 = jnp.float32
BF16 = jnp.bfloat16

N_HEADS = 16
HEAD_DIM = 128
N_BACK = 128
CLASSES = 16
CONV_WIDTH = 4
LRU_BLOCK_DIM = 128
LRU_C = 8.0
N_GROUPS = 4
EXPERTS_PER_GROUP = 8
N_EXPERTS = N_GROUPS * EXPERTS_PER_GROUP
DEPTH = 1
DEEPNORM_ALPHA = (2.0 * DEPTH) ** 0.25
LN_EPS = 1e-5
ATTN_SCALE_LOG2 = HEAD_DIM ** -0.5 * math.log2(math.e)

LANES = 128
SUBLANES = 8
VMEM_LIMIT_BYTES = 56 * 1024 * 1024

MASK_BIAS = -1e30
MOE_TILE = 256
ROUTER_LANES = LANES


def _cparams(sem):
    return pltpu.CompilerParams(dimension_semantics=sem, vmem_limit_bytes=VMEM_LIMIT_BYTES)


def _qkv_kernel(x_ref, w_ref, o_ref, w_sc):
    @pl.when(pl.program_id(1) == 0)
    def _():
        w_sc[...] = w_ref[...].astype(BF16)

    res = jnp.dot(x_ref[0], w_sc[...], preferred_element_type=F32).astype(o_ref.dtype)
    for hh in range(o_ref.shape[0]):
        o_ref[hh, 0] = res[:, hh * HEAD_DIM:(hh + 1) * HEAD_DIM]


def _qkv_proj(x_cls, w_in, *, n, tn):
    _, rows, d = x_cls.shape
    heads_per_step = tn // HEAD_DIM
    return pl.pallas_call(
        _qkv_kernel,
        grid=(n // tn, CLASSES),
        in_specs=[pl.BlockSpec((1, rows, d), lambda j, r: (r, 0, 0)),
                  pl.BlockSpec((d, tn), lambda j, r: (0, j))],
        out_specs=pl.BlockSpec((heads_per_step, 1, rows, HEAD_DIM), lambda j, r: (j, r, 0, 0)),
        out_shape=jax.ShapeDtypeStruct((n // HEAD_DIM, CLASSES, rows, HEAD_DIM), BF16),
        scratch_shapes=[pltpu.VMEM((d, tn), BF16)],
        compiler_params=_cparams(("parallel", "arbitrary")),
        name="qkv_proj",
    )(x_cls, w_in)


def _rest_kernel(x_ref, w_ref, o_ref, w_sc):
    @pl.when(pl.program_id(1) == 0)
    def _():
        w_sc[...] = w_ref[...].astype(BF16)

    o_ref[...] = jnp.dot(x_ref[...], w_sc[...], preferred_element_type=F32).astype(o_ref.dtype)


def _rest_proj(x2d, w_in, *, col0, tm, tn):
    t, d = x2d.shape
    n = w_in.shape[1] - col0
    j0 = col0 // tn
    return pl.pallas_call(
        _rest_kernel,
        grid=(n // tn, t // tm),
        in_specs=[pl.BlockSpec((tm, d), lambda j, i: (i, 0)),
                  pl.BlockSpec((d, tn), lambda j, i: (0, j0 + j))],
        out_specs=pl.BlockSpec((tm, tn), lambda j, i: (i, j)),
        out_shape=jax.ShapeDtypeStruct((t, n), BF16),
        scratch_shapes=[pltpu.VMEM((d, tn), BF16)],
        compiler_params=_cparams(("parallel", "arbitrary")),
        name="rest_proj",
    )(x2d, w_in)


def _band_bias(dist):
    return np.where((dist >= 0) & (dist <= N_BACK), 0.0, MASK_BIAS).astype(np.float32)


def _attn_biases(i_len):
    iq = np.arange(i_len)[:, None]
    ik = np.arange(i_len)[None, :]
    b3 = _band_bias(iq - ik)
    cq = np.repeat(np.arange(4), 32)[:, None]
    lq = np.tile(np.arange(32), 4)[:, None]
    ck = np.repeat(np.arange(4), 64)[None, :]
    lk = np.tile(np.arange(64), 4)[None, :]
    b2 = np.stack([_band_bias(4 * (lq - lk) + (cq - ck)),
                   _band_bias(4 * (32 + lq - lk) + (cq - ck))])
    rq = np.repeat(np.arange(16), 16)[:, None]
    lq = np.tile(np.arange(16), 16)[:, None]
    rk = np.repeat(np.arange(16), 32)[None, :]
    lk = np.tile(np.arange(32), 16)[None, :]
    b1 = np.stack([_band_bias(16 * (lq - lk) + (rq - rk)),
                   _band_bias(16 * (16 + lq - lk) + (rq - rk))])
    return b1, b2, b3


def _softmax_block(q, k, v, bias, state):
    s = lax.dot_general(q, k, (((1,), (1,)), ((), ())), preferred_element_type=F32) * ATTN_SCALE_LOG2 + bias
    rows, n = s.shape
    chunks = [s[:, c * LANES:(c + 1) * LANES] for c in range(n // LANES)]
    mx = functools.reduce(jnp.maximum, chunks)
    m_blk = jnp.broadcast_to(jnp.max(mx, axis=1, keepdims=True), (rows, LANES))
    m_new = m_blk if state is None else jnp.maximum(state[0], m_blk)
    ps = [jnp.exp2(c - m_new) for c in chunks]
    l_blk = jnp.broadcast_to(jnp.sum(functools.reduce(jnp.add, ps), axis=1, keepdims=True), (rows, LANES))
    pv = jnp.dot(jnp.concatenate(ps, axis=1).astype(BF16), v, preferred_element_type=F32)
    if state is None:
        return m_new, l_blk, pv
    alpha = jnp.exp2(state[0] - m_new)
    return m_new, alpha * state[1] + l_blk, alpha * state[2] + pv


D16_PER_ITER = 8
D4_BLOCKS_PER_ITER = 2
D1_BLOCKS_PER_ITER = 4


def _attn_kernel(q_ref, k_ref, v_ref, b1_ref, b2_ref, b3_ref, o_ref, m_sc, l_sc, acc_sc):
    i_len = q_ref.shape[2]

    def d16_body(it, carry):
        for u in range(D16_PER_ITER):
            r = it * D16_PER_ITER + u
            m, l, acc = _softmax_block(q_ref[0, r], k_ref[0, r], v_ref[0, r], b3_ref[...], None)
            m_sc[r] = m
            l_sc[r] = l
            acc_sc[r] = acc
        return carry

    lax.fori_loop(0, CLASSES // D16_PER_ITER, d16_body, 0)

    def d4_body(it, carry):
        for u in range(D4_BLOCKS_PER_ITER):
            n = it * D4_BLOCKS_PER_ITER + u
            q0 = pl.multiple_of(n * 32, 32)
            k0 = pl.multiple_of(jnp.maximum(n - 1, 0) * 32, 32)
            bias = b2_ref[jnp.minimum(n, 1)]
            for r4 in range(4):
                cls = [r4 + 4 * c for c in range(4)]
                cat = lambda ref, start, size: jnp.concatenate(
                    [ref[r, pl.ds(start, size), :] for r in cls], axis=0)
                q = jnp.concatenate([q_ref[0, r, pl.ds(q0, 32), :] for r in cls], axis=0)
                k = jnp.concatenate([k_ref[0, r, pl.ds(k0, 64), :] for r in cls], axis=0)
                v = jnp.concatenate([v_ref[0, r, pl.ds(k0, 64), :] for r in cls], axis=0)
                state = (cat(m_sc, q0, 32), cat(l_sc, q0, 32), cat(acc_sc, q0, 32))
                m, l, acc = _softmax_block(q, k, v, bias, state)
                for c, r in enumerate(cls):
                    m_sc[r, pl.ds(q0, 32), :] = m[32 * c:32 * (c + 1)]
                    l_sc[r, pl.ds(q0, 32), :] = l[32 * c:32 * (c + 1)]
                    acc_sc[r, pl.ds(q0, 32), :] = acc[32 * c:32 * (c + 1)]
        return carry

    lax.fori_loop(0, i_len // 32 // D4_BLOCKS_PER_ITER, d4_body, 0)

    def d1_body(it, carry):
        for u in range(D1_BLOCKS_PER_ITER):
            mb = it * D1_BLOCKS_PER_ITER + u
            q0 = pl.multiple_of(mb * 16, 16)
            k0 = pl.multiple_of(jnp.maximum(mb - 1, 0) * 16, 16)
            q = q_ref[0, :, pl.ds(q0, 16), :].reshape(CLASSES * 16, HEAD_DIM)
            k = k_ref[0, :, pl.ds(k0, 32), :].reshape(CLASSES * 32, HEAD_DIM)
            v = v_ref[0, :, pl.ds(k0, 32), :].reshape(CLASSES * 32, HEAD_DIM)
            flat = lambda ref: ref[:, pl.ds(q0, 16), :].reshape(CLASSES * 16, LANES)
            m, l, acc = _softmax_block(q, k, v, b1_ref[jnp.minimum(mb, 1)],
                                       (flat(m_sc), flat(l_sc), flat(acc_sc)))
            m_sc[:, pl.ds(q0, 16), :] = m.reshape(CLASSES, 16, LANES)
            l_sc[:, pl.ds(q0, 16), :] = l.reshape(CLASSES, 16, LANES)
            acc_sc[:, pl.ds(q0, 16), :] = acc.reshape(CLASSES, 16, LANES)
        return carry

    lax.fori_loop(0, i_len // 16 // D1_BLOCKS_PER_ITER, d1_body, 0)

    o_ref[0] = (acc_sc[...] / l_sc[...]).astype(o_ref.dtype)


def _dilated_attention(qkv_hm, batch):
    n3, _, bi, _ = qkv_hm.shape
    h = n3 // 3
    i_len = bi // batch
    b1, b2, b3 = (jnp.asarray(a) for a in _attn_biases(i_len))
    blk = (1, CLASSES, i_len, HEAD_DIM)
    const = lambda a: pl.BlockSpec(a.shape, lambda hh, bb: (0,) * a.ndim)
    return pl.pallas_call(
        _attn_kernel,
        grid=(h, batch),
        in_specs=[pl.BlockSpec(blk, lambda hh, bb: (hh, 0, bb, 0)),
                  pl.BlockSpec(blk, lambda hh, bb: (h + hh, 0, bb, 0)),
                  pl.BlockSpec(blk, lambda hh, bb: (2 * h + hh, 0, bb, 0)),
                  const(b1), const(b2), const(b3)],
        out_specs=pl.BlockSpec(blk, lambda hh, bb: (hh, 0, bb, 0)),
        out_shape=jax.ShapeDtypeStruct((h, CLASSES, bi, HEAD_DIM), BF16),
        scratch_shapes=[pltpu.VMEM((CLASSES, i_len, LANES), F32),
                        pltpu.VMEM((CLASSES, i_len, LANES), F32),
                        pltpu.VMEM((CLASSES, i_len, HEAD_DIM), F32)],
        compiler_params=_cparams(("parallel", "parallel")),
        name="dilated_attn",
    )(qkv_hm, qkv_hm, qkv_hm, b1, b2, b3)


def _softplus(z):
    return jnp.maximum(z, 0.0) + jnp.log1p(jnp.exp(-jnp.abs(z)))


def _lru_kernel(xr_ref, xg_ref, cw_ref, cb_ref, wa_ref, ba_ref, wx_ref, bx_ref, lam_ref,
                o_ref, h_sc, tail_sc):
    ts = xr_ref.shape[1]
    cw = xr_ref.shape[2]
    nv = ts // SUBLANES

    @pl.when(pl.program_id(2) == 0)
    def _():
        h_sc[...] = jnp.zeros_like(h_sc)
        tail_sc[...] = jnp.zeros_like(tail_sc)

    row = lax.broadcasted_iota(jnp.int32, (nv, SUBLANES, LANES), 1)
    for g in range(cw // LANES):
        sl = slice(g * LANES, (g + 1) * LANES)
        xr = xr_ref[0, :, sl].astype(F32)
        xfull = jnp.concatenate([tail_sc[:, sl], xr], axis=0)
        taps = cw_ref[:, sl]
        xc = xr * taps[3:4]
        for j in range(1, CONV_WIDTH):
            shifted = pltpu.roll(xfull, j, axis=0)[SUBLANES:]
            xc = xc + shifted * taps[3 - j:4 - j]
        xc = xc + cb_ref[:, sl]
        tail_sc[:, sl] = xr[ts - SUBLANES:]

        xcb = xc.astype(BF16)
        gate_a = jnp.dot(xcb, wa_ref[g], preferred_element_type=F32) + ba_ref[:, sl]
        gate_x = jnp.dot(xcb, wx_ref[g], preferred_element_type=F32) + bx_ref[:, sl]
        r = jax.nn.sigmoid(gate_a)
        i = jax.nn.sigmoid(gate_x)
        log_a = (-LRU_C * _softplus(-lam_ref[:, sl])) * r
        a = jnp.exp(log_a)
        u = jnp.sqrt(1.0 - a * a) * (i * xc)

        a3 = a.reshape(nv, SUBLANES, LANES)
        b3 = u.reshape(nv, SUBLANES, LANES)
        for k in (1, 2, 4):
            a_sh = jnp.where(row >= k, pltpu.roll(a3, k, axis=1), 1.0)
            b_sh = jnp.where(row >= k, pltpu.roll(b3, k, axis=1), 0.0)
            b3 = a3 * b_sh + b3
            a3 = a3 * a_sh
        h = h_sc[:, sl]
        outs = []
        for vi in range(nv):
            hv = a3[vi] * h + b3[vi]
            outs.append(hv)
            h = hv[SUBLANES - 1:SUBLANES]
        h_sc[:, sl] = h
        hs = jnp.concatenate(outs, axis=0)
        o_ref[0, :, sl] = (hs * jax.nn.gelu(xg_ref[0, :, sl].astype(F32))).astype(o_ref.dtype)


def _rglru(rest3, conv_w, conv_b, wa, ba, wx, bx, lam, *, d, ts, cw):
    b, s, _ = rest3.shape
    ncw = d // cw
    gpb = cw // LRU_BLOCK_DIM
    vec = lambda: pl.BlockSpec((1, cw), lambda bb, c, t: (0, c))
    return pl.pallas_call(
        _lru_kernel,
        grid=(b, ncw, s // ts),
        in_specs=[pl.BlockSpec((1, ts, cw), lambda bb, c, t: (bb, t, c)),
                  pl.BlockSpec((1, ts, cw), lambda bb, c, t: (bb, t, ncw + c)),
                  pl.BlockSpec((CONV_WIDTH, cw), lambda bb, c, t: (0, c)),
                  vec(),
                  pl.BlockSpec((gpb, LRU_BLOCK_DIM, LRU_BLOCK_DIM), lambda bb, c, t: (c, 0, 0)),
                  vec(),
                  pl.BlockSpec((gpb, LRU_BLOCK_DIM, LRU_BLOCK_DIM), lambda bb, c, t: (c, 0, 0)),
                  vec(), vec()],
        out_specs=pl.BlockSpec((1, ts, cw), lambda bb, c, t: (bb, t, c)),
        out_shape=jax.ShapeDtypeStruct((b, s, d), BF16),
        scratch_shapes=[pltpu.VMEM((1, cw), F32), pltpu.VMEM((SUBLANES, cw), F32)],
        compiler_params=_cparams(("parallel", "parallel", "arbitrary")),
        name="rglru",
    )(rest3, rest3, conv_w, conv_b, wa, ba, wx, bx, lam)


def _merge_kernel(a_ref, h_ref, ga_ref, gl_ref, wa_ref, wl_ref, o_ref):
    y_attn = jnp.dot(a_ref[...], wa_ref[...], preferred_element_type=F32)
    y_lru = jnp.dot(h_ref[...], wl_ref[...], preferred_element_type=F32)
    g_a = jax.nn.sigmoid(ga_ref[...].astype(F32))
    g_l = jax.nn.sigmoid(gl_ref[...].astype(F32))
    o_ref[...] = (g_a * y_attn + g_l * y_lru).astype(o_ref.dtype)


def _merge(attn, hg, rest, w_o_attn, w_o_lru, *, tm, tn):
    t, aw = attn.shape
    d = w_o_attn.shape[1]
    nj = d // tn
    return pl.pallas_call(
        _merge_kernel,
        grid=(t // tm, nj),
        in_specs=[pl.BlockSpec((tm, aw), lambda i, j: (i, 0)),
                  pl.BlockSpec((tm, d), lambda i, j: (i, 0)),
                  pl.BlockSpec((tm, tn), lambda i, j: (i, 2 * nj + j)),
                  pl.BlockSpec((tm, tn), lambda i, j: (i, 3 * nj + j)),
                  pl.BlockSpec((aw, tn), lambda i, j: (0, j)),
                  pl.BlockSpec((d, tn), lambda i, j: (0, j))],
        out_specs=pl.BlockSpec((tm, tn), lambda i, j: (i, j)),
        out_shape=jax.ShapeDtypeStruct((t, d), BF16),
        compiler_params=_cparams(("parallel", "parallel")),
        name="merge",
    )(attn, hg, rest, rest, w_o_attn, w_o_lru)


def _layer_norm(z, g, b):
    mu = jnp.mean(z, axis=-1, keepdims=True)
    zc = z - mu
    var = jnp.mean(zc * zc, axis=-1, keepdims=True)
    return zc * lax.rsqrt(var + LN_EPS) * g + b


def _out_ln_route_kernel(m_ref, x_ref, w_ref, g_ref, b_ref, rwh_ref, rwl_ref, rb_ref,
                         x1_ref, route_ref):
    y = jnp.dot(m_ref[...], w_ref[...], preferred_element_type=F32)
    z = DEEPNORM_ALPHA * x_ref[...] + y
    x1 = _layer_norm(z, g_ref[...], b_ref[...])
    x1_ref[...] = x1

    hi = x1.astype(BF16)
    lo = (x1 - hi.astype(F32)).astype(BF16)
    lg = (jnp.dot(hi, rwh_ref[...], preferred_element_type=F32)
          + jnp.dot(lo, rwh_ref[...], preferred_element_type=F32)
          + jnp.dot(hi, rwl_ref[...], preferred_element_type=F32)) + rb_ref[...]

    lane = lax.broadcasted_iota(jnp.int32, lg.shape, 1).astype(F32)
    first = lambda hit: jnp.min(jnp.where(hit, lane, float(ROUTER_LANES)), axis=1, keepdims=True)
    in_grp = lane < N_GROUPS
    lgg = jnp.where(in_grp, lg, MASK_BIAS)
    mg = jnp.max(lgg, axis=1, keepdims=True)
    g_idx = first(lgg == mg)
    p_grp = 1.0 / jnp.sum(jnp.where(in_grp, jnp.exp(lg - mg), 0.0), axis=1, keepdims=True)
    lo_lane = N_GROUPS + EXPERTS_PER_GROUP * g_idx
    in_exp = (lane >= lo_lane) & (lane < lo_lane + EXPERTS_PER_GROUP)
    le = jnp.where(in_exp, lg, MASK_BIAS)
    v1 = jnp.max(le, axis=1, keepdims=True)
    i1 = first((le == v1) & in_exp)
    rest = in_exp & (lane != i1)
    le2 = jnp.where(rest, lg, MASK_BIAS)
    v2 = jnp.max(le2, axis=1, keepdims=True)
    i2 = first((le2 == v2) & rest)
    t = jnp.exp(v2 - v1)
    w1 = p_grp / (1.0 + t)
    w2 = p_grp * t / (1.0 + t)
    route_ref[...] = jnp.where(lane == 0.0, i1 - N_GROUPS,
                     jnp.where(lane == 1.0, i2 - N_GROUPS,
                     jnp.where(lane == 2.0, w1,
                     jnp.where(lane == 3.0, w2, 0.0))))


def _out_ln_route(merged, x2d, w_out, ln_g, ln_b, rw_hi, rw_lo, rb, *, tm):
    t, d = merged.shape
    const = lambda a: pl.BlockSpec(a.shape, lambda i: (0,) * a.ndim)
    return pl.pallas_call(
        _out_ln_route_kernel,
        grid=(t // tm,),
        in_specs=[pl.BlockSpec((tm, d), lambda i: (i, 0)),
                  pl.BlockSpec((tm, d), lambda i: (i, 0)),
                  const(w_out), const(ln_g), const(ln_b), const(rw_hi), const(rw_lo), const(rb)],
        out_specs=[pl.BlockSpec((tm, d), lambda i: (i, 0)),
                   pl.BlockSpec((tm, ROUTER_LANES), lambda i: (i, 0))],
        out_shape=[jax.ShapeDtypeStruct((t, d), F32),
                   jax.ShapeDtypeStruct((t, ROUTER_LANES), F32)],
        compiler_params=_cparams(("parallel",)),
        name="out_ln_route",
    )(merged, x2d, w_out, ln_g, ln_b, rw_hi, rw_lo, rb)


def _moe_kernel(te_ref, g0_ref, gn_ref, dst_ref, x_hbm, wg_ref, wu_ref, wd_ref,
                y_hbm, xbuf, ybuf, wg_sc, wu_sc, wd_sc, gsem, ssem):
    i = pl.program_id(0)
    last = pl.num_programs(0) - 1
    tm = xbuf.shape[1]

    def gather_copy(tok, j, s):
        return pltpu.make_async_copy(x_hbm.at[pl.ds(tok, 1)], xbuf.at[s, pl.ds(j, 1)], gsem.at[s])

    def scatter_copy(row, j, s):
        return pltpu.make_async_copy(ybuf.at[s, pl.ds(j, 1)], y_hbm.at[pl.ds(row, 1)], ssem.at[s])

    @pl.when(i == 0)
    def _():
        for j in range(tm):
            gather_copy(g0_ref[0, 0, j], j, 0).start()

    @pl.when((i == 0) | (te_ref[i] != te_ref[jnp.maximum(i - 1, 0)]))
    def _():
        wg_sc[...] = wg_ref[0].astype(BF16)
        wu_sc[...] = wu_ref[0].astype(BF16)
        wd_sc[...] = wd_ref[0].astype(BF16)

    def step(slot):
        nxt = 1 - slot
        for j in range(tm):
            gather_copy(0, j, slot).wait()
        for j in range(tm):
            gather_copy(gn_ref[0, 0, j], j, nxt).start()
        xb = xbuf[slot].astype(BF16)
        gate = jnp.dot(xb, wg_sc[...], preferred_element_type=F32)
        up = jnp.dot(xb, wu_sc[...], preferred_element_type=F32)
        he = (jax.nn.silu(gate) * up).astype(BF16)
        ybuf[slot] = jnp.dot(he, wd_sc[...], preferred_element_type=F32)
        for j in range(tm):
            scatter_copy(dst_ref[0, 0, j], j, slot).start()

        @pl.when(i > 0)
        def _():
            for j in range(tm):
                scatter_copy(0, j, nxt).wait()

        @pl.when(i == last)
        def _():
            for j in range(tm):
                scatter_copy(0, j, slot).wait()
            for j in range(tm):
                gather_copy(0, j, nxt).wait()

    for parity in range(2):
        pl.when(lax.rem(i, 2) == parity)(functools.partial(step, parity))


def _moe_experts(x1, tile_expert, gidx, dest, wg, wu, wd):
    t, d = x1.shape
    nt = gidx.shape[0]
    f = wg.shape[2]
    idx_spec = lambda fn: pl.BlockSpec((1, 1, MOE_TILE), fn, memory_space=pltpu.SMEM)
    grid_spec = pltpu.PrefetchScalarGridSpec(
        num_scalar_prefetch=1,
        grid=(nt,),
        in_specs=[idx_spec(lambda i, te: (0, 0, 0)),
                  idx_spec(lambda i, te: (jnp.minimum(i + 1, nt - 1), 0, 0)),
                  idx_spec(lambda i, te: (i, 0, 0)),
                  pl.BlockSpec(memory_space=pl.ANY),
                  pl.BlockSpec((1, d, f), lambda i, te: (te[i], 0, 0)),
                  pl.BlockSpec((1, d, f), lambda i, te: (te[i], 0, 0)),
                  pl.BlockSpec((1, f, d), lambda i, te: (te[i], 0, 0))],
        out_specs=pl.BlockSpec(memory_space=pl.ANY),
        scratch_shapes=[pltpu.VMEM((2, MOE_TILE, d), F32),
                        pltpu.VMEM((2, MOE_TILE, d), F32),
                        pltpu.VMEM((d, f), BF16),
                        pltpu.VMEM((d, f), BF16),
                        pltpu.VMEM((f, d), BF16),
                        pltpu.SemaphoreType.DMA((2,)),
                        pltpu.SemaphoreType.DMA((2,))],
    )
    return pl.pallas_call(
        _moe_kernel,
        grid_spec=grid_spec,
        out_shape=jax.ShapeDtypeStruct((nt * MOE_TILE, d), F32),
        compiler_params=_cparams(("arbitrary",)),
        name="moe_experts",
    )(tile_expert, gidx, gidx, dest, x1, wg, wu, wd)


KEY_IDX_BITS = 16


def _moe_plan(route, n_tiles):
    t = route.shape[0]
    p = 2 * t
    assert p <= 1 << KEY_IDX_BITS and n_tiles * MOE_TILE == p + N_EXPERTS * MOE_TILE
    e_flat = jnp.concatenate([route[:, 0], route[:, 1]]).astype(jnp.int32)
    experts = jnp.arange(N_EXPERTS, dtype=jnp.int32)
    counts = jnp.sum((e_flat[:, None] == experts[None, :]).astype(jnp.int32), axis=0)
    n_pad = (-counts) % MOE_TILE
    k = jnp.arange(MOE_TILE, dtype=jnp.int32)
    pad_expert = jnp.where(k[None, :] < n_pad[:, None], experts[:, None], N_EXPERTS)
    pad_id = experts[:, None] * MOE_TILE + k[None, :]
    flag = 1 << KEY_IDX_BITS
    keys = jnp.concatenate([e_flat * (2 * flag) + jnp.arange(p, dtype=jnp.int32),
                            (pad_expert * (2 * flag) + flag + pad_id).reshape(-1)])
    keys = jnp.sort(keys)
    is_pad = (keys & flag) != 0
    idx = keys & (flag - 1)
    dest = jnp.where(is_pad, p + idx, idx)
    gidx = jnp.where(is_pad, 0, jnp.where(idx >= t, idx - t, idx))
    tile_expert = jnp.minimum(keys.reshape(n_tiles, MOE_TILE)[:, 0] // (2 * flag), N_EXPERTS - 1)
    shape3 = (n_tiles, 1, MOE_TILE)
    return tile_expert, gidx.reshape(shape3), dest.reshape(shape3)


def _final_kernel(x1_ref, y0_ref, y1_ref, route_ref, g_ref, b_ref, o_ref):
    y = route_ref[:, 2:3] * y0_ref[...] + route_ref[:, 3:4] * y1_ref[...]
    z = DEEPNORM_ALPHA * x1_ref[...] + y
    o_ref[...] = _layer_norm(z, g_ref[...], b_ref[...])


def _final_ln(x1, y_rows, route, ln_g, ln_b, *, tm):
    t, d = x1.shape
    nb = t // tm
    const = lambda a: pl.BlockSpec(a.shape, lambda i: (0,) * a.ndim)
    return pl.pallas_call(
        _final_kernel,
        grid=(nb,),
        in_specs=[pl.BlockSpec((tm, d), lambda i: (i, 0)),
                  pl.BlockSpec((tm, d), lambda i: (i, 0)),
                  pl.BlockSpec((tm, d), lambda i: (nb + i, 0)),
                  pl.BlockSpec((tm, ROUTER_LANES), lambda i: (i, 0)),
                  const(ln_g), const(ln_b)],
        out_specs=pl.BlockSpec((tm, d), lambda i: (i, 0)),
        out_shape=jax.ShapeDtypeStruct((t, d), F32),
        compiler_params=_cparams(("parallel",)),
        name="final_ln",
    )(x1, y_rows, y_rows, route, ln_g, ln_b)


def _router_weights(router_group_w, router_group_b, router_expert_w, router_expert_b):
    d = router_group_w.shape[0]
    pad = ROUTER_LANES - N_GROUPS - N_EXPERTS
    w = jnp.concatenate([router_group_w, router_expert_w, jnp.zeros((d, pad), F32)], axis=1)
    b = jnp.concatenate([router_group_b, router_expert_b, jnp.zeros((pad,), F32)]).reshape(1, ROUTER_LANES)
    w_hi = w.astype(BF16)
    w_lo = (w - w_hi.astype(F32)).astype(BF16)
    return w_hi, w_lo, b


def kernel(x, w_in, conv_w, conv_b, lru_wa, lru_ba, lru_wx, lru_bx, lru_lambda, w_o_attn, w_o_lru, w_out, ln1_g, ln1_b, router_group_w, router_group_b, router_expert_w, router_expert_b, w_e_gate, w_e_up, w_e_down, ln2_g, ln2_b):
    batch, seq, d = x.shape
    i_len = seq // CLASSES
    t = batch * seq
    attn_w = N_HEADS * HEAD_DIM
    for l in range(DEPTH):
        x2d = x.reshape(t, d)
        x_bf = x2d.astype(BF16)
        x_cls = x.reshape(batch, i_len, CLASSES, d).transpose(2, 0, 1, 3).astype(BF16)
        x_cls = x_cls.reshape(CLASSES, batch * i_len, d)
        row = lambda a: a.reshape(1, -1)

        qkv_hm = _qkv_proj(x_cls, w_in[l], n=3 * attn_w, tn=1024)
        rest = _rest_proj(x_bf, w_in[l], col0=3 * attn_w, tm=1024, tn=1024)
        attn_hm = _dilated_attention(qkv_hm, batch)
        attn = attn_hm.reshape(N_HEADS, CLASSES, batch, i_len, HEAD_DIM)
        attn = attn.transpose(2, 3, 1, 0, 4).reshape(t, attn_w)
        hg = _rglru(rest.reshape(batch, seq, 4 * d), conv_w[l], row(conv_b[l]),
                    lru_wa[l].astype(BF16), row(lru_ba[l]), lru_wx[l].astype(BF16), row(lru_bx[l]),
                    row(lru_lambda[l]), d=d, ts=256, cw=512)
        merged = _merge(attn, hg.reshape(t, d), rest, w_o_attn[l].astype(BF16), w_o_lru[l].astype(BF16),
                        tm=1024, tn=512)
        rw_hi, rw_lo, rb = _router_weights(router_group_w[l], router_group_b[l],
                                           router_expert_w[l], router_expert_b[l])
        x1, route = _out_ln_route(merged, x2d, w_out[l].astype(BF16),
                                  row(ln1_g[l]), row(ln1_b[l]), rw_hi, rw_lo, rb, tm=512)

        n_tiles = 2 * t // MOE_TILE + N_EXPERTS
        tile_expert, gidx, dest = _moe_plan(route, n_tiles)
        y_rows = _moe_experts(x1, tile_expert, gidx, dest, w_e_gate[l], w_e_up[l], w_e_down[l])
        x = _final_ln(x1, y_rows, route, row(ln2_g[l]), row(ln2_b[l]), tm=512)
        x = x.reshape(batch, seq, d)
    return x
```

```python
import functools
import math

import numpy as np
import jax
import jax.numpy as jnp
from jax import lax
from jax.experimental import pallas as pl
from jax.experimental.pallas import tpu as pltpu

F32 = jnp.float32
BF16 = jnp.bfloat16

N_HEADS = 16
HEAD_DIM = 128
N_BACK = 128
CLASSES = 16
CONV_WIDTH = 4
LRU_BLOCK_DIM = 128
LRU_C = 8.0
N_GROUPS = 4
EXPERTS_PER_GROUP = 8
N_EXPERTS = N_GROUPS * EXPERTS_PER_GROUP
DEPTH = 1
DEEPNORM_ALPHA = (2.0 * DEPTH) ** 0.25
LN_EPS = 1e-5
ATTN_SCALE_LOG2 = HEAD_DIM ** -0.5 * math.log2(math.e)

LANES = 128
SUBLANES = 8
VMEM_LIMIT_BYTES = 56 * 1024 * 1024

MASK_BIAS = -1e30
MOE_TILE = 256
ROUTER_LANES = LANES


def _cparams(sem):
    return pltpu.CompilerParams(dimension_semantics=sem, vmem_limit_bytes=VMEM_LIMIT_BYTES)


def _qkv_kernel(x_ref, w_ref, o_ref, w_sc):
    @pl.when(pl.program_id(1) == 0)
    def _():
        w_sc[...] = w_ref[...].astype(BF16)

    res = jnp.dot(x_ref[0], w_sc[...], preferred_element_type=F32).astype(o_ref.dtype)
    for hh in range(o_ref.shape[0]):
        o_ref[hh, 0] = res[:, hh * HEAD_DIM:(hh + 1) * HEAD_DIM]


def _qkv_proj(x_cls, w_in, *, n, tn):
    _, rows, d = x_cls.shape
    heads_per_step = tn // HEAD_DIM
    return pl.pallas_call(
        _qkv_kernel,
        grid=(n // tn, CLASSES),
        in_specs=[pl.BlockSpec((1, rows, d), lambda j, r: (r, 0, 0)),
                  pl.BlockSpec((d, tn), lambda j, r: (0, j))],
        out_specs=pl.BlockSpec((heads_per_step, 1, rows, HEAD_DIM), lambda j, r: (j, r, 0, 0)),
        out_shape=jax.ShapeDtypeStruct((n // HEAD_DIM, CLASSES, rows, HEAD_DIM), F32),
        scratch_shapes=[pltpu.VMEM((d, tn), BF16)],
        compiler_params=_cparams(("parallel", "arbitrary")),
        name="qkv_proj",
    )(x_cls, w_in)


def _rest_kernel(x_ref, w_ref, o_ref, w_sc):
    @pl.when(pl.program_id(1) == 0)
    def _():
        w_sc[...] = w_ref[...].astype(BF16)

    o_ref[...] = jnp.dot(x_ref[...], w_sc[...], preferred_element_type=F32).astype(o_ref.dtype)


def _rest_proj(x2d, w_in, *, col0, tm, tn):
    t, d = x2d.shape
    n = w_in.shape[1] - col0
    j0 = col0 // tn
    return pl.pallas_call(
        _rest_kernel,
        grid=(n // tn, t // tm),
        in_specs=[pl.BlockSpec((tm, d), lambda j, i: (i, 0)),
                  pl.BlockSpec((d, tn), lambda j, i: (0, j0 + j))],
        out_specs=pl.BlockSpec((tm, tn), lambda j, i: (i, j)),
        out_shape=jax.ShapeDtypeStruct((t, n), BF16),
        scratch_shapes=[pltpu.VMEM((d, tn), BF16)],
        compiler_params=_cparams(("parallel", "arbitrary")),
        name="rest_proj",
    )(x2d, w_in)


def _band_bias(dist):
    return np.where((dist >= 0) & (dist <= N_BACK), 0.0, MASK_BIAS).astype(np.float32)


Q_BLOCK = 128


def _attn_biases():
    lq = np.arange(Q_BLOCK)[:, None]
    b16_first = _band_bias(lq - np.arange(Q_BLOCK)[None, :])
    b16_next = _band_bias(Q_BLOCK + lq - np.arange(2 * Q_BLOCK)[None, :])
    cq = np.repeat(np.arange(4), 32)[:, None]
    lq = np.tile(np.arange(32), 4)[:, None]
    ck = np.repeat(np.arange(4), 64)[None, :]
    lk = np.tile(np.arange(64), 4)[None, :]
    b4 = np.stack([_band_bias(4 * (lq - lk) + (cq - ck)),
                   _band_bias(4 * (32 + lq - lk) + (cq - ck))])
    rq = np.repeat(np.arange(16), 8)[:, None]
    lq = np.tile(np.arange(8), 16)[:, None]
    rk = np.repeat(np.arange(16), 16)[None, :]
    lk = np.tile(np.arange(16), 16)[None, :]
    b1 = np.stack([_band_bias(16 * (lq - lk) + (rq - rk)),
                   _band_bias(16 * (8 + lq - lk) + (rq - rk))])
    return b16_first, b16_next, b4, b1


def _softmax_blocks(blocks):
    scores = [lax.dot_general(q, k, (((1,), (1,)), ((), ())), preferred_element_type=F32) * ATTN_SCALE_LOG2
              + bias for q, k, _, bias, _ in blocks]
    mids = []
    for s, (_, _, _, _, state) in zip(scores, blocks):
        rows, n = s.shape
        chunks = [s[:, c * LANES:(c + 1) * LANES] for c in range(n // LANES)]
        mx = functools.reduce(jnp.maximum, chunks)
        m_blk = jnp.broadcast_to(jnp.max(mx, axis=1, keepdims=True), (rows, LANES))
        m_new = m_blk if state is None else jnp.maximum(state[0], m_blk)
        ps = [jnp.exp2(c - m_new) for c in chunks]
        l_blk = jnp.broadcast_to(jnp.sum(functools.reduce(jnp.add, ps), axis=1, keepdims=True), (rows, LANES))
        mids.append((m_new, l_blk, jnp.concatenate(ps, axis=1).astype(BF16)))
    out = []
    for (m_new, l_blk, p), (_, _, v, _, state) in zip(mids, blocks):
        pv = jnp.dot(p, v, preferred_element_type=F32)
        if state is None:
            out.append((m_new, l_blk, pv))
        else:
            alpha = jnp.exp2(state[0] - m_new)
            out.append((m_new, alpha * state[1] + l_blk, alpha * state[2] + pv))
    return out


BLOCKS_PER_ITER = 16


def _attn_kernel(q_ref, k_ref, v_ref, b16a_ref, b16b_ref, b4_ref, b1_ref, o_ref,
                 qb, kb, vb, kb8, vb8, m_sc, l_sc, acc_sc):
    i_len = q_ref.shape[2]
    blocks_per_class = i_len // Q_BLOCK

    qb[...] = q_ref[0].astype(BF16)
    kb[...] = k_ref[0].astype(BF16)
    vb[...] = v_ref[0].astype(BF16)
    kb8[:, :i_len - SUBLANES, :] = k_ref[0, :, SUBLANES:, :].astype(BF16)
    vb8[:, :i_len - SUBLANES, :] = v_ref[0, :, SUBLANES:, :].astype(BF16)

    classes_per_iter = BLOCKS_PER_ITER // blocks_per_class

    def d16_body(it, carry):
        where, blocks = [], []
        for u in range(classes_per_iter):
            r = it * classes_per_iter + u
            for n in range(blocks_per_class):
                rows = slice(n * Q_BLOCK, (n + 1) * Q_BLOCK)
                keys = slice(0, Q_BLOCK) if n == 0 else slice((n - 1) * Q_BLOCK, (n + 1) * Q_BLOCK)
                bias = b16a_ref[...] if n == 0 else b16b_ref[...]
                where.append((r, rows))
                blocks.append((qb[r, rows, :], kb[r, keys, :], vb[r, keys, :], bias, None))
        for (r, rows), (m, l, acc) in zip(where, _softmax_blocks(blocks)):
            m_sc[r, rows, :] = m
            l_sc[r, rows, :] = l
            acc_sc[r, rows, :] = acc
        return carry

    lax.fori_loop(0, CLASSES // classes_per_iter, d16_body, 0)

    n_per_iter = BLOCKS_PER_ITER // 4

    def d4_body(it, carry):
        where, blocks = [], []
        for u in range(n_per_iter):
            n = it * n_per_iter + u
            q0 = pl.multiple_of(n * 32, 32)
            k0 = pl.multiple_of(jnp.maximum(n - 1, 0) * 32, 32)
            bias = b4_ref[jnp.minimum(n, 1)]
            for r4 in range(4):
                cls = [r4 + 4 * c for c in range(4)]
                cat = lambda ref, start, size, cls=cls: jnp.concatenate(
                    [ref[r, pl.ds(start, size), :] for r in cls], axis=0)
                state = (cat(m_sc, q0, 32), cat(l_sc, q0, 32), cat(acc_sc, q0, 32))
                where.append((cls, q0))
                blocks.append((cat(qb, q0, 32), cat(kb, k0, 64), cat(vb, k0, 64), bias, state))
        for (cls, q0), (m, l, acc) in zip(where, _softmax_blocks(blocks)):
            for c, r in enumerate(cls):
                m_sc[r, pl.ds(q0, 32), :] = m[32 * c:32 * (c + 1)]
                l_sc[r, pl.ds(q0, 32), :] = l[32 * c:32 * (c + 1)]
                acc_sc[r, pl.ds(q0, 32), :] = acc[32 * c:32 * (c + 1)]
        return carry

    lax.fori_loop(0, i_len // 32 // n_per_iter, d4_body, 0)

    def d1_body(it, carry):
        where, blocks = [], []
        for u in range(BLOCKS_PER_ITER):
            mb = it * BLOCKS_PER_ITER + u
            q0 = pl.multiple_of(mb * SUBLANES, SUBLANES)
            q = q_ref[0, :, pl.ds(q0, SUBLANES), :].reshape(Q_BLOCK, HEAD_DIM).astype(BF16)
            if u % 2 == 1:
                k0 = pl.multiple_of((mb - 1) * SUBLANES, 16)
                k, v = kb[:, pl.ds(k0, 16), :], vb[:, pl.ds(k0, 16), :]
            else:
                k0 = pl.multiple_of(jnp.maximum(mb - 2, 0) * SUBLANES, 16)
                k, v = kb8[:, pl.ds(k0, 16), :], vb8[:, pl.ds(k0, 16), :]
                if u == 0:
                    k = jnp.where(mb == 0, kb[:, 0:16, :], k)
                    v = jnp.where(mb == 0, vb[:, 0:16, :], v)
            flat = lambda ref, q0=q0: ref[:, pl.ds(q0, SUBLANES), :].reshape(Q_BLOCK, LANES)
            where.append(q0)
            blocks.append((q, k.reshape(2 * Q_BLOCK, HEAD_DIM), v.reshape(2 * Q_BLOCK, HEAD_DIM),
                           b1_ref[jnp.minimum(mb, 1)], (flat(m_sc), flat(l_sc), flat(acc_sc))))
        for q0, (m, l, acc) in zip(where, _softmax_blocks(blocks)):
            m_sc[:, pl.ds(q0, SUBLANES), :] = m.reshape(CLASSES, SUBLANES, LANES)
            l_sc[:, pl.ds(q0, SUBLANES), :] = l.reshape(CLASSES, SUBLANES, LANES)
            acc_sc[:, pl.ds(q0, SUBLANES), :] = acc.reshape(CLASSES, SUBLANES, LANES)
        return carry

    lax.fori_loop(0, i_len // SUBLANES // BLOCKS_PER_ITER, d1_body, 0)

    o_ref[0] = (acc_sc[...] / l_sc[...]).astype(o_ref.dtype)


def _dilated_attention(qkv_hm, batch):
    n3, _, bi, _ = qkv_hm.shape
    h = n3 // 3
    i_len = bi // batch
    assert CLASSES * SUBLANES == Q_BLOCK and i_len % Q_BLOCK == 0 and BLOCKS_PER_ITER % (i_len // Q_BLOCK) == 0
    biases = [jnp.asarray(a) for a in _attn_biases()]
    blk = (1, CLASSES, i_len, HEAD_DIM)
    const = lambda a: pl.BlockSpec(a.shape, lambda hh, bb: (0,) * a.ndim)
    operand = lambda: pltpu.VMEM((CLASSES, i_len, HEAD_DIM), BF16)
    state = lambda: pltpu.VMEM((CLASSES, i_len, LANES), F32)
    return pl.pallas_call(
        _attn_kernel,
        grid=(h, batch),
        in_specs=[pl.BlockSpec(blk, lambda hh, bb: (hh, 0, bb, 0)),
                  pl.BlockSpec(blk, lambda hh, bb: (h + hh, 0, bb, 0)),
                  pl.BlockSpec(blk, lambda hh, bb: (2 * h + hh, 0, bb, 0))] + [const(a) for a in biases],
        out_specs=pl.BlockSpec(blk, lambda hh, bb: (hh, 0, bb, 0)),
        out_shape=jax.ShapeDtypeStruct((h, CLASSES, bi, HEAD_DIM), BF16),
        scratch_shapes=[operand(), operand(), operand(), operand(), operand(), state(), state(), state()],
        compiler_params=_cparams(("parallel", "parallel")),
        name="dilated_attn",
    )(qkv_hm, qkv_hm, qkv_hm, *biases)


def _softplus(z):
    return jnp.maximum(z, 0.0) + jnp.log1p(jnp.exp(-jnp.abs(z)))


def _lru_kernel(xr_ref, xg_ref, cw_ref, cb_ref, wa_ref, ba_ref, wx_ref, bx_ref, lam_ref,
                o_ref, h_sc, tail_sc):
    ts = xr_ref.shape[1]
    cw = xr_ref.shape[2]
    nv = ts // SUBLANES

    @pl.when(pl.program_id(2) == 0)
    def _():
        h_sc[...] = jnp.zeros_like(h_sc)
        tail_sc[...] = jnp.zeros_like(tail_sc)

    row = lax.broadcasted_iota(jnp.int32, (nv, SUBLANES, LANES), 1)
    for g in range(cw // LANES):
        sl = slice(g * LANES, (g + 1) * LANES)
        xr = xr_ref[0, :, sl].astype(F32)
        xfull = jnp.concatenate([tail_sc[:, sl], xr], axis=0)
        taps = cw_ref[:, sl]
        xc = xr * taps[3:4]
        for j in range(1, CONV_WIDTH):
            shifted = pltpu.roll(xfull, j, axis=0)[SUBLANES:]
            xc = xc + shifted * taps[3 - j:4 - j]
        xc = xc + cb_ref[:, sl]
        tail_sc[:, sl] = xr[ts - SUBLANES:]

        xcb = xc.astype(BF16)
        gate_a = jnp.dot(xcb, wa_ref[g], preferred_element_type=F32) + ba_ref[:, sl]
        gate_x = jnp.dot(xcb, wx_ref[g], preferred_element_type=F32) + bx_ref[:, sl]
        r = jax.nn.sigmoid(gate_a)
        i = jax.nn.sigmoid(gate_x)
        log_a = (-LRU_C * _softplus(-lam_ref[:, sl])) * r
        a = jnp.exp(log_a)
        u = jnp.sqrt(1.0 - a * a) * (i * xc)

        a3 = a.reshape(nv, SUBLANES, LANES)
        b3 = u.reshape(nv, SUBLANES, LANES)
        for k in (1, 2, 4):
            a_sh = jnp.where(row >= k, pltpu.roll(a3, k, axis=1), 1.0)
            b_sh = jnp.where(row >= k, pltpu.roll(b3, k, axis=1), 0.0)
            b3 = a3 * b_sh + b3
            a3 = a3 * a_sh
        h = h_sc[:, sl]
        outs = []
        for vi in range(nv):
            hv = a3[vi] * h + b3[vi]
            outs.append(hv)
            h = hv[SUBLANES - 1:SUBLANES]
        h_sc[:, sl] = h
        hs = jnp.concatenate(outs, axis=0)
        o_ref[0, :, sl] = (hs * jax.nn.gelu(xg_ref[0, :, sl].astype(F32))).astype(o_ref.dtype)


def _rglru(rest3, conv_w, conv_b, wa, ba, wx, bx, lam, *, d, ts, cw):
    b, s, _ = rest3.shape
    ncw = d // cw
    gpb = cw // LRU_BLOCK_DIM
    vec = lambda: pl.BlockSpec((1, cw), lambda bb, c, t: (0, c))
    return pl.pallas_call(
        _lru_kernel,
        grid=(b, ncw, s // ts),
        in_specs=[pl.BlockSpec((1, ts, cw), lambda bb, c, t: (bb, t, c)),
                  pl.BlockSpec((1, ts, cw), lambda bb, c, t: (bb, t, ncw + c)),
                  pl.BlockSpec((CONV_WIDTH, cw), lambda bb, c, t: (0, c)),
                  vec(),
                  pl.BlockSpec((gpb, LRU_BLOCK_DIM, LRU_BLOCK_DIM), lambda bb, c, t: (c, 0, 0)),
                  vec(),
                  pl.BlockSpec((gpb, LRU_BLOCK_DIM, LRU_BLOCK_DIM), lambda bb, c, t: (c, 0, 0)),
                  vec(), vec()],
        out_specs=pl.BlockSpec((1, ts, cw), lambda bb, c, t: (bb, t, c)),
        out_shape=jax.ShapeDtypeStruct((b, s, d), BF16),
        scratch_shapes=[pltpu.VMEM((1, cw), F32), pltpu.VMEM((SUBLANES, cw), F32)],
        compiler_params=_cparams(("parallel", "parallel", "arbitrary")),
        name="rglru",
    )(rest3, rest3, conv_w, conv_b, wa, ba, wx, bx, lam)


def _merge_kernel(a_ref, h_ref, ga_ref, gl_ref, wa_ref, wl_ref, o_ref):
    y_attn = jnp.dot(a_ref[...], wa_ref[...], preferred_element_type=F32)
    y_lru = jnp.dot(h_ref[...], wl_ref[...], preferred_element_type=F32)
    g_a = jax.nn.sigmoid(ga_ref[...].astype(F32))
    g_l = jax.nn.sigmoid(gl_ref[...].astype(F32))
    o_ref[...] = (g_a * y_attn + g_l * y_lru).astype(o_ref.dtype)


def _merge(attn, hg, rest, w_o_attn, w_o_lru, *, tm, tn):
    t, aw = attn.shape
    d = w_o_attn.shape[1]
    nj = d // tn
    return pl.pallas_call(
        _merge_kernel,
        grid=(t // tm, nj),
        in_specs=[pl.BlockSpec((tm, aw), lambda i, j: (i, 0)),
                  pl.BlockSpec((tm, d), lambda i, j: (i, 0)),
                  pl.BlockSpec((tm, tn), lambda i, j: (i, 2 * nj + j)),
                  pl.BlockSpec((tm, tn), lambda i, j: (i, 3 * nj + j)),
                  pl.BlockSpec((aw, tn), lambda i, j: (0, j)),
                  pl.BlockSpec((d, tn), lambda i, j: (0, j))],
        out_specs=pl.BlockSpec((tm, tn), lambda i, j: (i, j)),
        out_shape=jax.ShapeDtypeStruct((t, d), BF16),
        compiler_params=_cparams(("parallel", "parallel")),
        name="merge",
    )(attn, hg, rest, rest, w_o_attn, w_o_lru)


def _layer_norm(z, g, b):
    mu = jnp.mean(z, axis=-1, keepdims=True)
    zc = z - mu
    var = jnp.mean(zc * zc, axis=-1, keepdims=True)
    return zc * lax.rsqrt(var + LN_EPS) * g + b


def _out_ln_route_kernel(m_ref, x_ref, w_ref, g_ref, b_ref, rwh_ref, rwl_ref, rb_ref,
                         x1_ref, x1p_ref, route_ref):
    y = jnp.dot(m_ref[...], w_ref[...], preferred_element_type=F32)
    z = DEEPNORM_ALPHA * x_ref[...] + y
    x1 = _layer_norm(z, g_ref[...], b_ref[...])
    x1_ref[...] = x1
    x1p_ref[...] = _pack_bf16_pairs(x1)

    hi = x1.astype(BF16)
    lo = (x1 - hi.astype(F32)).astype(BF16)
    lg = (jnp.dot(hi, rwh_ref[...], preferred_element_type=F32)
          + jnp.dot(lo, rwh_ref[...], preferred_element_type=F32)
          + jnp.dot(hi, rwl_ref[...], preferred_element_type=F32)) + rb_ref[...]

    lane = lax.broadcasted_iota(jnp.int32, lg.shape, 1).astype(F32)
    first = lambda hit: jnp.min(jnp.where(hit, lane, float(ROUTER_LANES)), axis=1, keepdims=True)
    in_grp = lane < N_GROUPS
    lgg = jnp.where(in_grp, lg, MASK_BIAS)
    mg = jnp.max(lgg, axis=1, keepdims=True)
    g_idx = first(lgg == mg)
    p_grp = 1.0 / jnp.sum(jnp.where(in_grp, jnp.exp(lg - mg), 0.0), axis=1, keepdims=True)
    lo_lane = N_GROUPS + EXPERTS_PER_GROUP * g_idx
    in_exp = (lane >= lo_lane) & (lane < lo_lane + EXPERTS_PER_GROUP)
    le = jnp.where(in_exp, lg, MASK_BIAS)
    v1 = jnp.max(le, axis=1, keepdims=True)
    i1 = first((le == v1) & in_exp)
    rest = in_exp & (lane != i1)
    le2 = jnp.where(rest, lg, MASK_BIAS)
    v2 = jnp.max(le2, axis=1, keepdims=True)
    i2 = first((le2 == v2) & rest)
    t = jnp.exp(v2 - v1)
    w1 = p_grp / (1.0 + t)
    w2 = p_grp * t / (1.0 + t)
    route_ref[...] = jnp.where(lane == 0.0, i1 - N_GROUPS,
                     jnp.where(lane == 1.0, i2 - N_GROUPS,
                     jnp.where(lane == 2.0, w1,
                     jnp.where(lane == 3.0, w2, 0.0))))


def _out_ln_route(merged, x2d, w_out, ln_g, ln_b, rw_hi, rw_lo, rb, *, tm):
    t, d = merged.shape
    const = lambda a: pl.BlockSpec(a.shape, lambda i: (0,) * a.ndim)
    return pl.pallas_call(
        _out_ln_route_kernel,
        grid=(t // tm,),
        in_specs=[pl.BlockSpec((tm, d), lambda i: (i, 0)),
                  pl.BlockSpec((tm, d), lambda i: (i, 0)),
                  const(w_out), const(ln_g), const(ln_b), const(rw_hi), const(rw_lo), const(rb)],
        out_specs=[pl.BlockSpec((tm, d), lambda i: (i, 0)),
                   pl.BlockSpec((tm, d // 2), lambda i: (i, 0)),
                   pl.BlockSpec((tm, ROUTER_LANES), lambda i: (i, 0))],
        out_shape=[jax.ShapeDtypeStruct((t, d), F32),
                   jax.ShapeDtypeStruct((t, d // 2), U32),
                   jax.ShapeDtypeStruct((t, ROUTER_LANES), F32)],
        compiler_params=_cparams(("parallel",)),
        name="out_ln_route",
    )(merged, x2d, w_out, ln_g, ln_b, rw_hi, rw_lo, rb)


U32 = jnp.uint32
_HI16 = np.uint32(0xFFFF0000)


def _pack_bf16_pairs(a):
    w = a.shape[1] // 2
    bits = lambda v: lax.bitcast_convert_type(v.astype(BF16).astype(F32), U32)
    return (bits(a[:, :w]) >> 16) | (bits(a[:, w:]) & _HI16)


def _unpack_bf16_pairs(p):
    lo = lax.bitcast_convert_type(p << 16, F32)
    hi = lax.bitcast_convert_type(p & _HI16, F32)
    return jnp.concatenate([lo, hi], axis=1)
def _moe_kernel(te_ref, g0_ref, gn_ref, dst_ref, x_hbm, wg_ref, wu_ref, wd_ref,
                y_hbm, xbuf, ybuf, wg_sc, wu_sc, wd_sc, gsem, ssem):
    i = pl.program_id(0)
    last = pl.num_programs(0) - 1
    tm = xbuf.shape[1]

    def gather_copy(tok, j, s):
        return pltpu.make_async_copy(x_hbm.at[pl.ds(tok, 1)], xbuf.at[s, pl.ds(j, 1)], gsem.at[s])

    def scatter_copy(row, j, s):
        return pltpu.make_async_copy(ybuf.at[s, pl.ds(j, 1)], y_hbm.at[pl.ds(row, 1)], ssem.at[s])

    @pl.when(i == 0)
    def _():
        for j in range(tm):
            gather_copy(g0_ref[0, 0, j], j, 0).start()

    @pl.when((i == 0) | (te_ref[i] != te_ref[jnp.maximum(i - 1, 0)]))
    def _():
        wg_sc[...] = wg_ref[0].astype(BF16)
        wu_sc[...] = wu_ref[0].astype(BF16)
        wd_sc[...] = wd_ref[0].astype(BF16)

    def step(slot):
        nxt = 1 - slot
        for j in range(tm):
            gather_copy(0, j, slot).wait()
        for j in range(tm):
            gather_copy(gn_ref[0, 0, j], j, nxt).start()
        xb = _unpack_bf16_pairs(xbuf[slot]).astype(BF16)
        gate = jnp.dot(xb, wg_sc[...], preferred_element_type=F32)
        up = jnp.dot(xb, wu_sc[...], preferred_element_type=F32)
        he = (jax.nn.silu(gate) * up).astype(BF16)
        ybuf[slot] = _pack_bf16_pairs(jnp.dot(he, wd_sc[...], preferred_element_type=F32))
        for j in range(tm):
            scatter_copy(dst_ref[0, 0, j], j, slot).start()

        @pl.when(i > 0)
        def _():
            for j in range(tm):
                scatter_copy(0, j, nxt).wait()

        @pl.when(i == last)
        def _():
            for j in range(tm):
                scatter_copy(0, j, slot).wait()
            for j in range(tm):
                gather_copy(0, j, nxt).wait()

    for parity in range(2):
        pl.when(lax.rem(i, 2) == parity)(functools.partial(step, parity))


def _moe_experts(x1p, tile_expert, gidx, dest, wg, wu, wd):
    t, dp = x1p.shape
    d = wg.shape[1]
    nt = gidx.shape[0]
    f = wg.shape[2]
    idx_spec = lambda fn: pl.BlockSpec((1, 1, MOE_TILE), fn, memory_space=pltpu.SMEM)
    grid_spec = pltpu.PrefetchScalarGridSpec(
        num_scalar_prefetch=1,
        grid=(nt,),
        in_specs=[idx_spec(lambda i, te: (0, 0, 0)),
                  idx_spec(lambda i, te: (jnp.minimum(i + 1, nt - 1), 0, 0)),
                  idx_spec(lambda i, te: (i, 0, 0)),
                  pl.BlockSpec(memory_space=pl.ANY),
                  pl.BlockSpec((1, d, f), lambda i, te: (te[i], 0, 0)),
                  pl.BlockSpec((1, d, f), lambda i, te: (te[i], 0, 0)),
                  pl.BlockSpec((1, f, d), lambda i, te: (te[i], 0, 0))],
        out_specs=pl.BlockSpec(memory_space=pl.ANY),
        scratch_shapes=[pltpu.VMEM((2, MOE_TILE, dp), U32),
                        pltpu.VMEM((2, MOE_TILE, dp), U32),
                        pltpu.VMEM((d, f), BF16),
                        pltpu.VMEM((d, f), BF16),
                        pltpu.VMEM((f, d), BF16),
                        pltpu.SemaphoreType.DMA((2,)),
                        pltpu.SemaphoreType.DMA((2,))],
    )
    return pl.pallas_call(
        _moe_kernel,
        grid_spec=grid_spec,
        out_shape=jax.ShapeDtypeStruct((nt * MOE_TILE, dp), U32),
        compiler_params=_cparams(("arbitrary",)),
        name="moe_experts",
    )(tile_expert, gidx, gidx, dest, x1p, wg, wu, wd)


KEY_IDX_BITS = 16


def _moe_plan(route, n_tiles):
    t = route.shape[0]
    p = 2 * t
    assert p <= 1 << KEY_IDX_BITS and n_tiles * MOE_TILE == p + N_EXPERTS * MOE_TILE
    e_flat = jnp.concatenate([route[:, 0], route[:, 1]]).astype(jnp.int32)
    experts = jnp.arange(N_EXPERTS, dtype=jnp.int32)
    counts = jnp.sum((e_flat[:, None] == experts[None, :]).astype(jnp.int32), axis=0)
    n_pad = (-counts) % MOE_TILE
    k = jnp.arange(MOE_TILE, dtype=jnp.int32)
    pad_expert = jnp.where(k[None, :] < n_pad[:, None], experts[:, None], N_EXPERTS)
    pad_id = experts[:, None] * MOE_TILE + k[None, :]
    flag = 1 << KEY_IDX_BITS
    keys = jnp.concatenate([e_flat * (2 * flag) + jnp.arange(p, dtype=jnp.int32),
                            (pad_expert * (2 * flag) + flag + pad_id).reshape(-1)])
    keys = jnp.sort(keys)
    is_pad = (keys & flag) != 0
    idx = keys & (flag - 1)
    dest = jnp.where(is_pad, p + idx, idx)
    gidx = jnp.where(is_pad, 0, jnp.where(idx >= t, idx - t, idx))
    tile_expert = jnp.minimum(keys.reshape(n_tiles, MOE_TILE)[:, 0] // (2 * flag), N_EXPERTS - 1)
    shape3 = (n_tiles, 1, MOE_TILE)
    return tile_expert, gidx.reshape(shape3), dest.reshape(shape3)


def _final_kernel(x1_ref, y0_ref, y1_ref, route_ref, g_ref, b_ref, o_ref):
    y = (route_ref[:, 2:3] * _unpack_bf16_pairs(y0_ref[...])
         + route_ref[:, 3:4] * _unpack_bf16_pairs(y1_ref[...]))
    z = DEEPNORM_ALPHA * x1_ref[...] + y
    o_ref[...] = _layer_norm(z, g_ref[...], b_ref[...])


def _final_ln(x1, y_rows, route, ln_g, ln_b, *, tm):
    t, d = x1.shape
    nb = t // tm
    const = lambda a: pl.BlockSpec(a.shape, lambda i: (0,) * a.ndim)
    return pl.pallas_call(
        _final_kernel,
        grid=(nb,),
        in_specs=[pl.BlockSpec((tm, d), lambda i: (i, 0)),
                  pl.BlockSpec((tm, d // 2), lambda i: (i, 0)),
                  pl.BlockSpec((tm, d // 2), lambda i: (nb + i, 0)),
                  pl.BlockSpec((tm, ROUTER_LANES), lambda i: (i, 0)),
                  const(ln_g), const(ln_b)],
        out_specs=pl.BlockSpec((tm, d), lambda i: (i, 0)),
        out_shape=jax.ShapeDtypeStruct((t, d), F32),
        compiler_params=_cparams(("parallel",)),
        name="final_ln",
    )(x1, y_rows, y_rows, route, ln_g, ln_b)


def _router_weights(router_group_w, router_group_b, router_expert_w, router_expert_b):
    d = router_group_w.shape[0]
    pad = ROUTER_LANES - N_GROUPS - N_EXPERTS
    w = jnp.concatenate([router_group_w, router_expert_w, jnp.zeros((d, pad), F32)], axis=1)
    b = jnp.concatenate([router_group_b, router_expert_b, jnp.zeros((pad,), F32)]).reshape(1, ROUTER_LANES)
    w_hi = w.astype(BF16)
    w_lo = (w - w_hi.astype(F32)).astype(BF16)
    return w_hi, w_lo, b


def kernel(x, w_in, conv_w, conv_b, lru_wa, lru_ba, lru_wx, lru_bx, lru_lambda, w_o_attn, w_o_lru, w_out, ln1_g, ln1_b, router_group_w, router_group_b, router_expert_w, router_expert_b, w_e_gate, w_e_up, w_e_down, ln2_g, ln2_b):
    batch, seq, d = x.shape
    i_len = seq // CLASSES
    t = batch * seq
    attn_w = N_HEADS * HEAD_DIM
    for l in range(DEPTH):
        x2d = x.reshape(t, d)
        x_bf = x2d.astype(BF16)
        x_cls = x.reshape(batch, i_len, CLASSES, d).transpose(2, 0, 1, 3).astype(BF16)
        x_cls = x_cls.reshape(CLASSES, batch * i_len, d)
        row = lambda a: a.reshape(1, -1)

        qkv_hm = _qkv_proj(x_cls, w_in[l], n=3 * attn_w, tn=1024)
        rest = _rest_proj(x_bf, w_in[l], col0=3 * attn_w, tm=1024, tn=1024)
        attn_hm = _dilated_attention(qkv_hm, batch)
        attn = attn_hm.reshape(N_HEADS, CLASSES, batch, i_len, HEAD_DIM)
        attn = attn.transpose(2, 3, 1, 0, 4).reshape(t, attn_w)
        hg = _rglru(rest.reshape(batch, seq, 4 * d), conv_w[l], row(conv_b[l]),
                    lru_wa[l].astype(BF16), row(lru_ba[l]), lru_wx[l].astype(BF16), row(lru_bx[l]),
                    row(lru_lambda[l]), d=d, ts=256, cw=512)
        merged = _merge(attn, hg.reshape(t, d), rest, w_o_attn[l].astype(BF16), w_o_lru[l].astype(BF16),
                        tm=1024, tn=512)
        rw_hi, rw_lo, rb = _router_weights(router_group_w[l], router_group_b[l],
                                           router_expert_w[l], router_expert_b[l])
        x1, x1p, route = _out_ln_route(merged, x2d, w_out[l].astype(BF16),
                                       row(ln1_g[l]), row(ln1_b[l]), rw_hi, rw_lo, rb, tm=512)

        n_tiles = 2 * t // MOE_TILE + N_EXPERTS
        tile_expert, gidx, dest = _moe_plan(route, n_tiles)
        y_rows = _moe_experts(x1p, tile_expert, gidx, dest, w_e_gate[l], w_e_up[l], w_e_down[l])
        x = _final_ln(x1, y_rows, route, row(ln2_g[l]), row(ln2_b[l]), tm=512)
        x = x.reshape(batch, seq, d)
    return x
```

```python
import functools
import math

import numpy as np
import jax
import jax.numpy as jnp
from jax import lax
from jax.experimental import pallas as pl
from jax.experimental.pallas import tpu as pltpu

F32 = jnp.float32
BF16 = jnp.bfloat16

N_HEADS = 16
HEAD_DIM = 128
N_BACK = 128
CLASSES = 16
CONV_WIDTH = 4
LRU_BLOCK_DIM = 128
LRU_C = 8.0
N_GROUPS = 4
EXPERTS_PER_GROUP = 8
N_EXPERTS = N_GROUPS * EXPERTS_PER_GROUP
DEPTH = 1
DEEPNORM_ALPHA = (2.0 * DEPTH) ** 0.25
LN_EPS = 1e-5
ATTN_SCALE_LOG2 = HEAD_DIM ** -0.5 * math.log2(math.e)

LANES = 128
SUBLANES = 8
VMEM_LIMIT_BYTES = 56 * 1024 * 1024

MASK_BIAS = -1e30
MOE_TILE = 256
ROUTER_LANES = LANES


def _cparams(sem):
    return pltpu.CompilerParams(dimension_semantics=sem, vmem_limit_bytes=VMEM_LIMIT_BYTES)


def _qkv_kernel(x_ref, w_ref, o_ref, w_sc):
    @pl.when(pl.program_id(1) == 0)
    def _():
        w_sc[...] = w_ref[...].astype(BF16)

    res = jnp.dot(x_ref[0], w_sc[...], preferred_element_type=F32).astype(o_ref.dtype)
    for hh in range(o_ref.shape[0]):
        o_ref[hh, 0] = res[:, hh * HEAD_DIM:(hh + 1) * HEAD_DIM]


def _qkv_proj(x_cls, w_in, *, n, tn):
    _, rows, d = x_cls.shape
    heads_per_step = tn // HEAD_DIM
    return pl.pallas_call(
        _qkv_kernel,
        grid=(n // tn, CLASSES),
        in_specs=[pl.BlockSpec((1, rows, d), lambda j, r: (r, 0, 0)),
                  pl.BlockSpec((d, tn), lambda j, r: (0, j))],
        out_specs=pl.BlockSpec((heads_per_step, 1, rows, HEAD_DIM), lambda j, r: (j, r, 0, 0)),
        out_shape=jax.ShapeDtypeStruct((n // HEAD_DIM, CLASSES, rows, HEAD_DIM), F32),
        scratch_shapes=[pltpu.VMEM((d, tn), BF16)],
        compiler_params=_cparams(("parallel", "arbitrary")),
        name="qkv_proj",
    )(x_cls, w_in)


def _rest_kernel(x_ref, w_ref, o_ref, w_sc):
    @pl.when(pl.program_id(1) == 0)
    def _():
        w_sc[...] = w_ref[...].astype(BF16)

    o_ref[...] = jnp.dot(x_ref[...], w_sc[...], preferred_element_type=F32).astype(o_ref.dtype)


def _rest_proj(x2d, w_in, *, col0, tm, tn):
    t, d = x2d.shape
    n = w_in.shape[1] - col0
    j0 = col0 // tn
    return pl.pallas_call(
        _rest_kernel,
        grid=(n // tn, t // tm),
        in_specs=[pl.BlockSpec((tm, d), lambda j, i: (i, 0)),
                  pl.BlockSpec((d, tn), lambda j, i: (0, j0 + j))],
        out_specs=pl.BlockSpec((tm, tn), lambda j, i: (i, j)),
        out_shape=jax.ShapeDtypeStruct((t, n), BF16),
        scratch_shapes=[pltpu.VMEM((d, tn), BF16)],
        compiler_params=_cparams(("parallel", "arbitrary")),
        name="rest_proj",
    )(x2d, w_in)


def _band_bias(dist):
    return np.where((dist >= 0) & (dist <= N_BACK), 0.0, MASK_BIAS).astype(np.float32)


Q_BLOCK = 128


def _attn_biases():
    lq = np.arange(Q_BLOCK)[:, None]
    b16_first = _band_bias(lq - np.arange(Q_BLOCK)[None, :])
    b16_next = _band_bias(Q_BLOCK + lq - np.arange(2 * Q_BLOCK)[None, :])
    cq = np.repeat(np.arange(4), 32)[:, None]
    lq = np.tile(np.arange(32), 4)[:, None]
    ck = np.repeat(np.arange(4), 64)[None, :]
    lk = np.tile(np.arange(64), 4)[None, :]
    b4 = np.stack([_band_bias(4 * (lq - lk) + (cq - ck)),
                   _band_bias(4 * (32 + lq - lk) + (cq - ck))])
    rq = np.repeat(np.arange(16), 8)[:, None]
    lq = np.tile(np.arange(8), 16)[:, None]
    rk = np.repeat(np.arange(16), 16)[None, :]
    lk = np.tile(np.arange(16), 16)[None, :]
    b1 = np.stack([_band_bias(16 * (lq - lk) + (rq - rk)),
                   _band_bias(16 * (8 + lq - lk) + (rq - rk))])
    return b16_first, b16_next, b4, b1


def _softmax_blocks(blocks):
    scores = [lax.dot_general(q, k, (((1,), (1,)), ((), ())), preferred_element_type=F32) * ATTN_SCALE_LOG2
              + bias for q, k, _, bias, _ in blocks]
    mids = []
    for s, (_, _, _, _, state) in zip(scores, blocks):
        rows, n = s.shape
        chunks = [s[:, c * LANES:(c + 1) * LANES] for c in range(n // LANES)]
        mx = functools.reduce(jnp.maximum, chunks)
        m_blk = jnp.broadcast_to(jnp.max(mx, axis=1, keepdims=True), (rows, LANES))
        m_new = m_blk if state is None else jnp.maximum(state[0], m_blk)
        ps = [jnp.exp2(c - m_new) for c in chunks]
        l_blk = jnp.broadcast_to(jnp.sum(functools.reduce(jnp.add, ps), axis=1, keepdims=True), (rows, LANES))
        mids.append((m_new, l_blk, jnp.concatenate(ps, axis=1).astype(BF16)))
    out = []
    for (m_new, l_blk, p), (_, _, v, _, state) in zip(mids, blocks):
        pv = jnp.dot(p, v, preferred_element_type=F32)
        if state is None:
            out.append((m_new, l_blk, pv))
        else:
            alpha = jnp.exp2(state[0] - m_new)
            out.append((m_new, alpha * state[1] + l_blk, alpha * state[2] + pv))
    return out


BLOCKS_PER_ITER = 16


def _attn_kernel(q_ref, k_ref, v_ref, b16a_ref, b16b_ref, b4_ref, b1_ref, o_ref,
                 qb, kb, vb, kb8, vb8, m_sc, l_sc, acc_sc):
    i_len = q_ref.shape[2]
    blocks_per_class = i_len // Q_BLOCK

    qb[...] = q_ref[0].astype(BF16)
    kb[...] = k_ref[0].astype(BF16)
    vb[...] = v_ref[0].astype(BF16)
    kb8[:, :i_len - SUBLANES, :] = k_ref[0, :, SUBLANES:, :].astype(BF16)
    vb8[:, :i_len - SUBLANES, :] = v_ref[0, :, SUBLANES:, :].astype(BF16)

    classes_per_iter = BLOCKS_PER_ITER // blocks_per_class

    def d16_body(it, carry):
        where, blocks = [], []
        for u in range(classes_per_iter):
            r = it * classes_per_iter + u
            for n in range(blocks_per_class):
                rows = slice(n * Q_BLOCK, (n + 1) * Q_BLOCK)
                keys = slice(0, Q_BLOCK) if n == 0 else slice((n - 1) * Q_BLOCK, (n + 1) * Q_BLOCK)
                bias = b16a_ref[...] if n == 0 else b16b_ref[...]
                where.append((r, rows))
                blocks.append((qb[r, rows, :], kb[r, keys, :], vb[r, keys, :], bias, None))
        for (r, rows), (m, l, acc) in zip(where, _softmax_blocks(blocks)):
            m_sc[r, rows, :] = m
            l_sc[r, rows, :] = l
            acc_sc[r, rows, :] = acc
        return carry

    lax.fori_loop(0, CLASSES // classes_per_iter, d16_body, 0)

    n_per_iter = BLOCKS_PER_ITER // 4

    def d4_body(it, carry):
        where, blocks = [], []
        for u in range(n_per_iter):
            n = it * n_per_iter + u
            q0 = pl.multiple_of(n * 32, 32)
            k0 = pl.multiple_of(jnp.maximum(n - 1, 0) * 32, 32)
            bias = b4_ref[jnp.minimum(n, 1)]
            for r4 in range(4):
                cls = [r4 + 4 * c for c in range(4)]
                cat = lambda ref, start, size, cls=cls: jnp.concatenate(
                    [ref[r, pl.ds(start, size), :] for r in cls], axis=0)
                state = (cat(m_sc, q0, 32), cat(l_sc, q0, 32), cat(acc_sc, q0, 32))
                where.append((cls, q0))
                blocks.append((cat(qb, q0, 32), cat(kb, k0, 64), cat(vb, k0, 64), bias, state))
        for (cls, q0), (m, l, acc) in zip(where, _softmax_blocks(blocks)):
            for c, r in enumerate(cls):
                m_sc[r, pl.ds(q0, 32), :] = m[32 * c:32 * (c + 1)]
                l_sc[r, pl.ds(q0, 32), :] = l[32 * c:32 * (c + 1)]
                acc_sc[r, pl.ds(q0, 32), :] = acc[32 * c:32 * (c + 1)]
        return carry

    lax.fori_loop(0, i_len // 32 // n_per_iter, d4_body, 0)

    def d1_body(it, carry):
        where, blocks = [], []
        for u in range(BLOCKS_PER_ITER):
            mb = it * BLOCKS_PER_ITER + u
            q0 = pl.multiple_of(mb * SUBLANES, SUBLANES)
            q = q_ref[0, :, pl.ds(q0, SUBLANES), :].reshape(Q_BLOCK, HEAD_DIM).astype(BF16)
            if u % 2 == 1:
                k0 = pl.multiple_of((mb - 1) * SUBLANES, 16)
                k, v = kb[:, pl.ds(k0, 16), :], vb[:, pl.ds(k0, 16), :]
            else:
                k0 = pl.multiple_of(jnp.maximum(mb - 2, 0) * SUBLANES, 16)
                k, v = kb8[:, pl.ds(k0, 16), :], vb8[:, pl.ds(k0, 16), :]
                if u == 0:
                    k = jnp.where(mb == 0, kb[:, 0:16, :], k)
                    v = jnp.where(mb == 0, vb[:, 0:16, :], v)
            flat = lambda ref, q0=q0: ref[:, pl.ds(q0, SUBLANES), :].reshape(Q_BLOCK, LANES)
            where.append(q0)
            blocks.append((q, k.reshape(2 * Q_BLOCK, HEAD_DIM), v.reshape(2 * Q_BLOCK, HEAD_DIM),
                           b1_ref[jnp.minimum(mb, 1)], (flat(m_sc), flat(l_sc), flat(acc_sc))))
        for q0, (m, l, acc) in zip(where, _softmax_blocks(blocks)):
            m_sc[:, pl.ds(q0, SUBLANES), :] = m.reshape(CLASSES, SUBLANES, LANES)
            l_sc[:, pl.ds(q0, SUBLANES), :] = l.reshape(CLASSES, SUBLANES, LANES)
            acc_sc[:, pl.ds(q0, SUBLANES), :] = acc.reshape(CLASSES, SUBLANES, LANES)
        return carry

    lax.fori_loop(0, i_len // SUBLANES // BLOCKS_PER_ITER, d1_body, 0)

    o_ref[0] = (acc_sc[...] / l_sc[...]).astype(o_ref.dtype)


def _dilated_attention(qkv_hm, batch):
    n3, _, bi, _ = qkv_hm.shape
    h = n3 // 3
    i_len = bi // batch
    assert CLASSES * SUBLANES == Q_BLOCK and i_len % Q_BLOCK == 0 and BLOCKS_PER_ITER % (i_len // Q_BLOCK) == 0
    biases = [jnp.asarray(a) for a in _attn_biases()]
    blk = (1, CLASSES, i_len, HEAD_DIM)
    const = lambda a: pl.BlockSpec(a.shape, lambda hh, bb: (0,) * a.ndim)
    operand = lambda: pltpu.VMEM((CLASSES, i_len, HEAD_DIM), BF16)
    state = lambda: pltpu.VMEM((CLASSES, i_len, LANES), F32)
    return pl.pallas_call(
        _attn_kernel,
        grid=(h, batch),
        in_specs=[pl.BlockSpec(blk, lambda hh, bb: (hh, 0, bb, 0)),
                  pl.BlockSpec(blk, lambda hh, bb: (h + hh, 0, bb, 0)),
                  pl.BlockSpec(blk, lambda hh, bb: (2 * h + hh, 0, bb, 0))] + [const(a) for a in biases],
        out_specs=pl.BlockSpec(blk, lambda hh, bb: (hh, 0, bb, 0)),
        out_shape=jax.ShapeDtypeStruct((h, CLASSES, bi, HEAD_DIM), BF16),
        scratch_shapes=[operand(), operand(), operand(), operand(), operand(), state(), state(), state()],
        compiler_params=_cparams(("parallel", "parallel")),
        name="dilated_attn",
    )(qkv_hm, qkv_hm, qkv_hm, *biases)


LOG2_E = math.log2(math.e)


def _softplus(z):
    return jnp.maximum(z, 0.0) + jnp.log1p(jnp.exp(-jnp.abs(z)))


def _sigmoid(z):
    return 1.0 / (1.0 + jnp.exp2(z * (-LOG2_E)))


def _lru_kernel(xr_ref, xg_ref, cw_ref, cb_ref, wa_ref, ba_ref, wx_ref, bx_ref, lam_ref,
                o_ref, h_sc, tail_sc):
    ts = xr_ref.shape[1]
    cw = xr_ref.shape[2]
    nv = ts // SUBLANES

    @pl.when(pl.program_id(2) == 0)
    def _():
        h_sc[...] = jnp.zeros_like(h_sc)
        tail_sc[...] = jnp.zeros_like(tail_sc)

    row = lax.broadcasted_iota(jnp.int32, (nv, SUBLANES, LANES), 1)
    for g in range(cw // LANES):
        sl = slice(g * LANES, (g + 1) * LANES)
        xr = xr_ref[0, :, sl].astype(F32)
        xfull = jnp.concatenate([tail_sc[:, sl], xr], axis=0)
        taps = cw_ref[:, sl]
        xc = xr * taps[3:4]
        for j in range(1, CONV_WIDTH):
            shifted = pltpu.roll(xfull, j, axis=0)[SUBLANES:]
            xc = xc + shifted * taps[3 - j:4 - j]
        xc = xc + cb_ref[:, sl]
        tail_sc[:, sl] = xr[ts - SUBLANES:]

        xcb = xc.astype(BF16)
        gate_a = jnp.dot(xcb, wa_ref[g], preferred_element_type=F32) + ba_ref[:, sl]
        gate_x = jnp.dot(xcb, wx_ref[g], preferred_element_type=F32) + bx_ref[:, sl]
        r = _sigmoid(gate_a)
        i = _sigmoid(gate_x)
        a = jnp.exp2(((-LRU_C * LOG2_E) * _softplus(-lam_ref[:, sl])) * r)
        y = 1.0 - a * a
        u = jnp.where(y > 0.0, y * lax.rsqrt(y), 0.0) * (i * xc)

        a3 = a.reshape(nv, SUBLANES, LANES)
        b3 = u.reshape(nv, SUBLANES, LANES)
        for k in (1, 2, 4):
            a_sh = jnp.where(row >= k, pltpu.roll(a3, k, axis=1), 1.0)
            b_sh = jnp.where(row >= k, pltpu.roll(b3, k, axis=1), 0.0)
            b3 = a3 * b_sh + b3
            a3 = a3 * a_sh
        h = h_sc[:, sl]
        outs = []
        for vi in range(nv):
            hv = a3[vi] * h + b3[vi]
            outs.append(hv)
            h = hv[SUBLANES - 1:SUBLANES]
        h_sc[:, sl] = h
        hs = jnp.concatenate(outs, axis=0)
        o_ref[0, :, sl] = (hs * jax.nn.gelu(xg_ref[0, :, sl].astype(F32))).astype(o_ref.dtype)


def _rglru(rest3, conv_w, conv_b, wa, ba, wx, bx, lam, *, d, ts, cw):
    b, s, _ = rest3.shape
    ncw = d // cw
    gpb = cw // LRU_BLOCK_DIM
    vec = lambda: pl.BlockSpec((1, cw), lambda bb, c, t: (0, c))
    return pl.pallas_call(
        _lru_kernel,
        grid=(b, ncw, s // ts),
        in_specs=[pl.BlockSpec((1, ts, cw), lambda bb, c, t: (bb, t, c)),
                  pl.BlockSpec((1, ts, cw), lambda bb, c, t: (bb, t, ncw + c)),
                  pl.BlockSpec((CONV_WIDTH, cw), lambda bb, c, t: (0, c)),
                  vec(),
                  pl.BlockSpec((gpb, LRU_BLOCK_DIM, LRU_BLOCK_DIM), lambda bb, c, t: (c, 0, 0)),
                  vec(),
                  pl.BlockSpec((gpb, LRU_BLOCK_DIM, LRU_BLOCK_DIM), lambda bb, c, t: (c, 0, 0)),
                  vec(), vec()],
        out_specs=pl.BlockSpec((1, ts, cw), lambda bb, c, t: (bb, t, c)),
        out_shape=jax.ShapeDtypeStruct((b, s, d), BF16),
        scratch_shapes=[pltpu.VMEM((1, cw), F32), pltpu.VMEM((SUBLANES, cw), F32)],
        compiler_params=_cparams(("parallel", "parallel", "arbitrary")),
        name="rglru",
    )(rest3, rest3, conv_w, conv_b, wa, ba, wx, bx, lam)


def _merge_kernel(a_ref, h_ref, ga_ref, gl_ref, wa_ref, wl_ref, o_ref):
    y_attn = jnp.dot(a_ref[...], wa_ref[...], preferred_element_type=F32)
    y_lru = jnp.dot(h_ref[...], wl_ref[...], preferred_element_type=F32)
    g_a = _sigmoid(ga_ref[...].astype(F32))
    g_l = _sigmoid(gl_ref[...].astype(F32))
    o_ref[...] = (g_a * y_attn + g_l * y_lru).astype(o_ref.dtype)


def _merge(attn, hg, rest, w_o_attn, w_o_lru, *, tm, tn):
    t, aw = attn.shape
    d = w_o_attn.shape[1]
    nj = d // tn
    return pl.pallas_call(
        _merge_kernel,
        grid=(t // tm, nj),
        in_specs=[pl.BlockSpec((tm, aw), lambda i, j: (i, 0)),
                  pl.BlockSpec((tm, d), lambda i, j: (i, 0)),
                  pl.BlockSpec((tm, tn), lambda i, j: (i, 2 * nj + j)),
                  pl.BlockSpec((tm, tn), lambda i, j: (i, 3 * nj + j)),
                  pl.BlockSpec((aw, tn), lambda i, j: (0, j)),
                  pl.BlockSpec((d, tn), lambda i, j: (0, j))],
        out_specs=pl.BlockSpec((tm, tn), lambda i, j: (i, j)),
        out_shape=jax.ShapeDtypeStruct((t, d), BF16),
        compiler_params=_cparams(("parallel", "parallel")),
        name="merge",
    )(attn, hg, rest, rest, w_o_attn, w_o_lru)


def _layer_norm(z, g, b):
    mu = jnp.mean(z, axis=-1, keepdims=True)
    zc = z - mu
    var = jnp.mean(zc * zc, axis=-1, keepdims=True)
    return zc * lax.rsqrt(var + LN_EPS) * g + b


def _out_ln_route_kernel(m_ref, x_ref, w_ref, g_ref, b_ref, rwh_ref, rwl_ref, rb_ref,
                         x1_ref, route_ref):
    y = jnp.dot(m_ref[...], w_ref[...], preferred_element_type=F32)
    z = DEEPNORM_ALPHA * x_ref[...] + y
    x1 = _layer_norm(z, g_ref[...], b_ref[...])
    x1_ref[...] = x1

    hi = x1.astype(BF16)
    lo = (x1 - hi.astype(F32)).astype(BF16)
    lg = (jnp.dot(hi, rwh_ref[...], preferred_element_type=F32)
          + jnp.dot(lo, rwh_ref[...], preferred_element_type=F32)
          + jnp.dot(hi, rwl_ref[...], preferred_element_type=F32)) + rb_ref[...]

    lane = lax.broadcasted_iota(jnp.int32, lg.shape, 1).astype(F32)
    first = lambda hit: jnp.min(jnp.where(hit, lane, float(ROUTER_LANES)), axis=1, keepdims=True)
    in_grp = lane < N_GROUPS
    lgg = jnp.where(in_grp, lg, MASK_BIAS)
    mg = jnp.max(lgg, axis=1, keepdims=True)
    g_idx = first(lgg == mg)
    p_grp = 1.0 / jnp.sum(jnp.where(in_grp, jnp.exp(lg - mg), 0.0), axis=1, keepdims=True)
    lo_lane = N_GROUPS + EXPERTS_PER_GROUP * g_idx
    in_exp = (lane >= lo_lane) & (lane < lo_lane + EXPERTS_PER_GROUP)
    le = jnp.where(in_exp, lg, MASK_BIAS)
    v1 = jnp.max(le, axis=1, keepdims=True)
    i1 = first((le == v1) & in_exp)
    rest = in_exp & (lane != i1)
    le2 = jnp.where(rest, lg, MASK_BIAS)
    v2 = jnp.max(le2, axis=1, keepdims=True)
    i2 = first((le2 == v2) & rest)
    t = jnp.exp(v2 - v1)
    w1 = p_grp / (1.0 + t)
    w2 = p_grp * t / (1.0 + t)
    route_ref[...] = jnp.where(lane == 0.0, i1 - N_GROUPS,
                     jnp.where(lane == 1.0, i2 - N_GROUPS,
                     jnp.where(lane == 2.0, w1,
                     jnp.where(lane == 3.0, w2, 0.0))))


def _out_ln_route(merged, x2d, w_out, ln_g, ln_b, rw_hi, rw_lo, rb, *, tm):
    t, d = merged.shape
    const = lambda a: pl.BlockSpec(a.shape, lambda i: (0,) * a.ndim)
    return pl.pallas_call(
        _out_ln_route_kernel,
        grid=(t // tm,),
        in_specs=[pl.BlockSpec((tm, d), lambda i: (i, 0)),
                  pl.BlockSpec((tm, d), lambda i: (i, 0)),
                  const(w_out), const(ln_g), const(ln_b), const(rw_hi), const(rw_lo), const(rb)],
        out_specs=[pl.BlockSpec((tm, d), lambda i: (i, 0)),
                   pl.BlockSpec((tm, ROUTER_LANES), lambda i: (i, 0))],
        out_shape=[jax.ShapeDtypeStruct((t, d), F32),
                   jax.ShapeDtypeStruct((t, ROUTER_LANES), F32)],
        compiler_params=_cparams(("parallel",)),
        name="out_ln_route",
    )(merged, x2d, w_out, ln_g, ln_b, rw_hi, rw_lo, rb)


def _moe_kernel(te_ref, nu_ref, g0_ref, gn_ref, dst_ref, x_hbm, wg_ref, wu_ref, wd_ref,
                y_hbm, xbuf, ybuf, wg_sc, wu_sc, wd_sc, gsem, ssem):
    i = pl.program_id(0)
    last = pl.num_programs(0) - 1
    n_used = nu_ref[0]
    tm = xbuf.shape[1]

    def gather_copy(tok, j, s):
        return pltpu.make_async_copy(x_hbm.at[pl.ds(tok, 1)], xbuf.at[s, pl.ds(j, 1)], gsem.at[s])

    def scatter_copy(row, j, s):
        return pltpu.make_async_copy(ybuf.at[s, pl.ds(j, 1)], y_hbm.at[pl.ds(row, 1)], ssem.at[s])

    @pl.when(i == 0)
    def _():
        for j in range(tm):
            gather_copy(g0_ref[0, 0, j], j, 0).start()

    @pl.when((i < n_used) & ((i == 0) | (te_ref[i] != te_ref[jnp.maximum(i - 1, 0)])))
    def _():
        wg_sc[...] = wg_ref[0].astype(BF16)
        wu_sc[...] = wu_ref[0].astype(BF16)
        wd_sc[...] = wd_ref[0].astype(BF16)

    def step(slot):
        nxt = 1 - slot

        @pl.when(i < n_used)
        def _():
            for j in range(tm):
                gather_copy(0, j, slot).wait()

            @pl.when(i + 1 < n_used)
            def _():
                for j in range(tm):
                    gather_copy(gn_ref[0, 0, j], j, nxt).start()

            xb = xbuf[slot].astype(BF16)
            gate = jnp.dot(xb, wg_sc[...], preferred_element_type=F32)
            up = jnp.dot(xb, wu_sc[...], preferred_element_type=F32)
            he = (gate * _sigmoid(gate) * up).astype(BF16)
            ybuf[slot] = jnp.dot(he, wd_sc[...], preferred_element_type=F32)

        for j in range(tm):
            scatter_copy(dst_ref[0, 0, j], j, slot).start()

        @pl.when(i > 0)
        def _():
            for j in range(tm):
                scatter_copy(0, j, nxt).wait()

        @pl.when(i == last)
        def _():
            for j in range(tm):
                scatter_copy(0, j, slot).wait()

    for parity in range(2):
        pl.when(lax.rem(i, 2) == parity)(functools.partial(step, parity))


def _moe_experts(x1, tile_expert, n_used, gidx, dest, wg, wu, wd):
    t, d = x1.shape
    nt = gidx.shape[0]
    f = wg.shape[2]
    idx_spec = lambda fn: pl.BlockSpec((1, 1, MOE_TILE), fn, memory_space=pltpu.SMEM)
    grid_spec = pltpu.PrefetchScalarGridSpec(
        num_scalar_prefetch=2,
        grid=(nt,),
        in_specs=[idx_spec(lambda i, te, nu: (0, 0, 0)),
                  idx_spec(lambda i, te, nu: (jnp.minimum(i + 1, nt - 1), 0, 0)),
                  idx_spec(lambda i, te, nu: (i, 0, 0)),
                  pl.BlockSpec(memory_space=pl.ANY),
                  pl.BlockSpec((1, d, f), lambda i, te, nu: (te[i], 0, 0)),
                  pl.BlockSpec((1, d, f), lambda i, te, nu: (te[i], 0, 0)),
                  pl.BlockSpec((1, f, d), lambda i, te, nu: (te[i], 0, 0))],
        out_specs=pl.BlockSpec(memory_space=pl.ANY),
        scratch_shapes=[pltpu.VMEM((2, MOE_TILE, d), F32),
                        pltpu.VMEM((2, MOE_TILE, d), F32),
                        pltpu.VMEM((d, f), BF16),
                        pltpu.VMEM((d, f), BF16),
                        pltpu.VMEM((f, d), BF16),
                        pltpu.SemaphoreType.DMA((2,)),
                        pltpu.SemaphoreType.DMA((2,))],
    )
    return pl.pallas_call(
        _moe_kernel,
        grid_spec=grid_spec,
        out_shape=jax.ShapeDtypeStruct((nt * MOE_TILE, d), F32),
        compiler_params=_cparams(("arbitrary",)),
        name="moe_experts",
    )(tile_expert, n_used, gidx, gidx, dest, x1, wg, wu, wd)


KEY_IDX_BITS = 16


def _moe_plan(route, n_tiles):
    t = route.shape[0]
    p = 2 * t
    assert p <= 1 << KEY_IDX_BITS and n_tiles * MOE_TILE == p + N_EXPERTS * MOE_TILE
    e_flat = jnp.concatenate([route[:, 0], route[:, 1]]).astype(jnp.int32)
    experts = jnp.arange(N_EXPERTS, dtype=jnp.int32)
    counts = jnp.sum((e_flat[:, None] == experts[None, :]).astype(jnp.int32), axis=0)
    n_pad = (-counts) % MOE_TILE
    k = jnp.arange(MOE_TILE, dtype=jnp.int32)
    pad_expert = jnp.where(k[None, :] < n_pad[:, None], experts[:, None], N_EXPERTS)
    pad_id = experts[:, None] * MOE_TILE + k[None, :]
    flag = 1 << KEY_IDX_BITS
    keys = jnp.concatenate([e_flat * (2 * flag) + jnp.arange(p, dtype=jnp.int32),
                            (pad_expert * (2 * flag) + flag + pad_id).reshape(-1)])
    keys = jnp.sort(keys)
    is_pad = (keys & flag) != 0
    idx = keys & (flag - 1)
    dest = jnp.where(is_pad, p + idx, idx)
    gidx = jnp.where(is_pad, 0, jnp.where(idx >= t, idx - t, idx))
    first_expert = keys.reshape(n_tiles, MOE_TILE)[:, 0] // (2 * flag)
    n_used = jnp.sum((first_expert < N_EXPERTS).astype(jnp.int32)).reshape(1)
    tile_expert = jnp.minimum(first_expert, N_EXPERTS - 1)
    shape3 = (n_tiles, 1, MOE_TILE)
    return tile_expert, n_used, gidx.reshape(shape3), dest.reshape(shape3)


def _final_kernel(x1_ref, y0_ref, y1_ref, route_ref, g_ref, b_ref, o_ref):
    y = route_ref[:, 2:3] * y0_ref[...] + route_ref[:, 3:4] * y1_ref[...]
    z = DEEPNORM_ALPHA * x1_ref[...] + y
    o_ref[...] = _layer_norm(z, g_ref[...], b_ref[...])


def _final_ln(x1, y_rows, route, ln_g, ln_b, *, tm):
    t, d = x1.shape
    nb = t // tm
    const = lambda a: pl.BlockSpec(a.shape, lambda i: (0,) * a.ndim)
    return pl.pallas_call(
        _final_kernel,
        grid=(nb,),
        in_specs=[pl.BlockSpec((tm, d), lambda i: (i, 0)),
                  pl.BlockSpec((tm, d), lambda i: (i, 0)),
                  pl.BlockSpec((tm, d), lambda i: (nb + i, 0)),
                  pl.BlockSpec((tm, ROUTER_LANES), lambda i: (i, 0)),
                  const(ln_g), const(ln_b)],
        out_specs=pl.BlockSpec((tm, d), lambda i: (i, 0)),
        out_shape=jax.ShapeDtypeStruct((t, d), F32),
        compiler_params=_cparams(("parallel",)),
        name="final_ln",
    )(x1, y_rows, y_rows, route, ln_g, ln_b)


def _router_weights(router_group_w, router_group_b, router_expert_w, router_expert_b):
    d = router_group_w.shape[0]
    pad = ROUTER_LANES - N_GROUPS - N_EXPERTS
    w = jnp.concatenate([router_group_w, router_expert_w, jnp.zeros((d, pad), F32)], axis=1)
    b = jnp.concatenate([router_group_b, router_expert_b, jnp.zeros((pad,), F32)]).reshape(1, ROUTER_LANES)
    w_hi = w.astype(BF16)
    w_lo = (w - w_hi.astype(F32)).astype(BF16)
    return w_hi, w_lo, b


def kernel(x, w_in, conv_w, conv_b, lru_wa, lru_ba, lru_wx, lru_bx, lru_lambda, w_o_attn, w_o_lru, w_out, ln1_g, ln1_b, router_group_w, router_group_b, router_expert_w, router_expert_b, w_e_gate, w_e_up, w_e_down, ln2_g, ln2_b):
    batch, seq, d = x.shape
    i_len = seq // CLASSES
    t = batch * seq
    attn_w = N_HEADS * HEAD_DIM
    for l in range(DEPTH):
        x2d = x.reshape(t, d)
        x_bf = x2d.astype(BF16)
        x_cls = x.reshape(batch, i_len, CLASSES, d).transpose(2, 0, 1, 3).astype(BF16)
        x_cls = x_cls.reshape(CLASSES, batch * i_len, d)
        row = lambda a: a.reshape(1, -1)

        qkv_hm = _qkv_proj(x_cls, w_in[l], n=3 * attn_w, tn=1024)
        rest = _rest_proj(x_bf, w_in[l], col0=3 * attn_w, tm=1024, tn=1024)
        attn_hm = _dilated_attention(qkv_hm, batch)
        attn = attn_hm.reshape(N_HEADS, CLASSES, batch, i_len, HEAD_DIM)
        attn = attn.transpose(2, 3, 1, 0, 4).reshape(t, attn_w)
        hg = _rglru(rest.reshape(batch, seq, 4 * d), conv_w[l], row(conv_b[l]),
                    lru_wa[l].astype(BF16), row(lru_ba[l]), lru_wx[l].astype(BF16), row(lru_bx[l]),
                    row(lru_lambda[l]), d=d, ts=256, cw=512)
        merged = _merge(attn, hg.reshape(t, d), rest, w_o_attn[l].astype(BF16), w_o_lru[l].astype(BF16),
                        tm=1024, tn=512)
        rw_hi, rw_lo, rb = _router_weights(router_group_w[l], router_group_b[l],
                                           router_expert_w[l], router_expert_b[l])
        x1, route = _out_ln_route(merged, x2d, w_out[l].astype(BF16),
                                  row(ln1_g[l]), row(ln1_b[l]), rw_hi, rw_lo, rb, tm=512)

        n_tiles = 2 * t // MOE_TILE + N_EXPERTS
        tile_expert, n_used, gidx, dest = _moe_plan(route, n_tiles)
        y_rows = _moe_experts(x1, tile_expert, n_used, gidx, dest, w_e_gate[l], w_e_up[l], w_e_down[l])
        x = _final_ln(x1, y_rows, route, row(ln2_g[l]), row(ln2_b[l]), tm=512)
        x = x.reshape(batch, seq, d)
    return x
```

```python
import functools
import math

import numpy as np
import jax
import jax.numpy as jnp
from jax import lax
from jax.experimental import pallas as pl
from jax.experimental.pallas import tpu as pltpu

F32 = jnp.float32
BF16 = jnp.bfloat16

N_HEADS = 16
HEAD_DIM = 128
N_BACK = 128
CLASSES = 16
CONV_WIDTH = 4
LRU_BLOCK_DIM = 128
LRU_C = 8.0
N_GROUPS = 4
EXPERTS_PER_GROUP = 8
N_EXPERTS = N_GROUPS * EXPERTS_PER_GROUP
DEPTH = 1
DEEPNORM_ALPHA = (2.0 * DEPTH) ** 0.25
LN_EPS = 1e-5
ATTN_SCALE_LOG2 = HEAD_DIM ** -0.5 * math.log2(math.e)

LANES = 128
SUBLANES = 8
VMEM_LIMIT_BYTES = 56 * 1024 * 1024

MASK_BIAS = -1e30
MOE_TILE = 256
ROUTER_LANES = LANES


def _cparams(sem):
    return pltpu.CompilerParams(dimension_semantics=sem, vmem_limit_bytes=VMEM_LIMIT_BYTES)


def _class_order_kernel(x_hbm, o_ref, buf, sem):
    nb = pl.num_programs(1)
    n = pl.program_id(0) * nb + pl.program_id(1)
    slot = lax.rem(n, 2)

    def copy(step, buf_slot):
        return pltpu.make_async_copy(x_hbm.at[lax.rem(step, nb), :, step // nb, :], buf.at[buf_slot],
                                     sem.at[buf_slot])

    @pl.when(n == 0)
    def _():
        copy(n, slot).start()

    @pl.when(n + 1 < pl.num_programs(0) * nb)
    def _():
        copy(n + 1, 1 - slot).start()

    copy(n, slot).wait()
    o_ref[0, 0] = buf[slot].astype(o_ref.dtype)


def _to_class_order(x4):
    b, i, c, d = x4.shape
    return pl.pallas_call(
        _class_order_kernel,
        grid=(c, b),
        in_specs=[pl.BlockSpec(memory_space=pl.ANY)],
        out_specs=pl.BlockSpec((1, 1, i, d), lambda r, bb: (r, bb, 0, 0)),
        out_shape=jax.ShapeDtypeStruct((c, b, i, d), BF16),
        scratch_shapes=[pltpu.VMEM((2, i, d), F32), pltpu.SemaphoreType.DMA((2,))],
        compiler_params=_cparams(("arbitrary", "arbitrary")),
        name="class_order",
    )(x4)


def _qkv_kernel(x_ref, w_ref, o_ref, w_sc):
    @pl.when(pl.program_id(1) == 0)
    def _():
        w_sc[...] = w_ref[...].astype(BF16)

    res = jnp.dot(x_ref[0], w_sc[...], preferred_element_type=F32).astype(o_ref.dtype)
    for hh in range(o_ref.shape[0]):
        o_ref[hh, 0] = res[:, hh * HEAD_DIM:(hh + 1) * HEAD_DIM]


def _qkv_proj(x_cls, w_in, *, n, tn):
    _, rows, d = x_cls.shape
    heads_per_step = tn // HEAD_DIM
    return pl.pallas_call(
        _qkv_kernel,
        grid=(n // tn, CLASSES),
        in_specs=[pl.BlockSpec((1, rows, d), lambda j, r: (r, 0, 0)),
                  pl.BlockSpec((d, tn), lambda j, r: (0, j))],
        out_specs=pl.BlockSpec((heads_per_step, 1, rows, HEAD_DIM), lambda j, r: (j, r, 0, 0)),
        out_shape=jax.ShapeDtypeStruct((n // HEAD_DIM, CLASSES, rows, HEAD_DIM), F32),
        scratch_shapes=[pltpu.VMEM((d, tn), BF16)],
        compiler_params=_cparams(("parallel", "arbitrary")),
        name="qkv_proj",
    )(x_cls, w_in)


def _rest_kernel(x_ref, w_ref, o_ref, w_sc):
    @pl.when(pl.program_id(1) == 0)
    def _():
        w_sc[...] = w_ref[...].astype(BF16)

    o_ref[...] = jnp.dot(x_ref[...], w_sc[...], preferred_element_type=F32).astype(o_ref.dtype)


def _rest_proj(x2d, w_in, *, col0, tm, tn):
    t, d = x2d.shape
    n = w_in.shape[1] - col0
    j0 = col0 // tn
    return pl.pallas_call(
        _rest_kernel,
        grid=(n // tn, t // tm),
        in_specs=[pl.BlockSpec((tm, d), lambda j, i: (i, 0)),
                  pl.BlockSpec((d, tn), lambda j, i: (0, j0 + j))],
        out_specs=pl.BlockSpec((tm, tn), lambda j, i: (i, j)),
        out_shape=jax.ShapeDtypeStruct((t, n), BF16),
        scratch_shapes=[pltpu.VMEM((d, tn), BF16)],
        compiler_params=_cparams(("parallel", "arbitrary")),
        name="rest_proj",
    )(x2d, w_in)


def _band_bias(dist):
    return np.where((dist >= 0) & (dist <= N_BACK), 0.0, MASK_BIAS).astype(np.float32)


Q_BLOCK = 128


def _attn_biases():
    lq = np.arange(Q_BLOCK)[:, None]
    b16_first = _band_bias(lq - np.arange(Q_BLOCK)[None, :])
    b16_next = _band_bias(Q_BLOCK + lq - np.arange(2 * Q_BLOCK)[None, :])
    cq = np.repeat(np.arange(4), 32)[:, None]
    lq = np.tile(np.arange(32), 4)[:, None]
    ck = np.repeat(np.arange(4), 64)[None, :]
    lk = np.tile(np.arange(64), 4)[None, :]
    b4 = np.stack([_band_bias(4 * (lq - lk) + (cq - ck)),
                   _band_bias(4 * (32 + lq - lk) + (cq - ck))])
    rq = np.repeat(np.arange(16), 8)[:, None]
    lq = np.tile(np.arange(8), 16)[:, None]
    rk = np.repeat(np.arange(16), 16)[None, :]
    lk = np.tile(np.arange(16), 16)[None, :]
    b1 = np.stack([_band_bias(16 * (lq - lk) + (rq - rk)),
                   _band_bias(16 * (8 + lq - lk) + (rq - rk))])
    return b16_first, b16_next, b4, b1


def _softmax_blocks(blocks):
    scores = [lax.dot_general(q, k, (((1,), (1,)), ((), ())), preferred_element_type=F32) * ATTN_SCALE_LOG2
              + bias for q, k, _, bias, _ in blocks]
    mids = []
    for s, (_, _, _, _, state) in zip(scores, blocks):
        rows, n = s.shape
        chunks = [s[:, c * LANES:(c + 1) * LANES] for c in range(n // LANES)]
        mx = functools.reduce(jnp.maximum, chunks)
        m_blk = jnp.broadcast_to(jnp.max(mx, axis=1, keepdims=True), (rows, LANES))
        m_new = m_blk if state is None else jnp.maximum(state[0], m_blk)
        ps = [jnp.exp2(c - m_new) for c in chunks]
        l_blk = jnp.broadcast_to(jnp.sum(functools.reduce(jnp.add, ps), axis=1, keepdims=True), (rows, LANES))
        mids.append((m_new, l_blk, jnp.concatenate(ps, axis=1).astype(BF16)))
    out = []
    for (m_new, l_blk, p), (_, _, v, _, state) in zip(mids, blocks):
        pv = jnp.dot(p, v, preferred_element_type=F32)
        if state is None:
            out.append((m_new, l_blk, pv))
        else:
            alpha = jnp.exp2(state[0] - m_new)
            out.append((m_new, alpha * state[1] + l_blk, alpha * state[2] + pv))
    return out


BLOCKS_PER_ITER = 16


def _attn_kernel(q_ref, k_ref, v_ref, b16a_ref, b16b_ref, b4_ref, b1_ref, o_hbm,
                 qb, kb, vb, kb8, vb8, m_sc, l_sc, acc_sc, o_sc, o_sem):
    i_len = q_ref.shape[2]
    blocks_per_class = i_len // Q_BLOCK

    qb[...] = q_ref[0].astype(BF16)
    kb[...] = k_ref[0].astype(BF16)
    vb[...] = v_ref[0].astype(BF16)
    kb8[:, :i_len - SUBLANES, :] = k_ref[0, :, SUBLANES:, :].astype(BF16)
    vb8[:, :i_len - SUBLANES, :] = v_ref[0, :, SUBLANES:, :].astype(BF16)

    classes_per_iter = BLOCKS_PER_ITER // blocks_per_class

    def d16_body(it, carry):
        where, blocks = [], []
        for u in range(classes_per_iter):
            r = it * classes_per_iter + u
            for n in range(blocks_per_class):
                rows = slice(n * Q_BLOCK, (n + 1) * Q_BLOCK)
                keys = slice(0, Q_BLOCK) if n == 0 else slice((n - 1) * Q_BLOCK, (n + 1) * Q_BLOCK)
                bias = b16a_ref[...] if n == 0 else b16b_ref[...]
                where.append((r, rows))
                blocks.append((qb[r, rows, :], kb[r, keys, :], vb[r, keys, :], bias, None))
        for (r, rows), (m, l, acc) in zip(where, _softmax_blocks(blocks)):
            m_sc[r, rows, :] = m
            l_sc[r, rows, :] = l
            acc_sc[r, rows, :] = acc
        return carry

    lax.fori_loop(0, CLASSES // classes_per_iter, d16_body, 0)

    n_per_iter = BLOCKS_PER_ITER // 4

    def d4_body(it, carry):
        where, blocks = [], []
        for u in range(n_per_iter):
            n = it * n_per_iter + u
            q0 = pl.multiple_of(n * 32, 32)
            k0 = pl.multiple_of(jnp.maximum(n - 1, 0) * 32, 32)
            bias = b4_ref[jnp.minimum(n, 1)]
            for r4 in range(4):
                cls = [r4 + 4 * c for c in range(4)]
                cat = lambda ref, start, size, cls=cls: jnp.concatenate(
                    [ref[r, pl.ds(start, size), :] for r in cls], axis=0)
                state = (cat(m_sc, q0, 32), cat(l_sc, q0, 32), cat(acc_sc, q0, 32))
                where.append((cls, q0))
                blocks.append((cat(qb, q0, 32), cat(kb, k0, 64), cat(vb, k0, 64), bias, state))
        for (cls, q0), (m, l, acc) in zip(where, _softmax_blocks(blocks)):
            for c, r in enumerate(cls):
                m_sc[r, pl.ds(q0, 32), :] = m[32 * c:32 * (c + 1)]
                l_sc[r, pl.ds(q0, 32), :] = l[32 * c:32 * (c + 1)]
                acc_sc[r, pl.ds(q0, 32), :] = acc[32 * c:32 * (c + 1)]
        return carry

    lax.fori_loop(0, i_len // 32 // n_per_iter, d4_body, 0)

    def d1_body(it, carry):
        where, blocks = [], []
        for u in range(BLOCKS_PER_ITER):
            mb = it * BLOCKS_PER_ITER + u
            q0 = pl.multiple_of(mb * SUBLANES, SUBLANES)
            q = q_ref[0, :, pl.ds(q0, SUBLANES), :].reshape(Q_BLOCK, HEAD_DIM).astype(BF16)
            if u % 2 == 1:
                k0 = pl.multiple_of((mb - 1) * SUBLANES, 16)
                k, v = kb[:, pl.ds(k0, 16), :], vb[:, pl.ds(k0, 16), :]
            else:
                k0 = pl.multiple_of(jnp.maximum(mb - 2, 0) * SUBLANES, 16)
                k, v = kb8[:, pl.ds(k0, 16), :], vb8[:, pl.ds(k0, 16), :]
                if u == 0:
                    k = jnp.where(mb == 0, kb[:, 0:16, :], k)
                    v = jnp.where(mb == 0, vb[:, 0:16, :], v)
            flat = lambda ref, q0=q0: ref[:, pl.ds(q0, SUBLANES), :].reshape(Q_BLOCK, LANES)
            where.append(q0)
            blocks.append((q, k.reshape(2 * Q_BLOCK, HEAD_DIM), v.reshape(2 * Q_BLOCK, HEAD_DIM),
                           b1_ref[jnp.minimum(mb, 1)], (flat(m_sc), flat(l_sc), flat(acc_sc))))
        for q0, (m, l, acc) in zip(where, _softmax_blocks(blocks)):
            m_sc[:, pl.ds(q0, SUBLANES), :] = m.reshape(CLASSES, SUBLANES, LANES)
            l_sc[:, pl.ds(q0, SUBLANES), :] = l.reshape(CLASSES, SUBLANES, LANES)
            acc_sc[:, pl.ds(q0, SUBLANES), :] = acc.reshape(CLASSES, SUBLANES, LANES)
        return carry

    lax.fori_loop(0, i_len // SUBLANES // BLOCKS_PER_ITER, d1_body, 0)

    hh, bb = pl.program_id(0), pl.program_id(1)
    step = hh * pl.num_programs(1) + bb
    n_steps = pl.num_programs(0) * pl.num_programs(1)
    col = pl.multiple_of(hh * HEAD_DIM, HEAD_DIM)

    def out_copy(r, b_idx, col0):
        return pltpu.make_async_copy(o_sc.at[r], o_hbm.at[b_idx, :, r, pl.ds(col0, HEAD_DIM)], o_sem)

    @pl.when(step > 0)
    def _():
        for r in range(CLASSES):
            out_copy(r, 0, 0).wait()

    o_sc[...] = acc_sc[...] / l_sc[...]
    for r in range(CLASSES):
        out_copy(r, bb, col).start()

    @pl.when(step == n_steps - 1)
    def _():
        for r in range(CLASSES):
            out_copy(r, 0, 0).wait()


def _dilated_attention(qkv_hm, batch):
    n3, _, bi, _ = qkv_hm.shape
    h = n3 // 3
    i_len = bi // batch
    assert CLASSES * SUBLANES == Q_BLOCK and i_len % Q_BLOCK == 0 and BLOCKS_PER_ITER % (i_len // Q_BLOCK) == 0
    biases = [jnp.asarray(a) for a in _attn_biases()]
    blk = (1, CLASSES, i_len, HEAD_DIM)
    const = lambda a: pl.BlockSpec(a.shape, lambda hh, bb: (0,) * a.ndim)
    operand = lambda: pltpu.VMEM((CLASSES, i_len, HEAD_DIM), BF16)
    state = lambda: pltpu.VMEM((CLASSES, i_len, LANES), F32)
    return pl.pallas_call(
        _attn_kernel,
        grid=(h, batch),
        in_specs=[pl.BlockSpec(blk, lambda hh, bb: (hh, 0, bb, 0)),
                  pl.BlockSpec(blk, lambda hh, bb: (h + hh, 0, bb, 0)),
                  pl.BlockSpec(blk, lambda hh, bb: (2 * h + hh, 0, bb, 0))] + [const(a) for a in biases],
        out_specs=pl.BlockSpec(memory_space=pl.ANY),
        out_shape=jax.ShapeDtypeStruct((batch, i_len, CLASSES, h * HEAD_DIM), F32),
        scratch_shapes=[operand(), operand(), operand(), operand(), operand(), state(), state(), state(),
                        pltpu.VMEM((CLASSES, i_len, HEAD_DIM), F32), pltpu.SemaphoreType.DMA(())],
        compiler_params=_cparams(("arbitrary", "arbitrary")),
        name="dilated_attn",
    )(qkv_hm, qkv_hm, qkv_hm, *biases)


LOG2_E = math.log2(math.e)


def _softplus(z):
    return jnp.maximum(z, 0.0) + jnp.log1p(jnp.exp(-jnp.abs(z)))


def _sigmoid(z):
    return 1.0 / (1.0 + jnp.exp2(z * (-LOG2_E)))


def _lru_kernel(xr_ref, xg_ref, cw_ref, cb_ref, wa_ref, ba_ref, wx_ref, bx_ref, lam_ref,
                o_ref, h_sc, tail_sc):
    ts = xr_ref.shape[1]
    cw = xr_ref.shape[2]
    nv = ts // SUBLANES

    @pl.when(pl.program_id(2) == 0)
    def _():
        h_sc[...] = jnp.zeros_like(h_sc)
        tail_sc[...] = jnp.zeros_like(tail_sc)

    row = lax.broadcasted_iota(jnp.int32, (nv, SUBLANES, LANES), 1)
    for g in range(cw // LANES):
        sl = slice(g * LANES, (g + 1) * LANES)
        xr = xr_ref[0, :, sl].astype(F32)
        xfull = jnp.concatenate([tail_sc[:, sl], xr], axis=0)
        taps = cw_ref[:, sl]
        xc = xr * taps[3:4]
        for j in range(1, CONV_WIDTH):
            shifted = pltpu.roll(xfull, j, axis=0)[SUBLANES:]
            xc = xc + shifted * taps[3 - j:4 - j]
        xc = xc + cb_ref[:, sl]
        tail_sc[:, sl] = xr[ts - SUBLANES:]

        xcb = xc.astype(BF16)
        gate_a = jnp.dot(xcb, wa_ref[g], preferred_element_type=F32) + ba_ref[:, sl]
        gate_x = jnp.dot(xcb, wx_ref[g], preferred_element_type=F32) + bx_ref[:, sl]
        r = _sigmoid(gate_a)
        i = _sigmoid(gate_x)
        a = jnp.exp2(((-LRU_C * LOG2_E) * _softplus(-lam_ref[:, sl])) * r)
        y = 1.0 - a * a
        u = jnp.where(y > 0.0, y * lax.rsqrt(y), 0.0) * (i * xc)

        a3 = a.reshape(nv, SUBLANES, LANES)
        b3 = u.reshape(nv, SUBLANES, LANES)
        for k in (1, 2, 4):
            a_sh = jnp.where(row >= k, pltpu.roll(a3, k, axis=1), 1.0)
            b_sh = jnp.where(row >= k, pltpu.roll(b3, k, axis=1), 0.0)
            b3 = a3 * b_sh + b3
            a3 = a3 * a_sh
        h = h_sc[:, sl]
        outs = []
        for vi in range(nv):
            hv = a3[vi] * h + b3[vi]
            outs.append(hv)
            h = hv[SUBLANES - 1:SUBLANES]
        h_sc[:, sl] = h
        hs = jnp.concatenate(outs, axis=0)
        o_ref[0, :, sl] = (hs * jax.nn.gelu(xg_ref[0, :, sl].astype(F32))).astype(o_ref.dtype)


def _rglru(rest3, conv_w, conv_b, wa, ba, wx, bx, lam, *, d, ts, cw):
    b, s, _ = rest3.shape
    ncw = d // cw
    gpb = cw // LRU_BLOCK_DIM
    vec = lambda: pl.BlockSpec((1, cw), lambda bb, c, t: (0, c))
    return pl.pallas_call(
        _lru_kernel,
        grid=(b, ncw, s // ts),
        in_specs=[pl.BlockSpec((1, ts, cw), lambda bb, c, t: (bb, t, c)),
                  pl.BlockSpec((1, ts, cw), lambda bb, c, t: (bb, t, ncw + c)),
                  pl.BlockSpec((CONV_WIDTH, cw), lambda bb, c, t: (0, c)),
                  vec(),
                  pl.BlockSpec((gpb, LRU_BLOCK_DIM, LRU_BLOCK_DIM), lambda bb, c, t: (c, 0, 0)),
                  vec(),
                  pl.BlockSpec((gpb, LRU_BLOCK_DIM, LRU_BLOCK_DIM), lambda bb, c, t: (c, 0, 0)),
                  vec(), vec()],
        out_specs=pl.BlockSpec((1, ts, cw), lambda bb, c, t: (bb, t, c)),
        out_shape=jax.ShapeDtypeStruct((b, s, d), BF16),
        scratch_shapes=[pltpu.VMEM((1, cw), F32), pltpu.VMEM((SUBLANES, cw), F32)],
        compiler_params=_cparams(("parallel", "parallel", "arbitrary")),
        name="rglru",
    )(rest3, rest3, conv_w, conv_b, wa, ba, wx, bx, lam)


def _merge_kernel(a_ref, h_ref, ga_ref, gl_ref, wa_ref, wl_ref, o_ref, a_sc):
    @pl.when(pl.program_id(1) == 0)
    def _():
        a_sc[...] = a_ref[...].astype(BF16)

    y_attn = jnp.dot(a_sc[...], wa_ref[...], preferred_element_type=F32)
    y_lru = jnp.dot(h_ref[...], wl_ref[...], preferred_element_type=F32)
    g_a = _sigmoid(ga_ref[...].astype(F32))
    g_l = _sigmoid(gl_ref[...].astype(F32))
    o_ref[...] = (g_a * y_attn + g_l * y_lru).astype(o_ref.dtype)


def _merge(attn, hg, rest, w_o_attn, w_o_lru, *, tm, tn):
    t, aw = attn.shape
    d = w_o_attn.shape[1]
    nj = d // tn
    return pl.pallas_call(
        _merge_kernel,
        grid=(t // tm, nj),
        in_specs=[pl.BlockSpec((tm, aw), lambda i, j: (i, 0)),
                  pl.BlockSpec((tm, d), lambda i, j: (i, 0)),
                  pl.BlockSpec((tm, tn), lambda i, j: (i, 2 * nj + j)),
                  pl.BlockSpec((tm, tn), lambda i, j: (i, 3 * nj + j)),
                  pl.BlockSpec((aw, tn), lambda i, j: (0, j)),
                  pl.BlockSpec((d, tn), lambda i, j: (0, j))],
        out_specs=pl.BlockSpec((tm, tn), lambda i, j: (i, j)),
        out_shape=jax.ShapeDtypeStruct((t, d), BF16),
        scratch_shapes=[pltpu.VMEM((tm, aw), BF16)],
        compiler_params=_cparams(("parallel", "arbitrary")),
        name="merge",
    )(attn, hg, rest, rest, w_o_attn, w_o_lru)


def _layer_norm(z, g, b):
    mu = jnp.mean(z, axis=-1, keepdims=True)
    zc = z - mu
    var = jnp.mean(zc * zc, axis=-1, keepdims=True)
    return zc * lax.rsqrt(var + LN_EPS) * g + b


def _out_ln_route_kernel(m_ref, x_ref, w_ref, g_ref, b_ref, rwh_ref, rwl_ref, rb_ref,
                         x1_ref, route_ref):
    y = jnp.dot(m_ref[...], w_ref[...], preferred_element_type=F32)
    z = DEEPNORM_ALPHA * x_ref[...] + y
    x1 = _layer_norm(z, g_ref[...], b_ref[...])
    x1_ref[...] = x1

    hi = x1.astype(BF16)
    lo = (x1 - hi.astype(F32)).astype(BF16)
    lg = (jnp.dot(hi, rwh_ref[...], preferred_element_type=F32)
          + jnp.dot(lo, rwh_ref[...], preferred_element_type=F32)
          + jnp.dot(hi, rwl_ref[...], preferred_element_type=F32)) + rb_ref[...]

    lane = lax.broadcasted_iota(jnp.int32, lg.shape, 1).astype(F32)
    first = lambda hit: jnp.min(jnp.where(hit, lane, float(ROUTER_LANES)), axis=1, keepdims=True)
    in_grp = lane < N_GROUPS
    lgg = jnp.where(in_grp, lg, MASK_BIAS)
    mg = jnp.max(lgg, axis=1, keepdims=True)
    g_idx = first(lgg == mg)
    p_grp = 1.0 / jnp.sum(jnp.where(in_grp, jnp.exp(lg - mg), 0.0), axis=1, keepdims=True)
    lo_lane = N_GROUPS + EXPERTS_PER_GROUP * g_idx
    in_exp = (lane >= lo_lane) & (lane < lo_lane + EXPERTS_PER_GROUP)
    le = jnp.where(in_exp, lg, MASK_BIAS)
    v1 = jnp.max(le, axis=1, keepdims=True)
    i1 = first((le == v1) & in_exp)
    rest = in_exp & (lane != i1)
    le2 = jnp.where(rest, lg, MASK_BIAS)
    v2 = jnp.max(le2, axis=1, keepdims=True)
    i2 = first((le2 == v2) & rest)
    t = jnp.exp(v2 - v1)
    w1 = p_grp / (1.0 + t)
    w2 = p_grp * t / (1.0 + t)
    route_ref[...] = jnp.where(lane == 0.0, i1 - N_GROUPS,
                     jnp.where(lane == 1.0, i2 - N_GROUPS,
                     jnp.where(lane == 2.0, w1,
                     jnp.where(lane == 3.0, w2, 0.0))))


def _out_ln_route(merged, x2d, w_out, ln_g, ln_b, rw_hi, rw_lo, rb, *, tm):
    t, d = merged.shape
    const = lambda a: pl.BlockSpec(a.shape, lambda i: (0,) * a.ndim)
    return pl.pallas_call(
        _out_ln_route_kernel,
        grid=(t // tm,),
        in_specs=[pl.BlockSpec((tm, d), lambda i: (i, 0)),
                  pl.BlockSpec((tm, d), lambda i: (i, 0)),
                  const(w_out), const(ln_g), const(ln_b), const(rw_hi), const(rw_lo), const(rb)],
        out_specs=[pl.BlockSpec((tm, d), lambda i: (i, 0)),
                   pl.BlockSpec((tm, ROUTER_LANES), lambda i: (i, 0))],
        out_shape=[jax.ShapeDtypeStruct((t, d), F32),
                   jax.ShapeDtypeStruct((t, ROUTER_LANES), F32)],
        compiler_params=_cparams(("parallel",)),
        name="out_ln_route",
    )(merged, x2d, w_out, ln_g, ln_b, rw_hi, rw_lo, rb)


def _moe_kernel(te_ref, nu_ref, gn_ref, dst_ref, x_hbm, wg_ref, wu_ref, wd_ref,
                y_hbm, xbuf, ybuf, wg_sc, wu_sc, wd_sc, gsem, ssem):
    s = pl.program_id(0)
    n_tiles = pl.num_programs(0) - 1
    n_used = nu_ref[0]
    tm = xbuf.shape[1]

    def gather_copy(tok, j, slot):
        return pltpu.make_async_copy(x_hbm.at[pl.ds(tok, 1)], xbuf.at[slot, pl.ds(j, 1)], gsem.at[slot])

    def scatter_copy(row, j, slot):
        return pltpu.make_async_copy(ybuf.at[slot, pl.ds(j, 1)], y_hbm.at[pl.ds(row, 1)], ssem.at[slot])

    def step(parity):
        cur, nxt = 1 - parity, parity
        have_cur = (s >= 1) & (s - 1 < n_used)

        @pl.when(have_cur)
        def _():
            for j in range(tm):
                gather_copy(0, j, cur).wait()

        @pl.when(s < n_used)
        def _():
            for j in range(tm):
                gather_copy(gn_ref[0, 0, j], j, nxt).start()

        @pl.when(have_cur)
        def _():
            xb = xbuf[cur].astype(BF16)
            gate = jnp.dot(xb, wg_sc[...], preferred_element_type=F32)
            up = jnp.dot(xb, wu_sc[...], preferred_element_type=F32)
            he = (gate * _sigmoid(gate) * up).astype(BF16)
            ybuf[cur] = jnp.dot(he, wd_sc[...], preferred_element_type=F32)

        @pl.when(s >= 1)
        def _():
            for j in range(tm):
                scatter_copy(dst_ref[0, 0, j], j, cur).start()

        @pl.when(s >= 2)
        def _():
            for j in range(tm):
                scatter_copy(0, j, nxt).wait()

        @pl.when(s == n_tiles)
        def _():
            for j in range(tm):
                scatter_copy(0, j, cur).wait()

    for parity in range(2):
        pl.when(lax.rem(s, 2) == parity)(functools.partial(step, parity))

    s_tile = jnp.minimum(s, n_tiles - 1)
    @pl.when((s < n_used) & ((s == 0) | (te_ref[s_tile] != te_ref[jnp.maximum(s_tile - 1, 0)])))
    def _():
        wg_sc[...] = wg_ref[0].astype(BF16)
        wu_sc[...] = wu_ref[0].astype(BF16)
        wd_sc[...] = wd_ref[0].astype(BF16)


def _moe_experts(x1, tile_expert, n_used, gidx, dest, wg, wu, wd):
    t, d = x1.shape
    nt = gidx.shape[0]
    f = wg.shape[2]
    idx_spec = lambda fn: pl.BlockSpec((1, 1, MOE_TILE), fn, memory_space=pltpu.SMEM)
    next_tile = lambda s: jnp.minimum(s, nt - 1)
    grid_spec = pltpu.PrefetchScalarGridSpec(
        num_scalar_prefetch=2,
        grid=(nt + 1,),
        in_specs=[idx_spec(lambda s, te, nu: (next_tile(s), 0, 0)),
                  idx_spec(lambda s, te, nu: (jnp.maximum(s - 1, 0), 0, 0)),
                  pl.BlockSpec(memory_space=pl.ANY),
                  pl.BlockSpec((1, d, f), lambda s, te, nu: (te[next_tile(s)], 0, 0)),
                  pl.BlockSpec((1, d, f), lambda s, te, nu: (te[next_tile(s)], 0, 0)),
                  pl.BlockSpec((1, f, d), lambda s, te, nu: (te[next_tile(s)], 0, 0))],
        out_specs=pl.BlockSpec(memory_space=pl.ANY),
        scratch_shapes=[pltpu.VMEM((2, MOE_TILE, d), F32),
                        pltpu.VMEM((2, MOE_TILE, d), F32),
                        pltpu.VMEM((d, f), BF16),
                        pltpu.VMEM((d, f), BF16),
                        pltpu.VMEM((f, d), BF16),
                        pltpu.SemaphoreType.DMA((2,)),
                        pltpu.SemaphoreType.DMA((2,))],
    )
    return pl.pallas_call(
        _moe_kernel,
        grid_spec=grid_spec,
        out_shape=jax.ShapeDtypeStruct((nt * MOE_TILE, d), F32),
        compiler_params=_cparams(("arbitrary",)),
        name="moe_experts",
    )(tile_expert, n_used, gidx, dest, x1, wg, wu, wd)


KEY_IDX_BITS = 16


def _moe_plan(route, n_tiles):
    t = route.shape[0]
    p = 2 * t
    assert p <= 1 << KEY_IDX_BITS and n_tiles * MOE_TILE == p + N_EXPERTS * MOE_TILE
    e_flat = jnp.concatenate([route[:, 0], route[:, 1]]).astype(jnp.int32)
    experts = jnp.arange(N_EXPERTS, dtype=jnp.int32)
    counts = jnp.sum((e_flat[:, None] == experts[None, :]).astype(jnp.int32), axis=0)
    n_pad = (-counts) % MOE_TILE
    k = jnp.arange(MOE_TILE, dtype=jnp.int32)
    pad_expert = jnp.where(k[None, :] < n_pad[:, None], experts[:, None], N_EXPERTS)
    pad_id = experts[:, None] * MOE_TILE + k[None, :]
    flag = 1 << KEY_IDX_BITS
    keys = jnp.concatenate([e_flat * (2 * flag) + jnp.arange(p, dtype=jnp.int32),
                            (pad_expert * (2 * flag) + flag + pad_id).reshape(-1)])
    keys = jnp.sort(keys)
    is_pad = (keys & flag) != 0
    idx = keys & (flag - 1)
    dest = jnp.where(is_pad, p + idx, idx)
    gidx = jnp.where(is_pad, 0, jnp.where(idx >= t, idx - t, idx))
    first_expert = keys.reshape(n_tiles, MOE_TILE)[:, 0] // (2 * flag)
    n_used = jnp.sum((first_expert < N_EXPERTS).astype(jnp.int32)).reshape(1)
    tile_expert = jnp.minimum(first_expert, N_EXPERTS - 1)
    shape3 = (n_tiles, 1, MOE_TILE)
    return tile_expert, n_used, gidx.reshape(shape3), dest.reshape(shape3)


def _final_kernel(x1_ref, y0_ref, y1_ref, route_ref, g_ref, b_ref, o_ref):
    y = route_ref[:, 2:3] * y0_ref[...] + route_ref[:, 3:4] * y1_ref[...]
    z = DEEPNORM_ALPHA * x1_ref[...] + y
    o_ref[...] = _layer_norm(z, g_ref[...], b_ref[...])


def _final_ln(x1, y_rows, route, ln_g, ln_b, *, tm):
    t, d = x1.shape
    nb = t // tm
    const = lambda a: pl.BlockSpec(a.shape, lambda i: (0,) * a.ndim)
    return pl.pallas_call(
        _final_kernel,
        grid=(nb,),
        in_specs=[pl.BlockSpec((tm, d), lambda i: (i, 0)),
                  pl.BlockSpec((tm, d), lambda i: (i, 0)),
                  pl.BlockSpec((tm, d), lambda i: (nb + i, 0)),
                  pl.BlockSpec((tm, ROUTER_LANES), lambda i: (i, 0)),
                  const(ln_g), const(ln_b)],
        out_specs=pl.BlockSpec((tm, d), lambda i: (i, 0)),
        out_shape=jax.ShapeDtypeStruct((t, d), F32),
        compiler_params=_cparams(("parallel",)),
        name="final_ln",
    )(x1, y_rows, y_rows, route, ln_g, ln_b)


def _router_weights(router_group_w, router_group_b, router_expert_w, router_expert_b):
    d = router_group_w.shape[0]
    pad = ROUTER_LANES - N_GROUPS - N_EXPERTS
    w = jnp.concatenate([router_group_w, router_expert_w, jnp.zeros((d, pad), F32)], axis=1)
    b = jnp.concatenate([router_group_b, router_expert_b, jnp.zeros((pad,), F32)]).reshape(1, ROUTER_LANES)
    w_hi = w.astype(BF16)
    w_lo = (w - w_hi.astype(F32)).astype(BF16)
    return w_hi, w_lo, b


def kernel(x, w_in, conv_w, conv_b, lru_wa, lru_ba, lru_wx, lru_bx, lru_lambda, w_o_attn, w_o_lru, w_out, ln1_g, ln1_b, router_group_w, router_group_b, router_expert_w, router_expert_b, w_e_gate, w_e_up, w_e_down, ln2_g, ln2_b):
    batch, seq, d = x.shape
    i_len = seq // CLASSES
    t = batch * seq
    attn_w = N_HEADS * HEAD_DIM
    for l in range(DEPTH):
        x2d = x.reshape(t, d)
        x_bf = x2d.astype(BF16)
        x_cls = _to_class_order(x.reshape(batch, i_len, CLASSES, d)).reshape(CLASSES, batch * i_len, d)
        row = lambda a: a.reshape(1, -1)

        qkv_hm = _qkv_proj(x_cls, w_in[l], n=3 * attn_w, tn=1024)
        rest = _rest_proj(x_bf, w_in[l], col0=3 * attn_w, tm=1024, tn=1024)
        attn = _dilated_attention(qkv_hm, batch).reshape(t, attn_w)
        hg = _rglru(rest.reshape(batch, seq, 4 * d), conv_w[l], row(conv_b[l]),
                    lru_wa[l].astype(BF16), row(lru_ba[l]), lru_wx[l].astype(BF16), row(lru_bx[l]),
                    row(lru_lambda[l]), d=d, ts=256, cw=512)
        merged = _merge(attn, hg.reshape(t, d), rest, w_o_attn[l].astype(BF16), w_o_lru[l].astype(BF16),
                        tm=1024, tn=512)
        rw_hi, rw_lo, rb = _router_weights(router_group_w[l], router_group_b[l],
                                           router_expert_w[l], router_expert_b[l])
        x1, route = _out_ln_route(merged, x2d, w_out[l].astype(BF16),
                                  row(ln1_g[l]), row(ln1_b[l]), rw_hi, rw_lo, rb, tm=512)

        n_tiles = 2 * t // MOE_TILE + N_EXPERTS
        tile_expert, n_used, gidx, dest = _moe_plan(route, n_tiles)
        y_rows = _moe_experts(x1, tile_expert, n_used, gidx, dest, w_e_gate[l], w_e_up[l], w_e_down[l])
        x = _final_ln(x1, y_rows, route, row(ln2_g[l]), row(ln2_b[l]), tm=512)
        x = x.reshape(batch, seq, d)
    return x
```

```python
import functools
import math

import numpy as np
import jax
import jax.numpy as jnp
from jax import lax
from jax.experimental import pallas as pl
from jax.experimental.pallas import tpu as pltpu

F32 = jnp.float32
BF16 = jnp.bfloat16

N_HEADS = 16
HEAD_DIM = 128
N_BACK = 128
CLASSES = 16
CONV_WIDTH = 4
LRU_BLOCK_DIM = 128
LRU_C = 8.0
N_GROUPS = 4
EXPERTS_PER_GROUP = 8
N_EXPERTS = N_GROUPS * EXPERTS_PER_GROUP
DEPTH = 1
DEEPNORM_ALPHA = (2.0 * DEPTH) ** 0.25
LN_EPS = 1e-5
ATTN_SCALE_LOG2 = HEAD_DIM ** -0.5 * math.log2(math.e)

LANES = 128
SUBLANES = 8
VMEM_LIMIT_BYTES = 56 * 1024 * 1024

MASK_BIAS = -1e30
MOE_TILE = 256
ROUTER_LANES = LANES


def _cparams(sem):
    return pltpu.CompilerParams(dimension_semantics=sem, vmem_limit_bytes=VMEM_LIMIT_BYTES)


def _class_order_kernel(x_hbm, o_ref, buf, sem):
    nb = pl.num_programs(1)
    n = pl.program_id(0) * nb + pl.program_id(1)
    slot = lax.rem(n, 2)

    def copy(step, buf_slot):
        return pltpu.make_async_copy(x_hbm.at[lax.rem(step, nb), :, step // nb, :], buf.at[buf_slot],
                                     sem.at[buf_slot])

    @pl.when(n == 0)
    def _():
        copy(n, slot).start()

    @pl.when(n + 1 < pl.num_programs(0) * nb)
    def _():
        copy(n + 1, 1 - slot).start()

    copy(n, slot).wait()
    o_ref[0, 0] = buf[slot].astype(o_ref.dtype)


def _to_class_order(x4):
    b, i, c, d = x4.shape
    return pl.pallas_call(
        _class_order_kernel,
        grid=(c, b),
        in_specs=[pl.BlockSpec(memory_space=pl.ANY)],
        out_specs=pl.BlockSpec((1, 1, i, d), lambda r, bb: (r, bb, 0, 0)),
        out_shape=jax.ShapeDtypeStruct((c, b, i, d), BF16),
        scratch_shapes=[pltpu.VMEM((2, i, d), F32), pltpu.SemaphoreType.DMA((2,))],
        compiler_params=_cparams(("arbitrary", "arbitrary")),
        name="class_order",
    )(x4)


def _qkv_kernel(x_ref, w_ref, o_ref, w_sc):
    @pl.when(pl.program_id(1) == 0)
    def _():
        w_sc[...] = w_ref[...].astype(BF16)

    res = jnp.dot(x_ref[0], w_sc[...], preferred_element_type=F32).astype(o_ref.dtype)
    for hh in range(o_ref.shape[0]):
        o_ref[hh, 0] = res[:, hh * HEAD_DIM:(hh + 1) * HEAD_DIM]


def _qkv_proj(x_cls, w_in, *, n, tn):
    _, rows, d = x_cls.shape
    heads_per_step = tn // HEAD_DIM
    return pl.pallas_call(
        _qkv_kernel,
        grid=(n // tn, CLASSES),
        in_specs=[pl.BlockSpec((1, rows, d), lambda j, r: (r, 0, 0)),
                  pl.BlockSpec((d, tn), lambda j, r: (0, j))],
        out_specs=pl.BlockSpec((heads_per_step, 1, rows, HEAD_DIM), lambda j, r: (j, r, 0, 0)),
        out_shape=jax.ShapeDtypeStruct((n // HEAD_DIM, CLASSES, rows, HEAD_DIM), F32),
        scratch_shapes=[pltpu.VMEM((d, tn), BF16)],
        compiler_params=_cparams(("parallel", "arbitrary")),
        name="qkv_proj",
    )(x_cls, w_in)


def _rest_kernel(x_ref, w_ref, o_ref, w_sc):
    @pl.when(pl.program_id(1) == 0)
    def _():
        w_sc[...] = w_ref[...].astype(BF16)

    o_ref[...] = jnp.dot(x_ref[...], w_sc[...], preferred_element_type=F32).astype(o_ref.dtype)


def _rest_proj(x2d, w_in, *, col0, tm, tn):
    t, d = x2d.shape
    n = w_in.shape[1] - col0
    j0 = col0 // tn
    return pl.pallas_call(
        _rest_kernel,
        grid=(n // tn, t // tm),
        in_specs=[pl.BlockSpec((tm, d), lambda j, i: (i, 0)),
                  pl.BlockSpec((d, tn), lambda j, i: (0, j0 + j))],
        out_specs=pl.BlockSpec((tm, tn), lambda j, i: (i, j)),
        out_shape=jax.ShapeDtypeStruct((t, n), BF16),
        scratch_shapes=[pltpu.VMEM((d, tn), BF16)],
        compiler_params=_cparams(("parallel", "arbitrary")),
        name="rest_proj",
    )(x2d, w_in)


def _band_bias(dist):
    return np.where((dist >= 0) & (dist <= N_BACK), 0.0, MASK_BIAS).astype(np.float32)


Q_BLOCK = 128


def _attn_biases():
    lq = np.arange(Q_BLOCK)[:, None]
    b16_first = _band_bias(lq - np.arange(Q_BLOCK)[None, :])
    b16_next = _band_bias(Q_BLOCK + lq - np.arange(2 * Q_BLOCK)[None, :])
    cq = np.repeat(np.arange(4), 32)[:, None]
    lq = np.tile(np.arange(32), 4)[:, None]
    ck = np.repeat(np.arange(4), 64)[None, :]
    lk = np.tile(np.arange(64), 4)[None, :]
    b4 = np.stack([_band_bias(4 * (lq - lk) + (cq - ck)),
                   _band_bias(4 * (32 + lq - lk) + (cq - ck))])
    rq = np.repeat(np.arange(16), 8)[:, None]
    lq = np.tile(np.arange(8), 16)[:, None]
    rk = np.repeat(np.arange(16), 16)[None, :]
    lk = np.tile(np.arange(16), 16)[None, :]
    b1 = np.stack([_band_bias(16 * (lq - lk) + (rq - rk)),
                   _band_bias(16 * (8 + lq - lk) + (rq - rk))])
    return b16_first, b16_next, b4, b1


def _softmax_blocks(blocks):
    scores = [lax.dot_general(q, k, (((1,), (1,)), ((), ())), preferred_element_type=F32) * ATTN_SCALE_LOG2
              + bias for q, k, _, bias, _ in blocks]
    mids = []
    for s, (_, _, _, _, state) in zip(scores, blocks):
        rows, n = s.shape
        chunks = [s[:, c * LANES:(c + 1) * LANES] for c in range(n // LANES)]
        mx = functools.reduce(jnp.maximum, chunks)
        m_blk = jnp.broadcast_to(jnp.max(mx, axis=1, keepdims=True), (rows, LANES))
        m_new = m_blk if state is None else jnp.maximum(state[0], m_blk)
        ps = [jnp.exp2(c - m_new) for c in chunks]
        l_blk = jnp.broadcast_to(jnp.sum(functools.reduce(jnp.add, ps), axis=1, keepdims=True), (rows, LANES))
        mids.append((m_new, l_blk, jnp.concatenate(ps, axis=1).astype(BF16)))
    out = []
    for (m_new, l_blk, p), (_, _, v, _, state) in zip(mids, blocks):
        pv = jnp.dot(p, v, preferred_element_type=F32)
        if state is None:
            out.append((m_new, l_blk, pv))
        else:
            alpha = jnp.exp2(state[0] - m_new)
            out.append((m_new, alpha * state[1] + l_blk, alpha * state[2] + pv))
    return out


BLOCKS_PER_ITER = 16


def _attn_kernel(q_ref, k_ref, v_ref, b16a_ref, b16b_ref, b4_ref, b1_ref, o_hbm,
                 qb, kb, vb, kb8, vb8, m_sc, l_sc, acc_sc, o_sc, o_sem):
    i_len = q_ref.shape[2]
    blocks_per_class = i_len // Q_BLOCK

    qb[...] = q_ref[0].astype(BF16)
    kb[...] = k_ref[0].astype(BF16)
    vb[...] = v_ref[0].astype(BF16)
    kb8[:, :i_len - SUBLANES, :] = k_ref[0, :, SUBLANES:, :].astype(BF16)
    vb8[:, :i_len - SUBLANES, :] = v_ref[0, :, SUBLANES:, :].astype(BF16)

    classes_per_iter = BLOCKS_PER_ITER // blocks_per_class

    def d16_body(it, carry):
        where, blocks = [], []
        for u in range(classes_per_iter):
            r = it * classes_per_iter + u
            for n in range(blocks_per_class):
                rows = slice(n * Q_BLOCK, (n + 1) * Q_BLOCK)
                keys = slice(0, Q_BLOCK) if n == 0 else slice((n - 1) * Q_BLOCK, (n + 1) * Q_BLOCK)
                bias = b16a_ref[...] if n == 0 else b16b_ref[...]
                where.append((r, rows))
                blocks.append((qb[r, rows, :], kb[r, keys, :], vb[r, keys, :], bias, None))
        for (r, rows), (m, l, acc) in zip(where, _softmax_blocks(blocks)):
            m_sc[r, rows, :] = m
            l_sc[r, rows, :] = l
            acc_sc[r, rows, :] = acc
        return carry

    lax.fori_loop(0, CLASSES // classes_per_iter, d16_body, 0)

    n_per_iter = BLOCKS_PER_ITER // 4

    def d4_body(it, carry):
        where, blocks = [], []
        for u in range(n_per_iter):
            n = it * n_per_iter + u
            q0 = pl.multiple_of(n * 32, 32)
            k0 = pl.multiple_of(jnp.maximum(n - 1, 0) * 32, 32)
            bias = b4_ref[jnp.minimum(n, 1)]
            for r4 in range(4):
                cls = [r4 + 4 * c for c in range(4)]
                cat = lambda ref, start, size, cls=cls: jnp.concatenate(
                    [ref[r, pl.ds(start, size), :] for r in cls], axis=0)
                state = (cat(m_sc, q0, 32), cat(l_sc, q0, 32), cat(acc_sc, q0, 32))
                where.append((cls, q0))
                blocks.append((cat(qb, q0, 32), cat(kb, k0, 64), cat(vb, k0, 64), bias, state))
        for (cls, q0), (m, l, acc) in zip(where, _softmax_blocks(blocks)):
            for c, r in enumerate(cls):
                m_sc[r, pl.ds(q0, 32), :] = m[32 * c:32 * (c + 1)]
                l_sc[r, pl.ds(q0, 32), :] = l[32 * c:32 * (c + 1)]
                acc_sc[r, pl.ds(q0, 32), :] = acc[32 * c:32 * (c + 1)]
        return carry

    lax.fori_loop(0, i_len // 32 // n_per_iter, d4_body, 0)

    def d1_body(it, carry):
        where, blocks = [], []
        for u in range(BLOCKS_PER_ITER):
            mb = it * BLOCKS_PER_ITER + u
            q0 = pl.multiple_of(mb * SUBLANES, SUBLANES)
            q = q_ref[0, :, pl.ds(q0, SUBLANES), :].reshape(Q_BLOCK, HEAD_DIM).astype(BF16)
            if u % 2 == 1:
                k0 = pl.multiple_of((mb - 1) * SUBLANES, 16)
                k, v = kb[:, pl.ds(k0, 16), :], vb[:, pl.ds(k0, 16), :]
            else:
                k0 = pl.multiple_of(jnp.maximum(mb - 2, 0) * SUBLANES, 16)
                k, v = kb8[:, pl.ds(k0, 16), :], vb8[:, pl.ds(k0, 16), :]
                if u == 0:
                    k = jnp.where(mb == 0, kb[:, 0:16, :], k)
                    v = jnp.where(mb == 0, vb[:, 0:16, :], v)
            flat = lambda ref, q0=q0: ref[:, pl.ds(q0, SUBLANES), :].reshape(Q_BLOCK, LANES)
            where.append(q0)
            blocks.append((q, k.reshape(2 * Q_BLOCK, HEAD_DIM), v.reshape(2 * Q_BLOCK, HEAD_DIM),
                           b1_ref[jnp.minimum(mb, 1)], (flat(m_sc), flat(l_sc), flat(acc_sc))))
        for q0, (m, l, acc) in zip(where, _softmax_blocks(blocks)):
            m_sc[:, pl.ds(q0, SUBLANES), :] = m.reshape(CLASSES, SUBLANES, LANES)
            l_sc[:, pl.ds(q0, SUBLANES), :] = l.reshape(CLASSES, SUBLANES, LANES)
            acc_sc[:, pl.ds(q0, SUBLANES), :] = acc.reshape(CLASSES, SUBLANES, LANES)
        return carry

    lax.fori_loop(0, i_len // SUBLANES // BLOCKS_PER_ITER, d1_body, 0)

    hh, bb = pl.program_id(0), pl.program_id(1)
    step = hh * pl.num_programs(1) + bb
    n_steps = pl.num_programs(0) * pl.num_programs(1)
    col = pl.multiple_of(hh * HEAD_DIM, HEAD_DIM)

    def out_copy(r, b_idx, col0):
        return pltpu.make_async_copy(o_sc.at[r], o_hbm.at[b_idx, :, r, pl.ds(col0, HEAD_DIM)], o_sem)

    @pl.when(step > 0)
    def _():
        for r in range(CLASSES):
            out_copy(r, 0, 0).wait()

    o_sc[...] = acc_sc[...] / l_sc[...]
    for r in range(CLASSES):
        out_copy(r, bb, col).start()

    @pl.when(step == n_steps - 1)
    def _():
        for r in range(CLASSES):
            out_copy(r, 0, 0).wait()


def _dilated_attention(qkv_hm, batch):
    n3, _, bi, _ = qkv_hm.shape
    h = n3 // 3
    i_len = bi // batch
    assert CLASSES * SUBLANES == Q_BLOCK and i_len % Q_BLOCK == 0 and BLOCKS_PER_ITER % (i_len // Q_BLOCK) == 0
    biases = [jnp.asarray(a) for a in _attn_biases()]
    blk = (1, CLASSES, i_len, HEAD_DIM)
    const = lambda a: pl.BlockSpec(a.shape, lambda hh, bb: (0,) * a.ndim)
    operand = lambda: pltpu.VMEM((CLASSES, i_len, HEAD_DIM), BF16)
    state = lambda: pltpu.VMEM((CLASSES, i_len, LANES), F32)
    return pl.pallas_call(
        _attn_kernel,
        grid=(h, batch),
        in_specs=[pl.BlockSpec(blk, lambda hh, bb: (hh, 0, bb, 0)),
                  pl.BlockSpec(blk, lambda hh, bb: (h + hh, 0, bb, 0)),
                  pl.BlockSpec(blk, lambda hh, bb: (2 * h + hh, 0, bb, 0))] + [const(a) for a in biases],
        out_specs=pl.BlockSpec(memory_space=pl.ANY),
        out_shape=jax.ShapeDtypeStruct((batch, i_len, CLASSES, h * HEAD_DIM), F32),
        scratch_shapes=[operand(), operand(), operand(), operand(), operand(), state(), state(), state(),
                        pltpu.VMEM((CLASSES, i_len, HEAD_DIM), F32), pltpu.SemaphoreType.DMA(())],
        compiler_params=_cparams(("arbitrary", "arbitrary")),
        name="dilated_attn",
    )(qkv_hm, qkv_hm, qkv_hm, *biases)


LOG2_E = math.log2(math.e)
SEG_LEN = 4
SCAN_ROWS = SUBLANES * SEG_LEN


def _softplus(z):
    return jnp.maximum(z, 0.0) + jnp.log1p(jnp.exp(-jnp.abs(z)))


def _sigmoid(z):
    return 1.0 / (1.0 + jnp.exp2(z * (-LOG2_E)))


def _lru_kernel(xr_ref, xg_ref, cw_ref, cb_ref, wa_ref, ba_ref, wx_ref, bx_ref, lam_ref,
                o_ref, h_sc, tail_sc, a_sc, u_sc, hs_sc):
    ts = xr_ref.shape[1]
    cw = xr_ref.shape[2]
    nv = ts // SUBLANES

    @pl.when(pl.program_id(2) == 0)
    def _():
        h_sc[...] = jnp.zeros_like(h_sc)
        tail_sc[...] = jnp.zeros_like(tail_sc)

    seg = lax.broadcasted_iota(jnp.int32, (SUBLANES, LANES), 0)
    for g in range(cw // LANES):
        sl = slice(g * LANES, (g + 1) * LANES)
        xr = xr_ref[0, :, sl].astype(F32)
        xfull = jnp.concatenate([tail_sc[:, sl], xr], axis=0)
        taps = cw_ref[:, sl]
        xc = xr * taps[3:4]
        for j in range(1, CONV_WIDTH):
            shifted = pltpu.roll(xfull, j, axis=0)[SUBLANES:]
            xc = xc + shifted * taps[3 - j:4 - j]
        xc = xc + cb_ref[:, sl]
        tail_sc[:, sl] = xr[ts - SUBLANES:]

        xcb = xc.astype(BF16)
        gate_a = jnp.dot(xcb, wa_ref[g], preferred_element_type=F32) + ba_ref[:, sl]
        gate_x = jnp.dot(xcb, wx_ref[g], preferred_element_type=F32) + bx_ref[:, sl]
        r = _sigmoid(gate_a)
        i = _sigmoid(gate_x)
        a = jnp.exp2(((-LRU_C * LOG2_E) * _softplus(-lam_ref[:, sl])) * r)
        y = 1.0 - a * a
        u = jnp.where(y > 0.0, y * lax.rsqrt(y), 0.0) * (i * xc)

        a_sc[g] = a
        u_sc[g] = u
        h = h_sc[:, sl]
        for run in range(ts // SCAN_ROWS):
            rows = lambda v: pl.ds(run * SCAN_ROWS + v, SUBLANES, stride=SEG_LEN)
            av = [a_sc[g, rows(v), :] for v in range(SEG_LEN)]
            uv = [u_sc[g, rows(v), :] for v in range(SEG_LEN)]
            local, decay = [uv[0]], [av[0]]
            for v in range(1, SEG_LEN):
                local.append(av[v] * local[-1] + uv[v])
                decay.append(av[v] * decay[-1])
            p, e = decay[-1], local[-1]
            for k in (1, 2, 4):
                e = p * jnp.where(seg >= k, pltpu.roll(e, k, axis=0), 0.0) + e
                p = p * jnp.where(seg >= k, pltpu.roll(p, k, axis=0), 1.0)
            p_in = jnp.where(seg >= 1, pltpu.roll(p, 1, axis=0), 1.0)
            e_in = jnp.where(seg >= 1, pltpu.roll(e, 1, axis=0), 0.0)
            start = e_in + p_in * h
            for v in range(SEG_LEN):
                hs_sc[g, rows(v), :] = local[v] + decay[v] * start
            h = (e + p * h)[SUBLANES - 1:SUBLANES]
        h_sc[:, sl] = h
        o_ref[0, :, sl] = (hs_sc[g] * jax.nn.gelu(xg_ref[0, :, sl].astype(F32))).astype(o_ref.dtype)


def _rglru(rest3, conv_w, conv_b, wa, ba, wx, bx, lam, *, d, ts, cw):
    b, s, _ = rest3.shape
    ncw = d // cw
    gpb = cw // LRU_BLOCK_DIM
    vec = lambda: pl.BlockSpec((1, cw), lambda bb, c, t: (0, c))
    return pl.pallas_call(
        _lru_kernel,
        grid=(b, ncw, s // ts),
        in_specs=[pl.BlockSpec((1, ts, cw), lambda bb, c, t: (bb, t, c)),
                  pl.BlockSpec((1, ts, cw), lambda bb, c, t: (bb, t, ncw + c)),
                  pl.BlockSpec((CONV_WIDTH, cw), lambda bb, c, t: (0, c)),
                  vec(),
                  pl.BlockSpec((gpb, LRU_BLOCK_DIM, LRU_BLOCK_DIM), lambda bb, c, t: (c, 0, 0)),
                  vec(),
                  pl.BlockSpec((gpb, LRU_BLOCK_DIM, LRU_BLOCK_DIM), lambda bb, c, t: (c, 0, 0)),
                  vec(), vec()],
        out_specs=pl.BlockSpec((1, ts, cw), lambda bb, c, t: (bb, t, c)),
        out_shape=jax.ShapeDtypeStruct((b, s, d), BF16),
        scratch_shapes=[pltpu.VMEM((1, cw), F32), pltpu.VMEM((SUBLANES, cw), F32)]
        + [pltpu.VMEM((gpb, ts, LANES), F32)] * 3,
        compiler_params=_cparams(("parallel", "parallel", "arbitrary")),
        name="rglru",
    )(rest3, rest3, conv_w, conv_b, wa, ba, wx, bx, lam)


def _merge_kernel(a_ref, h_ref, ga_ref, gl_ref, wa_ref, wl_ref, o_ref, a_sc):
    @pl.when(pl.program_id(1) == 0)
    def _():
        a_sc[...] = a_ref[...].astype(BF16)

    y_attn = jnp.dot(a_sc[...], wa_ref[...], preferred_element_type=F32)
    y_lru = jnp.dot(h_ref[...], wl_ref[...], preferred_element_type=F32)
    g_a = _sigmoid(ga_ref[...].astype(F32))
    g_l = _sigmoid(gl_ref[...].astype(F32))
    o_ref[...] = (g_a * y_attn + g_l * y_lru).astype(o_ref.dtype)


def _merge(attn, hg, rest, w_o_attn, w_o_lru, *, tm, tn):
    t, aw = attn.shape
    d = w_o_attn.shape[1]
    nj = d // tn
    return pl.pallas_call(
        _merge_kernel,
        grid=(t // tm, nj),
        in_specs=[pl.BlockSpec((tm, aw), lambda i, j: (i, 0)),
                  pl.BlockSpec((tm, d), lambda i, j: (i, 0)),
                  pl.BlockSpec((tm, tn), lambda i, j: (i, 2 * nj + j)),
                  pl.BlockSpec((tm, tn), lambda i, j: (i, 3 * nj + j)),
                  pl.BlockSpec((aw, tn), lambda i, j: (0, j)),
                  pl.BlockSpec((d, tn), lambda i, j: (0, j))],
        out_specs=pl.BlockSpec((tm, tn), lambda i, j: (i, j)),
        out_shape=jax.ShapeDtypeStruct((t, d), BF16),
        scratch_shapes=[pltpu.VMEM((tm, aw), BF16)],
        compiler_params=_cparams(("parallel", "arbitrary")),
        name="merge",
    )(attn, hg, rest, rest, w_o_attn, w_o_lru)


def _layer_norm(z, g, b):
    mu = jnp.mean(z, axis=-1, keepdims=True)
    zc = z - mu
    var = jnp.mean(zc * zc, axis=-1, keepdims=True)
    return zc * lax.rsqrt(var + LN_EPS) * g + b


def _out_ln_route_kernel(m_ref, x_ref, w_ref, g_ref, b_ref, rwh_ref, rwl_ref, rb_ref,
                         x1_ref, route_ref):
    y = jnp.dot(m_ref[...], w_ref[...], preferred_element_type=F32)
    z = DEEPNORM_ALPHA * x_ref[...] + y
    x1 = _layer_norm(z, g_ref[...], b_ref[...])
    x1_ref[...] = x1

    hi = x1.astype(BF16)
    lo = (x1 - hi.astype(F32)).astype(BF16)
    hi_both = jnp.dot(hi, jnp.concatenate([rwh_ref[...], rwl_ref[...]], axis=1), preferred_element_type=F32)
    lg = (hi_both[:, :ROUTER_LANES] + jnp.dot(lo, rwh_ref[...], preferred_element_type=F32)
          + hi_both[:, ROUTER_LANES:]) + rb_ref[...]

    lane = lax.broadcasted_iota(jnp.int32, lg.shape, 1).astype(F32)
    first = lambda hit: jnp.min(jnp.where(hit, lane, float(ROUTER_LANES)), axis=1, keepdims=True)
    in_grp = lane < N_GROUPS
    lgg = jnp.where(in_grp, lg, MASK_BIAS)
    mg = jnp.max(lgg, axis=1, keepdims=True)
    g_idx = first(lgg == mg)
    p_grp = 1.0 / jnp.sum(jnp.where(in_grp, jnp.exp(lg - mg), 0.0), axis=1, keepdims=True)
    lo_lane = N_GROUPS + EXPERTS_PER_GROUP * g_idx
    in_exp = (lane >= lo_lane) & (lane < lo_lane + EXPERTS_PER_GROUP)
    le = jnp.where(in_exp, lg, MASK_BIAS)
    v1 = jnp.max(le, axis=1, keepdims=True)
    i1 = first((le == v1) & in_exp)
    rest = in_exp & (lane != i1)
    le2 = jnp.where(rest, lg, MASK_BIAS)
    v2 = jnp.max(le2, axis=1, keepdims=True)
    i2 = first((le2 == v2) & rest)
    t = jnp.exp(v2 - v1)
    w1 = p_grp / (1.0 + t)
    w2 = p_grp * t / (1.0 + t)
    route_ref[...] = jnp.where(lane == 0.0, i1 - N_GROUPS,
                     jnp.where(lane == 1.0, i2 - N_GROUPS,
                     jnp.where(lane == 2.0, w1,
                     jnp.where(lane == 3.0, w2, 0.0))))


def _out_ln_route(merged, x2d, w_out, ln_g, ln_b, rw_hi, rw_lo, rb, *, tm):
    t, d = merged.shape
    const = lambda a: pl.BlockSpec(a.shape, lambda i: (0,) * a.ndim)
    return pl.pallas_call(
        _out_ln_route_kernel,
        grid=(t // tm,),
        in_specs=[pl.BlockSpec((tm, d), lambda i: (i, 0)),
                  pl.BlockSpec((tm, d), lambda i: (i, 0)),
                  const(w_out), const(ln_g), const(ln_b), const(rw_hi), const(rw_lo), const(rb)],
        out_specs=[pl.BlockSpec((tm, d), lambda i: (i, 0)),
                   pl.BlockSpec((tm, ROUTER_LANES), lambda i: (i, 0))],
        out_shape=[jax.ShapeDtypeStruct((t, d), F32),
                   jax.ShapeDtypeStruct((t, ROUTER_LANES), F32)],
        compiler_params=_cparams(("parallel",)),
        name="out_ln_route",
    )(merged, x2d, w_out, ln_g, ln_b, rw_hi, rw_lo, rb)


def _moe_kernel(te_ref, nu_ref, g0_ref, gn_ref, dst_ref, x_hbm, wg_ref, wu_ref, wd_ref,
                y_hbm, xbuf, ybuf, wg_sc, wu_sc, wd_sc, gsem, ssem):
    i = pl.program_id(0)
    last = pl.num_programs(0) - 1
    n_used = nu_ref[0]
    tm = xbuf.shape[1]

    def gather_copy(tok, j, s):
        return pltpu.make_async_copy(x_hbm.at[pl.ds(tok, 1)], xbuf.at[s, pl.ds(j, 1)], gsem.at[s])

    def scatter_copy(row, j, s):
        return pltpu.make_async_copy(ybuf.at[s, pl.ds(j, 1)], y_hbm.at[pl.ds(row, 1)], ssem.at[s])

    @pl.when(i == 0)
    def _():
        for j in range(tm):
            gather_copy(g0_ref[0, 0, j], j, 0).start()

    @pl.when((i < n_used) & ((i == 0) | (te_ref[i] != te_ref[jnp.maximum(i - 1, 0)])))
    def _():
        wg_sc[...] = wg_ref[0].astype(BF16)
        wu_sc[...] = wu_ref[0].astype(BF16)
        wd_sc[...] = wd_ref[0].astype(BF16)

    def step(slot):
        nxt = 1 - slot

        @pl.when(i < n_used)
        def _():
            for j in range(tm):
                gather_copy(0, j, slot).wait()

            @pl.when(i + 1 < n_used)
            def _():
                for j in range(tm):
                    gather_copy(gn_ref[0, 0, j], j, nxt).start()

            xb = xbuf[slot].astype(BF16)
            gate = jnp.dot(xb, wg_sc[...], preferred_element_type=F32)
            up = jnp.dot(xb, wu_sc[...], preferred_element_type=F32)
            he = (gate * _sigmoid(gate) * up).astype(BF16)
            ybuf[slot] = jnp.dot(he, wd_sc[...], preferred_element_type=F32)

        for j in range(tm):
            scatter_copy(dst_ref[0, 0, j], j, slot).start()

        @pl.when(i > 0)
        def _():
            for j in range(tm):
                scatter_copy(0, j, nxt).wait()

        @pl.when(i == last)
        def _():
            for j in range(tm):
                scatter_copy(0, j, slot).wait()

    for parity in range(2):
        pl.when(lax.rem(i, 2) == parity)(functools.partial(step, parity))


def _moe_experts(x1, tile_expert, n_used, gidx, dest, wg, wu, wd):
    t, d = x1.shape
    nt = gidx.shape[0]
    f = wg.shape[2]
    idx_spec = lambda fn: pl.BlockSpec((1, 1, MOE_TILE), fn, memory_space=pltpu.SMEM)
    grid_spec = pltpu.PrefetchScalarGridSpec(
        num_scalar_prefetch=2,
        grid=(nt,),
        in_specs=[idx_spec(lambda i, te, nu: (0, 0, 0)),
                  idx_spec(lambda i, te, nu: (jnp.minimum(i + 1, nt - 1), 0, 0)),
                  idx_spec(lambda i, te, nu: (i, 0, 0)),
                  pl.BlockSpec(memory_space=pl.ANY),
                  pl.BlockSpec((1, d, f), lambda i, te, nu: (te[i], 0, 0)),
                  pl.BlockSpec((1, d, f), lambda i, te, nu: (te[i], 0, 0)),
                  pl.BlockSpec((1, f, d), lambda i, te, nu: (te[i], 0, 0))],
        out_specs=pl.BlockSpec(memory_space=pl.ANY),
        scratch_shapes=[pltpu.VMEM((2, MOE_TILE, d), F32),
                        pltpu.VMEM((2, MOE_TILE, d), F32),
                        pltpu.VMEM((d, f), BF16),
                        pltpu.VMEM((d, f), BF16),
                        pltpu.VMEM((f, d), BF16),
                        pltpu.SemaphoreType.DMA((2,)),
                        pltpu.SemaphoreType.DMA((2,))],
    )
    return pl.pallas_call(
        _moe_kernel,
        grid_spec=grid_spec,
        out_shape=jax.ShapeDtypeStruct((nt * MOE_TILE, d), F32),
        compiler_params=_cparams(("arbitrary",)),
        name="moe_experts",
    )(tile_expert, n_used, gidx, gidx, dest, x1, wg, wu, wd)


KEY_IDX_BITS = 16


def _moe_plan(route, n_tiles):
    t = route.shape[0]
    p = 2 * t
    assert p <= 1 << KEY_IDX_BITS and n_tiles * MOE_TILE == p + N_EXPERTS * MOE_TILE
    e_flat = jnp.concatenate([route[:, 0], route[:, 1]]).astype(jnp.int32)
    experts = jnp.arange(N_EXPERTS, dtype=jnp.int32)
    counts = jnp.sum((e_flat[:, None] == experts[None, :]).astype(jnp.int32), axis=0)
    n_pad = (-counts) % MOE_TILE
    k = jnp.arange(MOE_TILE, dtype=jnp.int32)
    pad_expert = jnp.where(k[None, :] < n_pad[:, None], experts[:, None], N_EXPERTS)
    pad_id = experts[:, None] * MOE_TILE + k[None, :]
    flag = 1 << KEY_IDX_BITS
    keys = jnp.concatenate([e_flat * (2 * flag) + jnp.arange(p, dtype=jnp.int32),
                            (pad_expert * (2 * flag) + flag + pad_id).reshape(-1)])
    keys = jnp.sort(keys)
    is_pad = (keys & flag) != 0
    idx = keys & (flag - 1)
    dest = jnp.where(is_pad, p + idx, idx)
    gidx = jnp.where(is_pad, 0, jnp.where(idx >= t, idx - t, idx))
    first_expert = keys.reshape(n_tiles, MOE_TILE)[:, 0] // (2 * flag)
    n_used = jnp.sum((first_expert < N_EXPERTS).astype(jnp.int32)).reshape(1)
    tile_expert = jnp.minimum(first_expert, N_EXPERTS - 1)
    shape3 = (n_tiles, 1, MOE_TILE)
    return tile_expert, n_used, gidx.reshape(shape3), dest.reshape(shape3)


def _final_kernel(x1_ref, y0_ref, y1_ref, route_ref, g_ref, b_ref, o_ref):
    y = route_ref[:, 2:3] * y0_ref[...] + route_ref[:, 3:4] * y1_ref[...]
    z = DEEPNORM_ALPHA * x1_ref[...] + y
    o_ref[...] = _layer_norm(z, g_ref[...], b_ref[...])


def _final_ln(x1, y_rows, route, ln_g, ln_b, *, tm):
    t, d = x1.shape
    nb = t // tm
    const = lambda a: pl.BlockSpec(a.shape, lambda i: (0,) * a.ndim)
    return pl.pallas_call(
        _final_kernel,
        grid=(nb,),
        in_specs=[pl.BlockSpec((tm, d), lambda i: (i, 0)),
                  pl.BlockSpec((tm, d), lambda i: (i, 0)),
                  pl.BlockSpec((tm, d), lambda i: (nb + i, 0)),
                  pl.BlockSpec((tm, ROUTER_LANES), lambda i: (i, 0)),
                  const(ln_g), const(ln_b)],
        out_specs=pl.BlockSpec((tm, d), lambda i: (i, 0)),
        out_shape=jax.ShapeDtypeStruct((t, d), F32),
        compiler_params=_cparams(("parallel",)),
        name="final_ln",
    )(x1, y_rows, y_rows, route, ln_g, ln_b)


def _router_weights(router_group_w, router_group_b, router_expert_w, router_expert_b):
    d = router_group_w.shape[0]
    pad = ROUTER_LANES - N_GROUPS - N_EXPERTS
    w = jnp.concatenate([router_group_w, router_expert_w, jnp.zeros((d, pad), F32)], axis=1)
    b = jnp.concatenate([router_group_b, router_expert_b, jnp.zeros((pad,), F32)]).reshape(1, ROUTER_LANES)
    w_hi = w.astype(BF16)
    w_lo = (w - w_hi.astype(F32)).astype(BF16)
    return w_hi, w_lo, b


def kernel(x, w_in, conv_w, conv_b, lru_wa, lru_ba, lru_wx, lru_bx, lru_lambda, w_o_attn, w_o_lru, w_out, ln1_g, ln1_b, router_group_w, router_group_b, router_expert_w, router_expert_b, w_e_gate, w_e_up, w_e_down, ln2_g, ln2_b):
    batch, seq, d = x.shape
    i_len = seq // CLASSES
    t = batch * seq
    attn_w = N_HEADS * HEAD_DIM
    for l in range(DEPTH):
        x2d = x.reshape(t, d)
        x_bf = x2d.astype(BF16)
        x_cls = _to_class_order(x.reshape(batch, i_len, CLASSES, d)).reshape(CLASSES, batch * i_len, d)
        row = lambda a: a.reshape(1, -1)

        qkv_hm = _qkv_proj(x_cls, w_in[l], n=3 * attn_w, tn=1024)
        rest = _rest_proj(x_bf, w_in[l], col0=3 * attn_w, tm=1024, tn=1024)
        attn = _dilated_attention(qkv_hm, batch).reshape(t, attn_w)
        hg = _rglru(rest.reshape(batch, seq, 4 * d), conv_w[l], row(conv_b[l]),
                    lru_wa[l].astype(BF16), row(lru_ba[l]), lru_wx[l].astype(BF16), row(lru_bx[l]),
                    row(lru_lambda[l]), d=d, ts=256, cw=512)
        merged = _merge(attn, hg.reshape(t, d), rest, w_o_attn[l].astype(BF16), w_o_lru[l].astype(BF16),
                        tm=1024, tn=512)
        rw_hi, rw_lo, rb = _router_weights(router_group_w[l], router_group_b[l],
                                           router_expert_w[l], router_expert_b[l])
        x1, route = _out_ln_route(merged, x2d, w_out[l].astype(BF16),
                                  row(ln1_g[l]), row(ln1_b[l]), rw_hi, rw_lo, rb, tm=512)

        n_tiles = 2 * t // MOE_TILE + N_EXPERTS
        tile_expert, n_used, gidx, dest = _moe_plan(route, n_tiles)
        y_rows = _moe_experts(x1, tile_expert, n_used, gidx, dest, w_e_gate[l], w_e_up[l], w_e_down[l])
        x = _final_ln(x1, y_rows, route, row(ln2_g[l]), row(ln2_b[l]), tm=512)
        x = x.reshape(batch, seq, d)
    return x
```

```python
import functools
import math

import numpy as np
import jax
import jax.numpy as jnp
from jax import lax
from jax.experimental import pallas as pl
from jax.experimental.pallas import tpu as pltpu

F32 = jnp.float32
BF16 = jnp.bfloat16

N_HEADS = 16
HEAD_DIM = 128
N_BACK = 128
CLASSES = 16
CONV_WIDTH = 4
LRU_BLOCK_DIM = 128
LRU_C = 8.0
N_GROUPS = 4
EXPERTS_PER_GROUP = 8
N_EXPERTS = N_GROUPS * EXPERTS_PER_GROUP
DEPTH = 1
DEEPNORM_ALPHA = (2.0 * DEPTH) ** 0.25
LN_EPS = 1e-5
ATTN_SCALE_LOG2 = HEAD_DIM ** -0.5 * math.log2(math.e)

LANES = 128
SUBLANES = 8
VMEM_LIMIT_BYTES = 56 * 1024 * 1024

MASK_BIAS = -1e30
MOE_TILE = 256
ROUTER_LANES = LANES


def _cparams(sem):
    return pltpu.CompilerParams(dimension_semantics=sem, vmem_limit_bytes=VMEM_LIMIT_BYTES)


def _class_order_kernel(x_hbm, o_ref, buf, sem):
    nb = pl.num_programs(1)
    n = pl.program_id(0) * nb + pl.program_id(1)
    slot = lax.rem(n, 2)

    def copy(step, buf_slot):
        return pltpu.make_async_copy(x_hbm.at[lax.rem(step, nb), :, step // nb, :], buf.at[buf_slot],
                                     sem.at[buf_slot])

    @pl.when(n == 0)
    def _():
        copy(n, slot).start()

    @pl.when(n + 1 < pl.num_programs(0) * nb)
    def _():
        copy(n + 1, 1 - slot).start()

    copy(n, slot).wait()
    o_ref[0, 0] = buf[slot].astype(o_ref.dtype)


def _to_class_order(x4):
    b, i, c, d = x4.shape
    return pl.pallas_call(
        _class_order_kernel,
        grid=(c, b),
        in_specs=[pl.BlockSpec(memory_space=pl.ANY)],
        out_specs=pl.BlockSpec((1, 1, i, d), lambda r, bb: (r, bb, 0, 0)),
        out_shape=jax.ShapeDtypeStruct((c, b, i, d), BF16),
        scratch_shapes=[pltpu.VMEM((2, i, d), F32), pltpu.SemaphoreType.DMA((2,))],
        compiler_params=_cparams(("arbitrary", "arbitrary")),
        name="class_order",
    )(x4)


def _qkv_kernel(x_ref, w_ref, o_ref, w_sc):
    @pl.when(pl.program_id(1) == 0)
    def _():
        w_sc[...] = w_ref[...].astype(BF16)

    n_cls, rows, d = x_ref.shape
    res = jnp.dot(x_ref[...].reshape(n_cls * rows, d), w_sc[...], preferred_element_type=F32).astype(o_ref.dtype)
    for hh in range(o_ref.shape[0]):
        for c in range(n_cls):
            o_ref[hh, c] = res[c * rows:(c + 1) * rows, hh * HEAD_DIM:(hh + 1) * HEAD_DIM]


def _qkv_proj(x_cls, w_in, *, n, tn, cls_per_step):
    _, rows, d = x_cls.shape
    heads_per_step = tn // HEAD_DIM
    return pl.pallas_call(
        _qkv_kernel,
        grid=(n // tn, CLASSES // cls_per_step),
        in_specs=[pl.BlockSpec((cls_per_step, rows, d), lambda j, r: (r, 0, 0)),
                  pl.BlockSpec((d, tn), lambda j, r: (0, j))],
        out_specs=pl.BlockSpec((heads_per_step, cls_per_step, rows, HEAD_DIM), lambda j, r: (j, r, 0, 0)),
        out_shape=jax.ShapeDtypeStruct((n // HEAD_DIM, CLASSES, rows, HEAD_DIM), F32),
        scratch_shapes=[pltpu.VMEM((d, tn), BF16)],
        compiler_params=_cparams(("parallel", "arbitrary")),
        name="qkv_proj",
    )(x_cls, w_in)


def _rest_kernel(x_ref, w_ref, o_ref, w_sc):
    @pl.when(pl.program_id(1) == 0)
    def _():
        w_sc[...] = w_ref[...].astype(BF16)

    o_ref[...] = jnp.dot(x_ref[...], w_sc[...], preferred_element_type=F32).astype(o_ref.dtype)


def _rest_proj(x2d, w_in, *, col0, tm, tn):
    t, d = x2d.shape
    n = w_in.shape[1] - col0
    j0 = col0 // tn
    return pl.pallas_call(
        _rest_kernel,
        grid=(n // tn, t // tm),
        in_specs=[pl.BlockSpec((tm, d), lambda j, i: (i, 0)),
                  pl.BlockSpec((d, tn), lambda j, i: (0, j0 + j))],
        out_specs=pl.BlockSpec((tm, tn), lambda j, i: (i, j)),
        out_shape=jax.ShapeDtypeStruct((t, n), BF16),
        scratch_shapes=[pltpu.VMEM((d, tn), BF16)],
        compiler_params=_cparams(("parallel", "arbitrary")),
        name="rest_proj",
    )(x2d, w_in)


def _band_bias(dist):
    return np.where((dist >= 0) & (dist <= N_BACK), 0.0, MASK_BIAS).astype(np.float32)


Q_BLOCK = 128


def _attn_biases():
    lq = np.arange(Q_BLOCK)[:, None]
    b16_first = _band_bias(lq - np.arange(Q_BLOCK)[None, :])
    b16_next = _band_bias(Q_BLOCK + lq - np.arange(2 * Q_BLOCK)[None, :])
    cq = np.repeat(np.arange(4), 32)[:, None]
    lq = np.tile(np.arange(32), 4)[:, None]
    ck = np.repeat(np.arange(4), 64)[None, :]
    lk = np.tile(np.arange(64), 4)[None, :]
    b4 = np.stack([_band_bias(4 * (lq - lk) + (cq - ck)),
                   _band_bias(4 * (32 + lq - lk) + (cq - ck))])
    rq = np.repeat(np.arange(16), 8)[:, None]
    lq = np.tile(np.arange(8), 16)[:, None]
    rk = np.repeat(np.arange(16), 16)[None, :]
    lk = np.tile(np.arange(16), 16)[None, :]
    b1 = np.stack([_band_bias(16 * (lq - lk) + (rq - rk)),
                   _band_bias(16 * (8 + lq - lk) + (rq - rk))])
    return b16_first, b16_next, b4, b1


def _softmax_blocks(blocks):
    scores = [lax.dot_general(q, k, (((1,), (1,)), ((), ())), preferred_element_type=F32) * ATTN_SCALE_LOG2
              + bias for q, k, _, bias, _ in blocks]
    mids = []
    for s, (_, _, _, _, state) in zip(scores, blocks):
        rows, n = s.shape
        chunks = [s[:, c * LANES:(c + 1) * LANES] for c in range(n // LANES)]
        mx = functools.reduce(jnp.maximum, chunks)
        m_blk = jnp.broadcast_to(jnp.max(mx, axis=1, keepdims=True), (rows, LANES))
        m_new = m_blk if state is None else jnp.maximum(state[0], m_blk)
        ps = [jnp.exp2(c - m_new) for c in chunks]
        l_blk = jnp.broadcast_to(jnp.sum(functools.reduce(jnp.add, ps), axis=1, keepdims=True), (rows, LANES))
        mids.append((m_new, l_blk, jnp.concatenate(ps, axis=1).astype(BF16)))
    out = []
    for (m_new, l_blk, p), (_, _, v, _, state) in zip(mids, blocks):
        pv = jnp.dot(p, v, preferred_element_type=F32)
        if state is None:
            out.append((m_new, l_blk, pv))
        else:
            alpha = jnp.exp2(state[0] - m_new)
            out.append((m_new, alpha * state[1] + l_blk, alpha * state[2] + pv))
    return out


BLOCKS_PER_ITER = 16


def _attn_kernel(q_ref, k_ref, v_ref, b16a_ref, b16b_ref, b4_ref, b1_ref, o_hbm,
                 qb, kb, vb, kb8, vb8, m_sc, l_sc, acc_sc, o_sc, o_sem):
    i_len = q_ref.shape[2]
    blocks_per_class = i_len // Q_BLOCK

    qb[...] = q_ref[0].astype(BF16)
    kb[...] = k_ref[0].astype(BF16)
    vb[...] = v_ref[0].astype(BF16)
    kb8[:, :i_len - SUBLANES, :] = k_ref[0, :, SUBLANES:, :].astype(BF16)
    vb8[:, :i_len - SUBLANES, :] = v_ref[0, :, SUBLANES:, :].astype(BF16)

    classes_per_iter = BLOCKS_PER_ITER // blocks_per_class

    def d16_body(it, carry):
        where, blocks = [], []
        for u in range(classes_per_iter):
            r = it * classes_per_iter + u
            for n in range(blocks_per_class):
                rows = slice(n * Q_BLOCK, (n + 1) * Q_BLOCK)
                keys = slice(0, Q_BLOCK) if n == 0 else slice((n - 1) * Q_BLOCK, (n + 1) * Q_BLOCK)
                bias = b16a_ref[...] if n == 0 else b16b_ref[...]
                where.append((r, rows))
                blocks.append((qb[r, rows, :], kb[r, keys, :], vb[r, keys, :], bias, None))
        for (r, rows), (m, l, acc) in zip(where, _softmax_blocks(blocks)):
            m_sc[r, rows, :] = m
            l_sc[r, rows, :] = l
            acc_sc[r, rows, :] = acc
        return carry

    lax.fori_loop(0, CLASSES // classes_per_iter, d16_body, 0)

    n_per_iter = BLOCKS_PER_ITER // 4

    def d4_body(it, carry):
        where, blocks = [], []
        for u in range(n_per_iter):
            n = it * n_per_iter + u
            q0 = pl.multiple_of(n * 32, 32)
            k0 = pl.multiple_of(jnp.maximum(n - 1, 0) * 32, 32)
            bias = b4_ref[jnp.minimum(n, 1)]
            for r4 in range(4):
                cls = [r4 + 4 * c for c in range(4)]
                cat = lambda ref, start, size, cls=cls: jnp.concatenate(
                    [ref[r, pl.ds(start, size), :] for r in cls], axis=0)
                state = (cat(m_sc, q0, 32), cat(l_sc, q0, 32), cat(acc_sc, q0, 32))
                where.append((cls, q0))
                blocks.append((cat(qb, q0, 32), cat(kb, k0, 64), cat(vb, k0, 64), bias, state))
        for (cls, q0), (m, l, acc) in zip(where, _softmax_blocks(blocks)):
            for c, r in enumerate(cls):
                m_sc[r, pl.ds(q0, 32), :] = m[32 * c:32 * (c + 1)]
                l_sc[r, pl.ds(q0, 32), :] = l[32 * c:32 * (c + 1)]
                acc_sc[r, pl.ds(q0, 32), :] = acc[32 * c:32 * (c + 1)]
        return carry

    lax.fori_loop(0, i_len // 32 // n_per_iter, d4_body, 0)

    def d1_body(it, carry):
        where, blocks = [], []
        for u in range(BLOCKS_PER_ITER):
            mb = it * BLOCKS_PER_ITER + u
            q0 = pl.multiple_of(mb * SUBLANES, SUBLANES)
            q = q_ref[0, :, pl.ds(q0, SUBLANES), :].reshape(Q_BLOCK, HEAD_DIM).astype(BF16)
            if u % 2 == 1:
                k0 = pl.multiple_of((mb - 1) * SUBLANES, 16)
                k, v = kb[:, pl.ds(k0, 16), :], vb[:, pl.ds(k0, 16), :]
            else:
                k0 = pl.multiple_of(jnp.maximum(mb - 2, 0) * SUBLANES, 16)
                k, v = kb8[:, pl.ds(k0, 16), :], vb8[:, pl.ds(k0, 16), :]
                if u == 0:
                    k = jnp.where(mb == 0, kb[:, 0:16, :], k)
                    v = jnp.where(mb == 0, vb[:, 0:16, :], v)
            flat = lambda ref, q0=q0: ref[:, pl.ds(q0, SUBLANES), :].reshape(Q_BLOCK, LANES)
            where.append(q0)
            blocks.append((q, k.reshape(2 * Q_BLOCK, HEAD_DIM), v.reshape(2 * Q_BLOCK, HEAD_DIM),
                           b1_ref[jnp.minimum(mb, 1)], (flat(m_sc), flat(l_sc), flat(acc_sc))))
        for q0, (m, l, acc) in zip(where, _softmax_blocks(blocks)):
            m_sc[:, pl.ds(q0, SUBLANES), :] = m.reshape(CLASSES, SUBLANES, LANES)
            l_sc[:, pl.ds(q0, SUBLANES), :] = l.reshape(CLASSES, SUBLANES, LANES)
            acc_sc[:, pl.ds(q0, SUBLANES), :] = acc.reshape(CLASSES, SUBLANES, LANES)
        return carry

    lax.fori_loop(0, i_len // SUBLANES // BLOCKS_PER_ITER, d1_body, 0)

    hh, bb = pl.program_id(0), pl.program_id(1)
    step = hh * pl.num_programs(1) + bb
    n_steps = pl.num_programs(0) * pl.num_programs(1)
    col = pl.multiple_of(hh * HEAD_DIM, HEAD_DIM)

    def out_copy(r, b_idx, col0):
        return pltpu.make_async_copy(o_sc.at[r], o_hbm.at[b_idx, :, r, pl.ds(col0, HEAD_DIM)], o_sem)

    @pl.when(step > 0)
    def _():
        for r in range(CLASSES):
            out_copy(r, 0, 0).wait()

    o_sc[...] = acc_sc[...] / l_sc[...]
    for r in range(CLASSES):
        out_copy(r, bb, col).start()

    @pl.when(step == n_steps - 1)
    def _():
        for r in range(CLASSES):
            out_copy(r, 0, 0).wait()


def _dilated_attention(qkv_hm, batch):
    n3, _, bi, _ = qkv_hm.shape
    h = n3 // 3
    i_len = bi // batch
    assert CLASSES * SUBLANES == Q_BLOCK and i_len % Q_BLOCK == 0 and BLOCKS_PER_ITER % (i_len // Q_BLOCK) == 0
    biases = [jnp.asarray(a) for a in _attn_biases()]
    blk = (1, CLASSES, i_len, HEAD_DIM)
    const = lambda a: pl.BlockSpec(a.shape, lambda hh, bb: (0,) * a.ndim)
    operand = lambda: pltpu.VMEM((CLASSES, i_len, HEAD_DIM), BF16)
    state = lambda: pltpu.VMEM((CLASSES, i_len, LANES), F32)
    return pl.pallas_call(
        _attn_kernel,
        grid=(h, batch),
        in_specs=[pl.BlockSpec(blk, lambda hh, bb: (hh, 0, bb, 0)),
                  pl.BlockSpec(blk, lambda hh, bb: (h + hh, 0, bb, 0)),
                  pl.BlockSpec(blk, lambda hh, bb: (2 * h + hh, 0, bb, 0))] + [const(a) for a in biases],
        out_specs=pl.BlockSpec(memory_space=pl.ANY),
        out_shape=jax.ShapeDtypeStruct((batch, i_len, CLASSES, h * HEAD_DIM), F32),
        scratch_shapes=[operand(), operand(), operand(), operand(), operand(), state(), state(), state(),
                        pltpu.VMEM((CLASSES, i_len, HEAD_DIM), F32), pltpu.SemaphoreType.DMA(())],
        compiler_params=_cparams(("arbitrary", "arbitrary")),
        name="dilated_attn",
    )(qkv_hm, qkv_hm, qkv_hm, *biases)


LOG2_E = math.log2(math.e)
SEG_LEN = 4
SCAN_ROWS = SUBLANES * SEG_LEN


def _softplus(z):
    return jnp.maximum(z, 0.0) + jnp.log1p(jnp.exp(-jnp.abs(z)))


def _sigmoid(z):
    return 1.0 / (1.0 + jnp.exp2(z * (-LOG2_E)))


def _lru_kernel(xr_ref, xg_ref, cw_ref, cb_ref, wa_ref, ba_ref, wx_ref, bx_ref, lam_ref,
                o_ref, h_sc, tail_sc, a_sc, u_sc, hs_sc):
    ts = xr_ref.shape[1]
    cw = xr_ref.shape[2]
    nv = ts // SUBLANES

    @pl.when(pl.program_id(2) == 0)
    def _():
        h_sc[...] = jnp.zeros_like(h_sc)
        tail_sc[...] = jnp.zeros_like(tail_sc)

    seg = lax.broadcasted_iota(jnp.int32, (SUBLANES, LANES), 0)
    for g in range(cw // LANES):
        sl = slice(g * LANES, (g + 1) * LANES)
        xr = xr_ref[0, :, sl].astype(F32)
        xfull = jnp.concatenate([tail_sc[:, sl], xr], axis=0)
        taps = cw_ref[:, sl]
        xc = xr * taps[3:4]
        for j in range(1, CONV_WIDTH):
            shifted = pltpu.roll(xfull, j, axis=0)[SUBLANES:]
            xc = xc + shifted * taps[3 - j:4 - j]
        xc = xc + cb_ref[:, sl]
        tail_sc[:, sl] = xr[ts - SUBLANES:]

        xcb = xc.astype(BF16)
        gate_a = jnp.dot(xcb, wa_ref[g], preferred_element_type=F32) + ba_ref[:, sl]
        gate_x = jnp.dot(xcb, wx_ref[g], preferred_element_type=F32) + bx_ref[:, sl]
        r = _sigmoid(gate_a)
        i = _sigmoid(gate_x)
        a = jnp.exp2(((-LRU_C * LOG2_E) * _softplus(-lam_ref[:, sl])) * r)
        y = 1.0 - a * a
        u = jnp.where(y > 0.0, y * lax.rsqrt(y), 0.0) * (i * xc)

        a_sc[g] = a
        u_sc[g] = u
        h = h_sc[:, sl]
        for run in range(ts // SCAN_ROWS):
            rows = lambda v: pl.ds(run * SCAN_ROWS + v, SUBLANES, stride=SEG_LEN)
            av = [a_sc[g, rows(v), :] for v in range(SEG_LEN)]
            uv = [u_sc[g, rows(v), :] for v in range(SEG_LEN)]
            local, decay = [uv[0]], [av[0]]
            for v in range(1, SEG_LEN):
                local.append(av[v] * local[-1] + uv[v])
                decay.append(av[v] * decay[-1])
            p, e = decay[-1], local[-1]
            for k in (1, 2, 4):
                e = p * jnp.where(seg >= k, pltpu.roll(e, k, axis=0), 0.0) + e
                p = p * jnp.where(seg >= k, pltpu.roll(p, k, axis=0), 1.0)
            p_in = jnp.where(seg >= 1, pltpu.roll(p, 1, axis=0), 1.0)
            e_in = jnp.where(seg >= 1, pltpu.roll(e, 1, axis=0), 0.0)
            start = e_in + p_in * h
            for v in range(SEG_LEN):
                hs_sc[g, rows(v), :] = local[v] + decay[v] * start
            h = (e + p * h)[SUBLANES - 1:SUBLANES]
        h_sc[:, sl] = h
        o_ref[0, :, sl] = (hs_sc[g] * jax.nn.gelu(xg_ref[0, :, sl].astype(F32))).astype(o_ref.dtype)


def _rglru(rest3, conv_w, conv_b, wa, ba, wx, bx, lam, *, d, ts, cw):
    b, s, _ = rest3.shape
    ncw = d // cw
    gpb = cw // LRU_BLOCK_DIM
    vec = lambda: pl.BlockSpec((1, cw), lambda bb, c, t: (0, c))
    return pl.pallas_call(
        _lru_kernel,
        grid=(b, ncw, s // ts),
        in_specs=[pl.BlockSpec((1, ts, cw), lambda bb, c, t: (bb, t, c)),
                  pl.BlockSpec((1, ts, cw), lambda bb, c, t: (bb, t, ncw + c)),
                  pl.BlockSpec((CONV_WIDTH, cw), lambda bb, c, t: (0, c)),
                  vec(),
                  pl.BlockSpec((gpb, LRU_BLOCK_DIM, LRU_BLOCK_DIM), lambda bb, c, t: (c, 0, 0)),
                  vec(),
                  pl.BlockSpec((gpb, LRU_BLOCK_DIM, LRU_BLOCK_DIM), lambda bb, c, t: (c, 0, 0)),
                  vec(), vec()],
        out_specs=pl.BlockSpec((1, ts, cw), lambda bb, c, t: (bb, t, c)),
        out_shape=jax.ShapeDtypeStruct((b, s, d), BF16),
        scratch_shapes=[pltpu.VMEM((1, cw), F32), pltpu.VMEM((SUBLANES, cw), F32)]
        + [pltpu.VMEM((gpb, ts, LANES), F32)] * 3,
        compiler_params=_cparams(("parallel", "parallel", "arbitrary")),
        name="rglru",
    )(rest3, rest3, conv_w, conv_b, wa, ba, wx, bx, lam)


def _merge_kernel(a_ref, h_ref, ga_ref, gl_ref, wa_ref, wl_ref, o_ref, a_sc):
    @pl.when(pl.program_id(1) == 0)
    def _():
        a_sc[...] = a_ref[...].astype(BF16)

    y_attn = jnp.dot(a_sc[...], wa_ref[...], preferred_element_type=F32)
    y_lru = jnp.dot(h_ref[...], wl_ref[...], preferred_element_type=F32)
    g_a = _sigmoid(ga_ref[...].astype(F32))
    g_l = _sigmoid(gl_ref[...].astype(F32))
    o_ref[...] = (g_a * y_attn + g_l * y_lru).astype(o_ref.dtype)


def _merge(attn, hg, rest, w_o_attn, w_o_lru, *, tm, tn):
    t, aw = attn.shape
    d = w_o_attn.shape[1]
    nj = d // tn
    return pl.pallas_call(
        _merge_kernel,
        grid=(t // tm, nj),
        in_specs=[pl.BlockSpec((tm, aw), lambda i, j: (i, 0)),
                  pl.BlockSpec((tm, d), lambda i, j: (i, 0)),
                  pl.BlockSpec((tm, tn), lambda i, j: (i, 2 * nj + j)),
                  pl.BlockSpec((tm, tn), lambda i, j: (i, 3 * nj + j)),
                  pl.BlockSpec((aw, tn), lambda i, j: (0, j)),
                  pl.BlockSpec((d, tn), lambda i, j: (0, j))],
        out_specs=pl.BlockSpec((tm, tn), lambda i, j: (i, j)),
        out_shape=jax.ShapeDtypeStruct((t, d), BF16),
        scratch_shapes=[pltpu.VMEM((tm, aw), BF16)],
        compiler_params=_cparams(("parallel", "arbitrary")),
        name="merge",
    )(attn, hg, rest, rest, w_o_attn, w_o_lru)


def _layer_norm(z, g, b):
    mu = jnp.mean(z, axis=-1, keepdims=True)
    zc = z - mu
    var = jnp.mean(zc * zc, axis=-1, keepdims=True)
    return zc * lax.rsqrt(var + LN_EPS) * g + b


def _out_ln_route_kernel(m_ref, x_ref, w_ref, g_ref, b_ref, rwh_ref, rwl_ref, rb_ref,
                         x1_ref, route_ref):
    y = jnp.dot(m_ref[...], w_ref[...], preferred_element_type=F32)
    z = DEEPNORM_ALPHA * x_ref[...] + y
    x1 = _layer_norm(z, g_ref[...], b_ref[...])
    x1_ref[...] = x1

    hi = x1.astype(BF16)
    lo = (x1 - hi.astype(F32)).astype(BF16)
    hi_both = jnp.dot(hi, jnp.concatenate([rwh_ref[...], rwl_ref[...]], axis=1), preferred_element_type=F32)
    lg = (hi_both[:, :ROUTER_LANES] + jnp.dot(lo, rwh_ref[...], preferred_element_type=F32)
          + hi_both[:, ROUTER_LANES:]) + rb_ref[...]

    lane = lax.broadcasted_iota(jnp.int32, lg.shape, 1).astype(F32)
    first = lambda hit: jnp.min(jnp.where(hit, lane, float(ROUTER_LANES)), axis=1, keepdims=True)
    in_grp = lane < N_GROUPS
    lgg = jnp.where(in_grp, lg, MASK_BIAS)
    mg = jnp.max(lgg, axis=1, keepdims=True)
    g_idx = first(lgg == mg)
    p_grp = 1.0 / jnp.sum(jnp.where(in_grp, jnp.exp(lg - mg), 0.0), axis=1, keepdims=True)
    lo_lane = N_GROUPS + EXPERTS_PER_GROUP * g_idx
    in_exp = (lane >= lo_lane) & (lane < lo_lane + EXPERTS_PER_GROUP)
    le = jnp.where(in_exp, lg, MASK_BIAS)
    v1 = jnp.max(le, axis=1, keepdims=True)
    i1 = first((le == v1) & in_exp)
    rest = in_exp & (lane != i1)
    le2 = jnp.where(rest, lg, MASK_BIAS)
    v2 = jnp.max(le2, axis=1, keepdims=True)
    i2 = first((le2 == v2) & rest)
    t = jnp.exp(v2 - v1)
    w1 = p_grp / (1.0 + t)
    w2 = p_grp * t / (1.0 + t)
    route_ref[...] = jnp.where(lane == 0.0, i1 - N_GROUPS,
                     jnp.where(lane == 1.0, i2 - N_GROUPS,
                     jnp.where(lane == 2.0, w1,
                     jnp.where(lane == 3.0, w2, 0.0))))


def _out_ln_route(merged, x2d, w_out, ln_g, ln_b, rw_hi, rw_lo, rb, *, tm):
    t, d = merged.shape
    const = lambda a: pl.BlockSpec(a.shape, lambda i: (0,) * a.ndim)
    return pl.pallas_call(
        _out_ln_route_kernel,
        grid=(t // tm,),
        in_specs=[pl.BlockSpec((tm, d), lambda i: (i, 0)),
                  pl.BlockSpec((tm, d), lambda i: (i, 0)),
                  const(w_out), const(ln_g), const(ln_b), const(rw_hi), const(rw_lo), const(rb)],
        out_specs=[pl.BlockSpec((tm, d), lambda i: (i, 0)),
                   pl.BlockSpec((tm, ROUTER_LANES), lambda i: (i, 0))],
        out_shape=[jax.ShapeDtypeStruct((t, d), F32),
                   jax.ShapeDtypeStruct((t, ROUTER_LANES), F32)],
        compiler_params=_cparams(("parallel",)),
        name="out_ln_route",
    )(merged, x2d, w_out, ln_g, ln_b, rw_hi, rw_lo, rb)


def _moe_kernel(te_ref, nu_ref, g0_ref, gn_ref, dst_ref, x_hbm, wg_ref, wu_ref, wd_ref,
                y_hbm, xbuf, ybuf, wg_sc, wu_sc, wd_sc, gsem, ssem):
    i = pl.program_id(0)
    last = pl.num_programs(0) - 1
    n_used = nu_ref[0]
    tm = xbuf.shape[1]

    def gather_copy(tok, j, s):
        return pltpu.make_async_copy(x_hbm.at[pl.ds(tok, 1)], xbuf.at[s, pl.ds(j, 1)], gsem.at[s])

    def scatter_copy(row, j, s):
        return pltpu.make_async_copy(ybuf.at[s, pl.ds(j, 1)], y_hbm.at[pl.ds(row, 1)], ssem.at[s])

    @pl.when(i == 0)
    def _():
        for j in range(tm):
            gather_copy(g0_ref[0, 0, j], j, 0).start()

    @pl.when((i < n_used) & ((i == 0) | (te_ref[i] != te_ref[jnp.maximum(i - 1, 0)])))
    def _():
        wg_sc[...] = wg_ref[0].astype(BF16)
        wu_sc[...] = wu_ref[0].astype(BF16)
        wd_sc[...] = wd_ref[0].astype(BF16)

    def step(slot):
        nxt = 1 - slot

        @pl.when(i < n_used)
        def _():
            for j in range(tm):
                gather_copy(0, j, slot).wait()

            @pl.when(i + 1 < n_used)
            def _():
                for j in range(tm):
                    gather_copy(gn_ref[0, 0, j], j, nxt).start()

            xb = xbuf[slot].astype(BF16)
            gate = jnp.dot(xb, wg_sc[...], preferred_element_type=F32)
            up = jnp.dot(xb, wu_sc[...], preferred_element_type=F32)
            he = (gate * _sigmoid(gate) * up).astype(BF16)
            ybuf[slot] = jnp.dot(he, wd_sc[...], preferred_element_type=F32)

        for j in range(tm):
            scatter_copy(dst_ref[0, 0, j], j, slot).start()

        @pl.when(i > 0)
        def _():
            for j in range(tm):
                scatter_copy(0, j, nxt).wait()

        @pl.when(i == last)
        def _():
            for j in range(tm):
                scatter_copy(0, j, slot).wait()

    for parity in range(2):
        pl.when(lax.rem(i, 2) == parity)(functools.partial(step, parity))


def _moe_experts(x1, tile_expert, n_used, gidx, dest, wg, wu, wd):
    t, d = x1.shape
    nt = gidx.shape[0]
    f = wg.shape[2]
    idx_spec = lambda fn: pl.BlockSpec((1, 1, MOE_TILE), fn, memory_space=pltpu.SMEM)
    grid_spec = pltpu.PrefetchScalarGridSpec(
        num_scalar_prefetch=2,
        grid=(nt,),
        in_specs=[idx_spec(lambda i, te, nu: (0, 0, 0)),
                  idx_spec(lambda i, te, nu: (jnp.minimum(i + 1, nt - 1), 0, 0)),
                  idx_spec(lambda i, te, nu: (i, 0, 0)),
                  pl.BlockSpec(memory_space=pl.ANY),
                  pl.BlockSpec((1, d, f), lambda i, te, nu: (te[i], 0, 0)),
                  pl.BlockSpec((1, d, f), lambda i, te, nu: (te[i], 0, 0)),
                  pl.BlockSpec((1, f, d), lambda i, te, nu: (te[i], 0, 0))],
        out_specs=pl.BlockSpec(memory_space=pl.ANY),
        scratch_shapes=[pltpu.VMEM((2, MOE_TILE, d), F32),
                        pltpu.VMEM((2, MOE_TILE, d), F32),
                        pltpu.VMEM((d, f), BF16),
                        pltpu.VMEM((d, f), BF16),
                        pltpu.VMEM((f, d), BF16),
                        pltpu.SemaphoreType.DMA((2,)),
                        pltpu.SemaphoreType.DMA((2,))],
    )
    return pl.pallas_call(
        _moe_kernel,
        grid_spec=grid_spec,
        out_shape=jax.ShapeDtypeStruct((nt * MOE_TILE, d), F32),
        compiler_params=_cparams(("arbitrary",)),
        name="moe_experts",
    )(tile_expert, n_used, gidx, gidx, dest, x1, wg, wu, wd)


KEY_IDX_BITS = 16


def _moe_plan(route, n_tiles):
    t = route.shape[0]
    p = 2 * t
    assert p <= 1 << KEY_IDX_BITS and n_tiles * MOE_TILE == p + N_EXPERTS * MOE_TILE
    e_flat = jnp.concatenate([route[:, 0], route[:, 1]]).astype(jnp.int32)
    experts = jnp.arange(N_EXPERTS, dtype=jnp.int32)
    counts = jnp.sum((e_flat[:, None] == experts[None, :]).astype(jnp.int32), axis=0)
    n_pad = (-counts) % MOE_TILE
    k = jnp.arange(MOE_TILE, dtype=jnp.int32)
    pad_expert = jnp.where(k[None, :] < n_pad[:, None], experts[:, None], N_EXPERTS)
    pad_id = experts[:, None] * MOE_TILE + k[None, :]
    flag = 1 << KEY_IDX_BITS
    keys = jnp.concatenate([e_flat * (2 * flag) + jnp.arange(p, dtype=jnp.int32),
                            (pad_expert * (2 * flag) + flag + pad_id).reshape(-1)])
    keys = jnp.sort(keys)
    is_pad = (keys & flag) != 0
    idx = keys & (flag - 1)
    dest = jnp.where(is_pad, p + idx, idx)
    gidx = jnp.where(is_pad, 0, jnp.where(idx >= t, idx - t, idx))
    first_expert = keys.reshape(n_tiles, MOE_TILE)[:, 0] // (2 * flag)
    n_used = jnp.sum((first_expert < N_EXPERTS).astype(jnp.int32)).reshape(1)
    tile_expert = jnp.minimum(first_expert, N_EXPERTS - 1)
    shape3 = (n_tiles, 1, MOE_TILE)
    return tile_expert, n_used, gidx.reshape(shape3), dest.reshape(shape3)


def _final_kernel(x1_ref, y0_ref, y1_ref, route_ref, g_ref, b_ref, o_ref):
    y = route_ref[:, 2:3] * y0_ref[...] + route_ref[:, 3:4] * y1_ref[...]
    z = DEEPNORM_ALPHA * x1_ref[...] + y
    o_ref[...] = _layer_norm(z, g_ref[...], b_ref[...])


def _final_ln(x1, y_rows, route, ln_g, ln_b, *, tm):
    t, d = x1.shape
    nb = t // tm
    const = lambda a: pl.BlockSpec(a.shape, lambda i: (0,) * a.ndim)
    return pl.pallas_call(
        _final_kernel,
        grid=(nb,),
        in_specs=[pl.BlockSpec((tm, d), lambda i: (i, 0)),
                  pl.BlockSpec((tm, d), lambda i: (i, 0)),
                  pl.BlockSpec((tm, d), lambda i: (nb + i, 0)),
                  pl.BlockSpec((tm, ROUTER_LANES), lambda i: (i, 0)),
                  const(ln_g), const(ln_b)],
        out_specs=pl.BlockSpec((tm, d), lambda i: (i, 0)),
        out_shape=jax.ShapeDtypeStruct((t, d), F32),
        compiler_params=_cparams(("parallel",)),
        name="final_ln",
    )(x1, y_rows, y_rows, route, ln_g, ln_b)


def _router_weights(router_group_w, router_group_b, router_expert_w, router_expert_b):
    d = router_group_w.shape[0]
    pad = ROUTER_LANES - N_GROUPS - N_EXPERTS
    w = jnp.concatenate([router_group_w, router_expert_w, jnp.zeros((d, pad), F32)], axis=1)
    b = jnp.concatenate([router_group_b, router_expert_b, jnp.zeros((pad,), F32)]).reshape(1, ROUTER_LANES)
    w_hi = w.astype(BF16)
    w_lo = (w - w_hi.astype(F32)).astype(BF16)
    return w_hi, w_lo, b


def kernel(x, w_in, conv_w, conv_b, lru_wa, lru_ba, lru_wx, lru_bx, lru_lambda, w_o_attn, w_o_lru, w_out, ln1_g, ln1_b, router_group_w, router_group_b, router_expert_w, router_expert_b, w_e_gate, w_e_up, w_e_down, ln2_g, ln2_b):
    batch, seq, d = x.shape
    i_len = seq // CLASSES
    t = batch * seq
    attn_w = N_HEADS * HEAD_DIM
    for l in range(DEPTH):
        x2d = x.reshape(t, d)
        x_bf = x2d.astype(BF16)
        x_cls = _to_class_order(x.reshape(batch, i_len, CLASSES, d)).reshape(CLASSES, batch * i_len, d)
        row = lambda a: a.reshape(1, -1)

        qkv_hm = _qkv_proj(x_cls, w_in[l], n=3 * attn_w, tn=512, cls_per_step=2)
        rest = _rest_proj(x_bf, w_in[l], col0=3 * attn_w, tm=2048, tn=512)
        attn = _dilated_attention(qkv_hm, batch).reshape(t, attn_w)
        hg = _rglru(rest.reshape(batch, seq, 4 * d), conv_w[l], row(conv_b[l]),
                    lru_wa[l].astype(BF16), row(lru_ba[l]), lru_wx[l].astype(BF16), row(lru_bx[l]),
                    row(lru_lambda[l]), d=d, ts=256, cw=512)
        merged = _merge(attn, hg.reshape(t, d), rest, w_o_attn[l].astype(BF16), w_o_lru[l].astype(BF16),
                        tm=1024, tn=512)
        rw_hi, rw_lo, rb = _router_weights(router_group_w[l], router_group_b[l],
                                           router_expert_w[l], router_expert_b[l])
        x1, route = _out_ln_route(merged, x2d, w_out[l].astype(BF16),
                                  row(ln1_g[l]), row(ln1_b[l]), rw_hi, rw_lo, rb, tm=512)

        n_tiles = 2 * t // MOE_TILE + N_EXPERTS
        tile_expert, n_used, gidx, dest = _moe_plan(route, n_tiles)
        y_rows = _moe_experts(x1, tile_expert, n_used, gidx, dest, w_e_gate[l], w_e_up[l], w_e_down[l])
        x = _final_ln(x1, y_rows, route, row(ln2_g[l]), row(ln2_b[l]), tm=512)
        x = x.reshape(batch, seq, d)
    return x
```

```python
import functools
import math

import numpy as np
import jax
import jax.numpy as jnp
from jax import lax
from jax.experimental import pallas as pl
from jax.experimental.pallas import tpu as pltpu

F32 = jnp.float32
BF16 = jnp.bfloat16

N_HEADS = 16
HEAD_DIM = 128
N_BACK = 128
CLASSES = 16
CONV_WIDTH = 4
LRU_BLOCK_DIM = 128
LRU_C = 8.0
N_GROUPS = 4
EXPERTS_PER_GROUP = 8
N_EXPERTS = N_GROUPS * EXPERTS_PER_GROUP
DEPTH = 1
DEEPNORM_ALPHA = (2.0 * DEPTH) ** 0.25
LN_EPS = 1e-5
ATTN_SCALE_LOG2 = HEAD_DIM ** -0.5 * math.log2(math.e)

LANES = 128
SUBLANES = 8
VMEM_LIMIT_BYTES = 56 * 1024 * 1024

MASK_BIAS = -1e30
MOE_TILE = 256
ROUTER_LANES = LANES

PROJ_ROWS, PROJ_COLS = 1024, 1024
MERGE_COLS = 512
NORM_ROWS = 512
LRU_ROWS, LRU_COLS = 256, 512


def _cparams(sem):
    return pltpu.CompilerParams(dimension_semantics=sem, vmem_limit_bytes=VMEM_LIMIT_BYTES)


def _class_order_kernel(x_hbm, o_ref, buf, sem):
    nb = pl.num_programs(1)
    n = pl.program_id(0) * nb + pl.program_id(1)
    slot = lax.rem(n, 2)

    def copy(step, buf_slot):
        return pltpu.make_async_copy(x_hbm.at[lax.rem(step, nb), :, step // nb, :], buf.at[buf_slot],
                                     sem.at[buf_slot])

    @pl.when(n == 0)
    def _():
        copy(n, slot).start()

    @pl.when(n + 1 < pl.num_programs(0) * nb)
    def _():
        copy(n + 1, 1 - slot).start()

    copy(n, slot).wait()
    o_ref[0, 0] = buf[slot].astype(o_ref.dtype)


def _to_class_order(x4):
    b, i, c, d = x4.shape
    return pl.pallas_call(
        _class_order_kernel,
        grid=(c, b),
        in_specs=[pl.BlockSpec(memory_space=pl.ANY)],
        out_specs=pl.BlockSpec((1, 1, i, d), lambda r, bb: (r, bb, 0, 0)),
        out_shape=jax.ShapeDtypeStruct((c, b, i, d), BF16),
        scratch_shapes=[pltpu.VMEM((2, i, d), F32), pltpu.SemaphoreType.DMA((2,))],
        compiler_params=_cparams(("arbitrary", "arbitrary")),
        name="class_order",
    )(x4)


def _qkv_kernel(x_ref, w_ref, o_ref, w_sc):
    @pl.when(pl.program_id(1) == 0)
    def _():
        w_sc[...] = w_ref[...].astype(BF16)

    res = jnp.dot(x_ref[0], w_sc[...], preferred_element_type=F32).astype(o_ref.dtype)
    for hh in range(o_ref.shape[0]):
        o_ref[hh, 0] = res[:, hh * HEAD_DIM:(hh + 1) * HEAD_DIM]


def _qkv_proj(x_cls, w_in, *, n, tn):
    _, rows, d = x_cls.shape
    heads_per_step = tn // HEAD_DIM
    return pl.pallas_call(
        _qkv_kernel,
        grid=(n // tn, CLASSES),
        in_specs=[pl.BlockSpec((1, rows, d), lambda j, r: (r, 0, 0)),
                  pl.BlockSpec((d, tn), lambda j, r: (0, j))],
        out_specs=pl.BlockSpec((heads_per_step, 1, rows, HEAD_DIM), lambda j, r: (j, r, 0, 0)),
        out_shape=jax.ShapeDtypeStruct((n // HEAD_DIM, CLASSES, rows, HEAD_DIM), F32),
        scratch_shapes=[pltpu.VMEM((d, tn), BF16)],
        compiler_params=_cparams(("parallel", "arbitrary")),
        name="qkv_proj",
    )(x_cls, w_in)


def _rest_kernel(x_ref, w_ref, o_ref, w_sc):
    @pl.when(pl.program_id(1) == 0)
    def _():
        w_sc[...] = w_ref[...].astype(BF16)

    o_ref[...] = jnp.dot(x_ref[...], w_sc[...], preferred_element_type=F32).astype(o_ref.dtype)


def _rest_proj(x2d, w_in, *, col0, tm, tn):
    t, d = x2d.shape
    n = w_in.shape[1] - col0
    j0 = col0 // tn
    return pl.pallas_call(
        _rest_kernel,
        grid=(n // tn, t // tm),
        in_specs=[pl.BlockSpec((tm, d), lambda j, i: (i, 0)),
                  pl.BlockSpec((d, tn), lambda j, i: (0, j0 + j))],
        out_specs=pl.BlockSpec((tm, tn), lambda j, i: (i, j)),
        out_shape=jax.ShapeDtypeStruct((t, n), BF16),
        scratch_shapes=[pltpu.VMEM((d, tn), BF16)],
        compiler_params=_cparams(("parallel", "arbitrary")),
        name="rest_proj",
    )(x2d, w_in)


def _band_bias(dist):
    return np.where((dist >= 0) & (dist <= N_BACK), 0.0, MASK_BIAS).astype(np.float32)


Q_BLOCK = 128


def _attn_biases():
    lq = np.arange(Q_BLOCK)[:, None]
    b16_first = _band_bias(lq - np.arange(Q_BLOCK)[None, :])
    b16_next = _band_bias(Q_BLOCK + lq - np.arange(2 * Q_BLOCK)[None, :])
    cq = np.repeat(np.arange(4), 32)[:, None]
    lq = np.tile(np.arange(32), 4)[:, None]
    ck = np.repeat(np.arange(4), 64)[None, :]
    lk = np.tile(np.arange(64), 4)[None, :]
    b4 = np.stack([_band_bias(4 * (lq - lk) + (cq - ck)),
                   _band_bias(4 * (32 + lq - lk) + (cq - ck))])
    rq = np.repeat(np.arange(16), 8)[:, None]
    lq = np.tile(np.arange(8), 16)[:, None]
    rk = np.repeat(np.arange(16), 16)[None, :]
    lk = np.tile(np.arange(16), 16)[None, :]
    b1 = np.stack([_band_bias(16 * (lq - lk) + (rq - rk)),
                   _band_bias(16 * (8 + lq - lk) + (rq - rk))])
    return b16_first, b16_next, b4, b1


def _softmax_blocks(blocks):
    scores = [lax.dot_general(q, k, (((1,), (1,)), ((), ())), preferred_element_type=F32) * ATTN_SCALE_LOG2
              + bias for q, k, _, bias, _ in blocks]
    mids = []
    for s, (_, _, _, _, state) in zip(scores, blocks):
        rows, n = s.shape
        chunks = [s[:, c * LANES:(c + 1) * LANES] for c in range(n // LANES)]
        mx = functools.reduce(jnp.maximum, chunks)
        m_blk = jnp.broadcast_to(jnp.max(mx, axis=1, keepdims=True), (rows, LANES))
        m_new = m_blk if state is None else jnp.maximum(state[0], m_blk)
        ps = [jnp.exp2(c - m_new) for c in chunks]
        l_blk = jnp.broadcast_to(jnp.sum(functools.reduce(jnp.add, ps), axis=1, keepdims=True), (rows, LANES))
        mids.append((m_new, l_blk, jnp.concatenate(ps, axis=1).astype(BF16)))
    out = []
    for (m_new, l_blk, p), (_, _, v, _, state) in zip(mids, blocks):
        pv = jnp.dot(p, v, preferred_element_type=F32)
        if state is None:
            out.append((m_new, l_blk, pv))
        else:
            alpha = jnp.exp2(state[0] - m_new)
            out.append((m_new, alpha * state[1] + l_blk, alpha * state[2] + pv))
    return out


BLOCKS_PER_ITER = 16


def _attn_kernel(q_ref, k_ref, v_ref, b16a_ref, b16b_ref, b4_ref, b1_ref, o_hbm,
                 qb, kb, vb, kb8, vb8, m_sc, l_sc, acc_sc, o_sc, o_sem):
    i_len = q_ref.shape[2]
    blocks_per_class = i_len // Q_BLOCK

    qb[...] = q_ref[0].astype(BF16)
    kb[...] = k_ref[0].astype(BF16)
    vb[...] = v_ref[0].astype(BF16)
    kb8[:, :i_len - SUBLANES, :] = k_ref[0, :, SUBLANES:, :].astype(BF16)
    vb8[:, :i_len - SUBLANES, :] = v_ref[0, :, SUBLANES:, :].astype(BF16)

    classes_per_iter = BLOCKS_PER_ITER // blocks_per_class

    def d16_body(it, carry):
        where, blocks = [], []
        for u in range(classes_per_iter):
            r = it * classes_per_iter + u
            for n in range(blocks_per_class):
                rows = slice(n * Q_BLOCK, (n + 1) * Q_BLOCK)
                keys = slice(0, Q_BLOCK) if n == 0 else slice((n - 1) * Q_BLOCK, (n + 1) * Q_BLOCK)
                bias = b16a_ref[...] if n == 0 else b16b_ref[...]
                where.append((r, rows))
                blocks.append((qb[r, rows, :], kb[r, keys, :], vb[r, keys, :], bias, None))
        for (r, rows), (m, l, acc) in zip(where, _softmax_blocks(blocks)):
            m_sc[r, rows, :] = m
            l_sc[r, rows, :] = l
            acc_sc[r, rows, :] = acc
        return carry

    lax.fori_loop(0, CLASSES // classes_per_iter, d16_body, 0)

    n_per_iter = BLOCKS_PER_ITER // 4

    def d4_body(it, carry):
        where, blocks = [], []
        for u in range(n_per_iter):
            n = it * n_per_iter + u
            q0 = pl.multiple_of(n * 32, 32)
            k0 = pl.multiple_of(jnp.maximum(n - 1, 0) * 32, 32)
            bias = b4_ref[jnp.minimum(n, 1)]
            for r4 in range(4):
                cls = [r4 + 4 * c for c in range(4)]
                cat = lambda ref, start, size, cls=cls: jnp.concatenate(
                    [ref[r, pl.ds(start, size), :] for r in cls], axis=0)
                state = (cat(m_sc, q0, 32), cat(l_sc, q0, 32), cat(acc_sc, q0, 32))
                where.append((cls, q0))
                blocks.append((cat(qb, q0, 32), cat(kb, k0, 64), cat(vb, k0, 64), bias, state))
        for (cls, q0), (m, l, acc) in zip(where, _softmax_blocks(blocks)):
            for c, r in enumerate(cls):
                m_sc[r, pl.ds(q0, 32), :] = m[32 * c:32 * (c + 1)]
                l_sc[r, pl.ds(q0, 32), :] = l[32 * c:32 * (c + 1)]
                acc_sc[r, pl.ds(q0, 32), :] = acc[32 * c:32 * (c + 1)]
        return carry

    lax.fori_loop(0, i_len // 32 // n_per_iter, d4_body, 0)

    def d1_body(it, carry):
        where, blocks = [], []
        for u in range(BLOCKS_PER_ITER):
            mb = it * BLOCKS_PER_ITER + u
            q0 = pl.multiple_of(mb * SUBLANES, SUBLANES)
            q = q_ref[0, :, pl.ds(q0, SUBLANES), :].reshape(Q_BLOCK, HEAD_DIM).astype(BF16)
            if u % 2 == 1:
                k0 = pl.multiple_of((mb - 1) * SUBLANES, 16)
                k, v = kb[:, pl.ds(k0, 16), :], vb[:, pl.ds(k0, 16), :]
            else:
                k0 = pl.multiple_of(jnp.maximum(mb - 2, 0) * SUBLANES, 16)
                k, v = kb8[:, pl.ds(k0, 16), :], vb8[:, pl.ds(k0, 16), :]
                if u == 0:
                    k = jnp.where(mb == 0, kb[:, 0:16, :], k)
                    v = jnp.where(mb == 0, vb[:, 0:16, :], v)
            flat = lambda ref, q0=q0: ref[:, pl.ds(q0, SUBLANES), :].reshape(Q_BLOCK, LANES)
            where.append(q0)
            blocks.append((q, k.reshape(2 * Q_BLOCK, HEAD_DIM), v.reshape(2 * Q_BLOCK, HEAD_DIM),
                           b1_ref[jnp.minimum(mb, 1)], (flat(m_sc), flat(l_sc), flat(acc_sc))))
        for q0, (m, l, acc) in zip(where, _softmax_blocks(blocks)):
            m_sc[:, pl.ds(q0, SUBLANES), :] = m.reshape(CLASSES, SUBLANES, LANES)
            l_sc[:, pl.ds(q0, SUBLANES), :] = l.reshape(CLASSES, SUBLANES, LANES)
            acc_sc[:, pl.ds(q0, SUBLANES), :] = acc.reshape(CLASSES, SUBLANES, LANES)
        return carry

    lax.fori_loop(0, i_len // SUBLANES // BLOCKS_PER_ITER, d1_body, 0)

    hh, bb = pl.program_id(0), pl.program_id(1)
    step = hh * pl.num_programs(1) + bb
    n_steps = pl.num_programs(0) * pl.num_programs(1)
    col = pl.multiple_of(hh * HEAD_DIM, HEAD_DIM)

    def out_copy(r, b_idx, col0):
        return pltpu.make_async_copy(o_sc.at[r], o_hbm.at[b_idx, :, r, pl.ds(col0, HEAD_DIM)], o_sem)

    @pl.when(step > 0)
    def _():
        for r in range(CLASSES):
            out_copy(r, 0, 0).wait()

    o_sc[...] = acc_sc[...] / l_sc[...]
    for r in range(CLASSES):
        out_copy(r, bb, col).start()

    @pl.when(step == n_steps - 1)
    def _():
        for r in range(CLASSES):
            out_copy(r, 0, 0).wait()


def _dilated_attention(qkv_hm, batch):
    n3, _, bi, _ = qkv_hm.shape
    h = n3 // 3
    i_len = bi // batch
    assert CLASSES * SUBLANES == Q_BLOCK and i_len % Q_BLOCK == 0 and BLOCKS_PER_ITER % (i_len // Q_BLOCK) == 0
    biases = [jnp.asarray(a) for a in _attn_biases()]
    blk = (1, CLASSES, i_len, HEAD_DIM)
    const = lambda a: pl.BlockSpec(a.shape, lambda hh, bb: (0,) * a.ndim)
    operand = lambda: pltpu.VMEM((CLASSES, i_len, HEAD_DIM), BF16)
    state = lambda: pltpu.VMEM((CLASSES, i_len, LANES), F32)
    return pl.pallas_call(
        _attn_kernel,
        grid=(h, batch),
        in_specs=[pl.BlockSpec(blk, lambda hh, bb: (hh, 0, bb, 0)),
                  pl.BlockSpec(blk, lambda hh, bb: (h + hh, 0, bb, 0)),
                  pl.BlockSpec(blk, lambda hh, bb: (2 * h + hh, 0, bb, 0))] + [const(a) for a in biases],
        out_specs=pl.BlockSpec(memory_space=pl.ANY),
        out_shape=jax.ShapeDtypeStruct((batch, i_len, CLASSES, h * HEAD_DIM), F32),
        scratch_shapes=[operand(), operand(), operand(), operand(), operand(), state(), state(), state(),
                        pltpu.VMEM((CLASSES, i_len, HEAD_DIM), F32), pltpu.SemaphoreType.DMA(())],
        compiler_params=_cparams(("arbitrary", "arbitrary")),
        name="dilated_attn",
    )(qkv_hm, qkv_hm, qkv_hm, *biases)


LOG2_E = math.log2(math.e)
SEG_LEN = 4
SCAN_ROWS = SUBLANES * SEG_LEN


def _softplus(z):
    return jnp.maximum(z, 0.0) + jnp.log1p(jnp.exp(-jnp.abs(z)))


def _sigmoid(z):
    return 1.0 / (1.0 + jnp.exp2(z * (-LOG2_E)))


def _lru_kernel(xr_ref, xg_ref, cw_ref, cb_ref, wa_ref, ba_ref, wx_ref, bx_ref, lam_ref,
                o_ref, h_sc, tail_sc, a_sc, u_sc, hs_sc):
    ts = xr_ref.shape[1]
    cw = xr_ref.shape[2]
    nv = ts // SUBLANES

    @pl.when(pl.program_id(2) == 0)
    def _():
        h_sc[...] = jnp.zeros_like(h_sc)
        tail_sc[...] = jnp.zeros_like(tail_sc)

    seg = lax.broadcasted_iota(jnp.int32, (SUBLANES, LANES), 0)
    for g in range(cw // LANES):
        sl = slice(g * LANES, (g + 1) * LANES)
        xr = xr_ref[0, :, sl].astype(F32)
        xfull = jnp.concatenate([tail_sc[:, sl], xr], axis=0)
        taps = cw_ref[:, sl]
        xc = xr * taps[3:4]
        for j in range(1, CONV_WIDTH):
            shifted = pltpu.roll(xfull, j, axis=0)[SUBLANES:]
            xc = xc + shifted * taps[3 - j:4 - j]
        xc = xc + cb_ref[:, sl]
        tail_sc[:, sl] = xr[ts - SUBLANES:]

        xcb = xc.astype(BF16)
        gate_a = jnp.dot(xcb, wa_ref[g], preferred_element_type=F32) + ba_ref[:, sl]
        gate_x = jnp.dot(xcb, wx_ref[g], preferred_element_type=F32) + bx_ref[:, sl]
        r = _sigmoid(gate_a)
        i = _sigmoid(gate_x)
        a = jnp.exp2(((-LRU_C * LOG2_E) * _softplus(-lam_ref[:, sl])) * r)
        y = 1.0 - a * a
        u = jnp.where(y > 0.0, y * lax.rsqrt(y), 0.0) * (i * xc)

        a_sc[g] = a
        u_sc[g] = u
        h = h_sc[:, sl]
        for run in range(ts // SCAN_ROWS):
            rows = lambda v: pl.ds(run * SCAN_ROWS + v, SUBLANES, stride=SEG_LEN)
            av = [a_sc[g, rows(v), :] for v in range(SEG_LEN)]
            uv = [u_sc[g, rows(v), :] for v in range(SEG_LEN)]
            local, decay = [uv[0]], [av[0]]
            for v in range(1, SEG_LEN):
                local.append(av[v] * local[-1] + uv[v])
                decay.append(av[v] * decay[-1])
            p, e = decay[-1], local[-1]
            for k in (1, 2, 4):
                e = p * jnp.where(seg >= k, pltpu.roll(e, k, axis=0), 0.0) + e
                p = p * jnp.where(seg >= k, pltpu.roll(p, k, axis=0), 1.0)
            p_in = jnp.where(seg >= 1, pltpu.roll(p, 1, axis=0), 1.0)
            e_in = jnp.where(seg >= 1, pltpu.roll(e, 1, axis=0), 0.0)
            start = e_in + p_in * h
            for v in range(SEG_LEN):
                hs_sc[g, rows(v), :] = local[v] + decay[v] * start
            h = (e + p * h)[SUBLANES - 1:SUBLANES]
        h_sc[:, sl] = h
        o_ref[0, :, sl] = (hs_sc[g] * jax.nn.gelu(xg_ref[0, :, sl].astype(F32))).astype(o_ref.dtype)


def _rglru(rest3, conv_w, conv_b, wa, ba, wx, bx, lam, *, d, ts, cw):
    b, s, _ = rest3.shape
    ncw = d // cw
    gpb = cw // LRU_BLOCK_DIM
    vec = lambda: pl.BlockSpec((1, cw), lambda bb, c, t: (0, c))
    return pl.pallas_call(
        _lru_kernel,
        grid=(b, ncw, s // ts),
        in_specs=[pl.BlockSpec((1, ts, cw), lambda bb, c, t: (bb, t, c)),
                  pl.BlockSpec((1, ts, cw), lambda bb, c, t: (bb, t, ncw + c)),
                  pl.BlockSpec((CONV_WIDTH, cw), lambda bb, c, t: (0, c)),
                  vec(),
                  pl.BlockSpec((gpb, LRU_BLOCK_DIM, LRU_BLOCK_DIM), lambda bb, c, t: (c, 0, 0)),
                  vec(),
                  pl.BlockSpec((gpb, LRU_BLOCK_DIM, LRU_BLOCK_DIM), lambda bb, c, t: (c, 0, 0)),
                  vec(), vec()],
        out_specs=pl.BlockSpec((1, ts, cw), lambda bb, c, t: (bb, t, c)),
        out_shape=jax.ShapeDtypeStruct((b, s, d), BF16),
        scratch_shapes=[pltpu.VMEM((1, cw), F32), pltpu.VMEM((SUBLANES, cw), F32)]
        + [pltpu.VMEM((gpb, ts, LANES), F32)] * 3,
        compiler_params=_cparams(("parallel", "parallel", "arbitrary")),
        name="rglru",
    )(rest3, rest3, conv_w, conv_b, wa, ba, wx, bx, lam)


def _merge_kernel(a_ref, h_ref, ga_ref, gl_ref, wa_ref, wl_ref, o_ref, a_sc):
    @pl.when(pl.program_id(1) == 0)
    def _():
        a_sc[...] = a_ref[...].astype(BF16)

    y_attn = jnp.dot(a_sc[...], wa_ref[...], preferred_element_type=F32)
    y_lru = jnp.dot(h_ref[...], wl_ref[...], preferred_element_type=F32)
    g_a = _sigmoid(ga_ref[...].astype(F32))
    g_l = _sigmoid(gl_ref[...].astype(F32))
    o_ref[...] = (g_a * y_attn + g_l * y_lru).astype(o_ref.dtype)


def _merge(attn, hg, rest, w_o_attn, w_o_lru, *, tm, tn):
    t, aw = attn.shape
    d = w_o_attn.shape[1]
    nj = d // tn
    return pl.pallas_call(
        _merge_kernel,
        grid=(t // tm, nj),
        in_specs=[pl.BlockSpec((tm, aw), lambda i, j: (i, 0)),
                  pl.BlockSpec((tm, d), lambda i, j: (i, 0)),
                  pl.BlockSpec((tm, tn), lambda i, j: (i, 2 * nj + j)),
                  pl.BlockSpec((tm, tn), lambda i, j: (i, 3 * nj + j)),
                  pl.BlockSpec((aw, tn), lambda i, j: (0, j)),
                  pl.BlockSpec((d, tn), lambda i, j: (0, j))],
        out_specs=pl.BlockSpec((tm, tn), lambda i, j: (i, j)),
        out_shape=jax.ShapeDtypeStruct((t, d), BF16),
        scratch_shapes=[pltpu.VMEM((tm, aw), BF16)],
        compiler_params=_cparams(("parallel", "arbitrary")),
        name="merge",
    )(attn, hg, rest, rest, w_o_attn, w_o_lru)


def _layer_norm(z, g, b):
    mu = jnp.mean(z, axis=-1, keepdims=True)
    zc = z - mu
    var = jnp.mean(zc * zc, axis=-1, keepdims=True)
    return zc * lax.rsqrt(var + LN_EPS) * g + b


def _out_ln_route_kernel(m_ref, x_ref, w_ref, g_ref, b_ref, rwh_ref, rwl_ref, rb_ref,
                         x1_ref, route_ref):
    y = jnp.dot(m_ref[...], w_ref[...], preferred_element_type=F32)
    z = DEEPNORM_ALPHA * x_ref[...] + y
    x1 = _layer_norm(z, g_ref[...], b_ref[...])
    x1_ref[...] = x1

    hi = x1.astype(BF16)
    lo = (x1 - hi.astype(F32)).astype(BF16)
    hi_both = jnp.dot(hi, jnp.concatenate([rwh_ref[...], rwl_ref[...]], axis=1), preferred_element_type=F32)
    lg = (hi_both[:, :ROUTER_LANES] + jnp.dot(lo, rwh_ref[...], preferred_element_type=F32)
          + hi_both[:, ROUTER_LANES:]) + rb_ref[...]

    lane = lax.broadcasted_iota(jnp.int32, lg.shape, 1).astype(F32)
    first = lambda hit: jnp.min(jnp.where(hit, lane, float(ROUTER_LANES)), axis=1, keepdims=True)
    in_grp = lane < N_GROUPS
    lgg = jnp.where(in_grp, lg, MASK_BIAS)
    mg = jnp.max(lgg, axis=1, keepdims=True)
    g_idx = first(lgg == mg)
    p_grp = 1.0 / jnp.sum(jnp.where(in_grp, jnp.exp(lg - mg), 0.0), axis=1, keepdims=True)
    lo_lane = N_GROUPS + EXPERTS_PER_GROUP * g_idx
    in_exp = (lane >= lo_lane) & (lane < lo_lane + EXPERTS_PER_GROUP)
    le = jnp.where(in_exp, lg, MASK_BIAS)
    v1 = jnp.max(le, axis=1, keepdims=True)
    i1 = first((le == v1) & in_exp)
    rest = in_exp & (lane != i1)
    le2 = jnp.where(rest, lg, MASK_BIAS)
    v2 = jnp.max(le2, axis=1, keepdims=True)
    i2 = first((le2 == v2) & rest)
    t = jnp.exp(v2 - v1)
    w1 = p_grp / (1.0 + t)
    w2 = p_grp * t / (1.0 + t)
    route_ref[...] = jnp.where(lane == 0.0, i1 - N_GROUPS,
                     jnp.where(lane == 1.0, i2 - N_GROUPS,
                     jnp.where(lane == 2.0, w1,
                     jnp.where(lane == 3.0, w2, 0.0))))


def _out_ln_route(merged, x2d, w_out, ln_g, ln_b, rw_hi, rw_lo, rb, *, tm):
    t, d = merged.shape
    const = lambda a: pl.BlockSpec(a.shape, lambda i: (0,) * a.ndim)
    return pl.pallas_call(
        _out_ln_route_kernel,
        grid=(t // tm,),
        in_specs=[pl.BlockSpec((tm, d), lambda i: (i, 0)),
                  pl.BlockSpec((tm, d), lambda i: (i, 0)),
                  const(w_out), const(ln_g), const(ln_b), const(rw_hi), const(rw_lo), const(rb)],
        out_specs=[pl.BlockSpec((tm, d), lambda i: (i, 0)),
                   pl.BlockSpec((tm, ROUTER_LANES), lambda i: (i, 0))],
        out_shape=[jax.ShapeDtypeStruct((t, d), F32),
                   jax.ShapeDtypeStruct((t, ROUTER_LANES), F32)],
        compiler_params=_cparams(("parallel",)),
        name="out_ln_route",
    )(merged, x2d, w_out, ln_g, ln_b, rw_hi, rw_lo, rb)


def _moe_kernel(te_ref, nu_ref, g0_ref, gn_ref, dst_ref, x_hbm, wg_ref, wu_ref, wd_ref,
                y_hbm, xbuf, ybuf, wg_sc, wu_sc, wd_sc, gsem, ssem):
    i = pl.program_id(0)
    last = pl.num_programs(0) - 1
    n_used = nu_ref[0]
    tm = xbuf.shape[1]

    def gather_copy(tok, j, s):
        return pltpu.make_async_copy(x_hbm.at[pl.ds(tok, 1)], xbuf.at[s, pl.ds(j, 1)], gsem.at[s])

    def scatter_copy(row, j, s):
        return pltpu.make_async_copy(ybuf.at[s, pl.ds(j, 1)], y_hbm.at[pl.ds(row, 1)], ssem.at[s])

    @pl.when(i == 0)
    def _():
        for j in range(tm):
            gather_copy(g0_ref[0, 0, j], j, 0).start()

    @pl.when((i < n_used) & ((i == 0) | (te_ref[i] != te_ref[jnp.maximum(i - 1, 0)])))
    def _():
        wg_sc[...] = wg_ref[0].astype(BF16)
        wu_sc[...] = wu_ref[0].astype(BF16)
        wd_sc[...] = wd_ref[0].astype(BF16)

    def step(slot):
        nxt = 1 - slot

        @pl.when(i < n_used)
        def _():
            for j in range(tm):
                gather_copy(0, j, slot).wait()

            @pl.when(i + 1 < n_used)
            def _():
                for j in range(tm):
                    gather_copy(gn_ref[0, 0, j], j, nxt).start()

            xb = xbuf[slot].astype(BF16)
            gate = jnp.dot(xb, wg_sc[...], preferred_element_type=F32)
            up = jnp.dot(xb, wu_sc[...], preferred_element_type=F32)
            he = (gate * _sigmoid(gate) * up).astype(BF16)
            ybuf[slot] = jnp.dot(he, wd_sc[...], preferred_element_type=F32)

        for j in range(tm):
            scatter_copy(dst_ref[0, 0, j], j, slot).start()

        @pl.when(i > 0)
        def _():
            for j in range(tm):
                scatter_copy(0, j, nxt).wait()

        @pl.when(i == last)
        def _():
            for j in range(tm):
                scatter_copy(0, j, slot).wait()

    for parity in range(2):
        pl.when(lax.rem(i, 2) == parity)(functools.partial(step, parity))


def _moe_experts(x1, tile_expert, n_used, gidx, dest, wg, wu, wd):
    t, d = x1.shape
    nt = gidx.shape[0]
    f = wg.shape[2]
    idx_spec = lambda fn: pl.BlockSpec((1, 1, MOE_TILE), fn, memory_space=pltpu.SMEM)
    grid_spec = pltpu.PrefetchScalarGridSpec(
        num_scalar_prefetch=2,
        grid=(nt,),
        in_specs=[idx_spec(lambda i, te, nu: (0, 0, 0)),
                  idx_spec(lambda i, te, nu: (jnp.minimum(i + 1, nt - 1), 0, 0)),
                  idx_spec(lambda i, te, nu: (i, 0, 0)),
                  pl.BlockSpec(memory_space=pl.ANY),
                  pl.BlockSpec((1, d, f), lambda i, te, nu: (te[i], 0, 0)),
                  pl.BlockSpec((1, d, f), lambda i, te, nu: (te[i], 0, 0)),
                  pl.BlockSpec((1, f, d), lambda i, te, nu: (te[i], 0, 0))],
        out_specs=pl.BlockSpec(memory_space=pl.ANY),
        scratch_shapes=[pltpu.VMEM((2, MOE_TILE, d), F32),
                        pltpu.VMEM((2, MOE_TILE, d), F32),
                        pltpu.VMEM((d, f), BF16),
                        pltpu.VMEM((d, f), BF16),
                        pltpu.VMEM((f, d), BF16),
                        pltpu.SemaphoreType.DMA((2,)),
                        pltpu.SemaphoreType.DMA((2,))],
    )
    return pl.pallas_call(
        _moe_kernel,
        grid_spec=grid_spec,
        out_shape=jax.ShapeDtypeStruct((nt * MOE_TILE, d), F32),
        compiler_params=_cparams(("arbitrary",)),
        name="moe_experts",
    )(tile_expert, n_used, gidx, gidx, dest, x1, wg, wu, wd)


KEY_IDX_BITS = 16


def _moe_plan(route, n_tiles):
    t = route.shape[0]
    p = 2 * t
    assert p <= 1 << KEY_IDX_BITS and n_tiles * MOE_TILE == p + N_EXPERTS * MOE_TILE
    e_flat = jnp.concatenate([route[:, 0], route[:, 1]]).astype(jnp.int32)
    experts = jnp.arange(N_EXPERTS, dtype=jnp.int32)
    counts = jnp.sum((e_flat[:, None] == experts[None, :]).astype(jnp.int32), axis=0)
    n_pad = (-counts) % MOE_TILE
    k = jnp.arange(MOE_TILE, dtype=jnp.int32)
    pad_expert = jnp.where(k[None, :] < n_pad[:, None], experts[:, None], N_EXPERTS)
    pad_id = experts[:, None] * MOE_TILE + k[None, :]
    flag = 1 << KEY_IDX_BITS
    keys = jnp.concatenate([e_flat * (2 * flag) + jnp.arange(p, dtype=jnp.int32),
                            (pad_expert * (2 * flag) + flag + pad_id).reshape(-1)])
    keys = jnp.sort(keys)
    is_pad = (keys & flag) != 0
    idx = keys & (flag - 1)
    dest = jnp.where(is_pad, p + idx, idx)
    gidx = jnp.where(is_pad, 0, jnp.where(idx >= t, idx - t, idx))
    first_expert = keys.reshape(n_tiles, MOE_TILE)[:, 0] // (2 * flag)
    n_used = jnp.sum((first_expert < N_EXPERTS).astype(jnp.int32)).reshape(1)
    tile_expert = jnp.minimum(first_expert, N_EXPERTS - 1)
    shape3 = (n_tiles, 1, MOE_TILE)
    return tile_expert, n_used, gidx.reshape(shape3), dest.reshape(shape3)


def _final_kernel(x1_ref, y0_ref, y1_ref, route_ref, g_ref, b_ref, o_ref):
    y = route_ref[:, 2:3] * y0_ref[...] + route_ref[:, 3:4] * y1_ref[...]
    z = DEEPNORM_ALPHA * x1_ref[...] + y
    o_ref[...] = _layer_norm(z, g_ref[...], b_ref[...])


def _final_ln(x1, y_rows, route, ln_g, ln_b, *, tm):
    t, d = x1.shape
    nb = t // tm
    const = lambda a: pl.BlockSpec(a.shape, lambda i: (0,) * a.ndim)
    return pl.pallas_call(
        _final_kernel,
        grid=(nb,),
        in_specs=[pl.BlockSpec((tm, d), lambda i: (i, 0)),
                  pl.BlockSpec((tm, d), lambda i: (i, 0)),
                  pl.BlockSpec((tm, d), lambda i: (nb + i, 0)),
                  pl.BlockSpec((tm, ROUTER_LANES), lambda i: (i, 0)),
                  const(ln_g), const(ln_b)],
        out_specs=pl.BlockSpec((tm, d), lambda i: (i, 0)),
        out_shape=jax.ShapeDtypeStruct((t, d), F32),
        compiler_params=_cparams(("parallel",)),
        name="final_ln",
    )(x1, y_rows, y_rows, route, ln_g, ln_b)


def _router_weights(router_group_w, router_group_b, router_expert_w, router_expert_b):
    d = router_group_w.shape[0]
    pad = ROUTER_LANES - N_GROUPS - N_EXPERTS
    w = jnp.concatenate([router_group_w, router_expert_w, jnp.zeros((d, pad), F32)], axis=1)
    b = jnp.concatenate([router_group_b, router_expert_b, jnp.zeros((pad,), F32)]).reshape(1, ROUTER_LANES)
    w_hi = w.astype(BF16)
    w_lo = (w - w_hi.astype(F32)).astype(BF16)
    return w_hi, w_lo, b


def kernel(x, w_in, conv_w, conv_b, lru_wa, lru_ba, lru_wx, lru_bx, lru_lambda, w_o_attn, w_o_lru, w_out, ln1_g, ln1_b, router_group_w, router_group_b, router_expert_w, router_expert_b, w_e_gate, w_e_up, w_e_down, ln2_g, ln2_b):
    batch, seq, d = x.shape
    i_len = seq // CLASSES
    t = batch * seq
    attn_w = N_HEADS * HEAD_DIM
    for l in range(DEPTH):
        x2d = x.reshape(t, d)
        x_bf = x2d.astype(BF16)
        x_cls = _to_class_order(x.reshape(batch, i_len, CLASSES, d)).reshape(CLASSES, batch * i_len, d)
        row = lambda a: a.reshape(1, -1)

        qkv_hm = _qkv_proj(x_cls, w_in[l], n=3 * attn_w, tn=PROJ_COLS)
        rest = _rest_proj(x_bf, w_in[l], col0=3 * attn_w, tm=PROJ_ROWS, tn=PROJ_COLS)
        attn = _dilated_attention(qkv_hm, batch).reshape(t, attn_w)
        hg = _rglru(rest.reshape(batch, seq, 4 * d), conv_w[l], row(conv_b[l]),
                    lru_wa[l].astype(BF16), row(lru_ba[l]), lru_wx[l].astype(BF16), row(lru_bx[l]),
                    row(lru_lambda[l]), d=d, ts=LRU_ROWS, cw=LRU_COLS)
        merged = _merge(attn, hg.reshape(t, d), rest, w_o_attn[l].astype(BF16), w_o_lru[l].astype(BF16),
                        tm=PROJ_ROWS, tn=MERGE_COLS)
        rw_hi, rw_lo, rb = _router_weights(router_group_w[l], router_group_b[l],
                                           router_expert_w[l], router_expert_b[l])
        x1, route = _out_ln_route(merged, x2d, w_out[l].astype(BF16),
                                  row(ln1_g[l]), row(ln1_b[l]), rw_hi, rw_lo, rb, tm=NORM_ROWS)

        n_tiles = 2 * t // MOE_TILE + N_EXPERTS
        tile_expert, n_used, gidx, dest = _moe_plan(route, n_tiles)
        y_rows = _moe_experts(x1, tile_expert, n_used, gidx, dest, w_e_gate[l], w_e_up[l], w_e_down[l])
        x = _final_ln(x1, y_rows, route, row(ln2_g[l]), row(ln2_b[l]), tm=NORM_ROWS)
        x = x.reshape(batch, seq, d)
    return x
```

```python
import functools
import math

import numpy as np
import jax
import jax.numpy as jnp
from jax import lax
from jax.experimental import pallas as pl
from jax.experimental.pallas import tpu as pltpu

F32 = jnp.float32
BF16 = jnp.bfloat16

N_HEADS = 16
HEAD_DIM = 128
N_BACK = 128
CLASSES = 16
CONV_WIDTH = 4
LRU_BLOCK_DIM = 128
LRU_C = 8.0
N_GROUPS = 4
EXPERTS_PER_GROUP = 8
N_EXPERTS = N_GROUPS * EXPERTS_PER_GROUP
DEPTH = 1
DEEPNORM_ALPHA = (2.0 * DEPTH) ** 0.25
LN_EPS = 1e-5
ATTN_SCALE_LOG2 = HEAD_DIM ** -0.5 * math.log2(math.e)

LANES = 128
SUBLANES = 8
VMEM_LIMIT_BYTES = 56 * 1024 * 1024

MASK_BIAS = -1e30
MOE_TILE = 256
ROUTER_LANES = LANES

PROJ_ROWS, PROJ_COLS = 1024, 1024
MERGE_COLS = 512
NORM_ROWS = 512
LRU_ROWS, LRU_COLS = 512, 512


def _cparams(sem):
    return pltpu.CompilerParams(dimension_semantics=sem, vmem_limit_bytes=VMEM_LIMIT_BYTES)


def _class_order_kernel(x_hbm, o_ref, buf, sem):
    nb = pl.num_programs(1)
    n = pl.program_id(0) * nb + pl.program_id(1)
    slot = lax.rem(n, 2)

    def copy(step, buf_slot):
        return pltpu.make_async_copy(x_hbm.at[lax.rem(step, nb), :, step // nb, :], buf.at[buf_slot],
                                     sem.at[buf_slot])

    @pl.when(n == 0)
    def _():
        copy(n, slot).start()

    @pl.when(n + 1 < pl.num_programs(0) * nb)
    def _():
        copy(n + 1, 1 - slot).start()

    copy(n, slot).wait()
    o_ref[0, 0] = buf[slot].astype(o_ref.dtype)


def _to_class_order(x4):
    b, i, c, d = x4.shape
    return pl.pallas_call(
        _class_order_kernel,
        grid=(c, b),
        in_specs=[pl.BlockSpec(memory_space=pl.ANY)],
        out_specs=pl.BlockSpec((1, 1, i, d), lambda r, bb: (r, bb, 0, 0)),
        out_shape=jax.ShapeDtypeStruct((c, b, i, d), BF16),
        scratch_shapes=[pltpu.VMEM((2, i, d), F32), pltpu.SemaphoreType.DMA((2,))],
        compiler_params=_cparams(("arbitrary", "arbitrary")),
        name="class_order",
    )(x4)


def _qkv_kernel(x_ref, w_ref, o_ref, w_sc):
    @pl.when(pl.program_id(1) == 0)
    def _():
        w_sc[...] = w_ref[...].astype(BF16)

    res = jnp.dot(x_ref[0], w_sc[...], preferred_element_type=F32).astype(o_ref.dtype)
    for hh in range(o_ref.shape[0]):
        o_ref[hh, 0] = res[:, hh * HEAD_DIM:(hh + 1) * HEAD_DIM]


def _qkv_proj(x_cls, w_in, *, n, tn):
    _, rows, d = x_cls.shape
    heads_per_step = tn // HEAD_DIM
    return pl.pallas_call(
        _qkv_kernel,
        grid=(n // tn, CLASSES),
        in_specs=[pl.BlockSpec((1, rows, d), lambda j, r: (r, 0, 0)),
                  pl.BlockSpec((d, tn), lambda j, r: (0, j))],
        out_specs=pl.BlockSpec((heads_per_step, 1, rows, HEAD_DIM), lambda j, r: (j, r, 0, 0)),
        out_shape=jax.ShapeDtypeStruct((n // HEAD_DIM, CLASSES, rows, HEAD_DIM), F32),
        scratch_shapes=[pltpu.VMEM((d, tn), BF16)],
        compiler_params=_cparams(("parallel", "arbitrary")),
        name="qkv_proj",
    )(x_cls, w_in)


def _rest_kernel(x_ref, w_ref, o_ref, w_sc):
    @pl.when(pl.program_id(1) == 0)
    def _():
        w_sc[...] = w_ref[...].astype(BF16)

    o_ref[...] = jnp.dot(x_ref[...], w_sc[...], preferred_element_type=F32).astype(o_ref.dtype)


def _rest_proj(x2d, w_in, *, col0, tm, tn):
    t, d = x2d.shape
    n = w_in.shape[1] - col0
    j0 = col0 // tn
    return pl.pallas_call(
        _rest_kernel,
        grid=(n // tn, t // tm),
        in_specs=[pl.BlockSpec((tm, d), lambda j, i: (i, 0)),
                  pl.BlockSpec((d, tn), lambda j, i: (0, j0 + j))],
        out_specs=pl.BlockSpec((tm, tn), lambda j, i: (i, j)),
        out_shape=jax.ShapeDtypeStruct((t, n), BF16),
        scratch_shapes=[pltpu.VMEM((d, tn), BF16)],
        compiler_params=_cparams(("parallel", "arbitrary")),
        name="rest_proj",
    )(x2d, w_in)


def _band_bias(dist):
    return np.where((dist >= 0) & (dist <= N_BACK), 0.0, MASK_BIAS).astype(np.float32)


Q_BLOCK = 128


def _attn_biases():
    lq = np.arange(Q_BLOCK)[:, None]
    b16_first = _band_bias(lq - np.arange(Q_BLOCK)[None, :])
    b16_next = _band_bias(Q_BLOCK + lq - np.arange(2 * Q_BLOCK)[None, :])
    cq = np.repeat(np.arange(4), 32)[:, None]
    lq = np.tile(np.arange(32), 4)[:, None]
    ck = np.repeat(np.arange(4), 64)[None, :]
    lk = np.tile(np.arange(64), 4)[None, :]
    b4 = np.stack([_band_bias(4 * (lq - lk) + (cq - ck)),
                   _band_bias(4 * (32 + lq - lk) + (cq - ck))])
    rq = np.repeat(np.arange(16), 8)[:, None]
    lq = np.tile(np.arange(8), 16)[:, None]
    rk = np.repeat(np.arange(16), 16)[None, :]
    lk = np.tile(np.arange(16), 16)[None, :]
    b1 = np.stack([_band_bias(16 * (lq - lk) + (rq - rk)),
                   _band_bias(16 * (8 + lq - lk) + (rq - rk))])
    return b16_first, b16_next, b4, b1


def _softmax_blocks(blocks):
    scores = [lax.dot_general(q, k, (((1,), (1,)), ((), ())), preferred_element_type=F32) * ATTN_SCALE_LOG2
              + bias for q, k, _, bias, _ in blocks]
    mids = []
    for s, (_, _, _, _, state) in zip(scores, blocks):
        rows, n = s.shape
        chunks = [s[:, c * LANES:(c + 1) * LANES] for c in range(n // LANES)]
        mx = functools.reduce(jnp.maximum, chunks)
        m_blk = jnp.broadcast_to(jnp.max(mx, axis=1, keepdims=True), (rows, LANES))
        m_new = m_blk if state is None else jnp.maximum(state[0], m_blk)
        ps = [jnp.exp2(c - m_new) for c in chunks]
        l_blk = jnp.broadcast_to(jnp.sum(functools.reduce(jnp.add, ps), axis=1, keepdims=True), (rows, LANES))
        mids.append((m_new, l_blk, jnp.concatenate(ps, axis=1).astype(BF16)))
    out = []
    for (m_new, l_blk, p), (_, _, v, _, state) in zip(mids, blocks):
        pv = jnp.dot(p, v, preferred_element_type=F32)
        if state is None:
            out.append((m_new, l_blk, pv))
        else:
            alpha = jnp.exp2(state[0] - m_new)
            out.append((m_new, alpha * state[1] + l_blk, alpha * state[2] + pv))
    return out


BLOCKS_PER_ITER = 16


def _attn_kernel(q_ref, k_ref, v_ref, b16a_ref, b16b_ref, b4_ref, b1_ref, o_hbm,
                 qb, kb, vb, kb8, vb8, m_sc, l_sc, acc_sc, o_sc, o_sem):
    i_len = q_ref.shape[2]
    blocks_per_class = i_len // Q_BLOCK

    qb[...] = q_ref[0].astype(BF16)
    kb[...] = k_ref[0].astype(BF16)
    vb[...] = v_ref[0].astype(BF16)
    kb8[:, :i_len - SUBLANES, :] = k_ref[0, :, SUBLANES:, :].astype(BF16)
    vb8[:, :i_len - SUBLANES, :] = v_ref[0, :, SUBLANES:, :].astype(BF16)

    classes_per_iter = BLOCKS_PER_ITER // blocks_per_class

    def d16_body(it, carry):
        where, blocks = [], []
        for u in range(classes_per_iter):
            r = it * classes_per_iter + u
            for n in range(blocks_per_class):
                rows = slice(n * Q_BLOCK, (n + 1) * Q_BLOCK)
                keys = slice(0, Q_BLOCK) if n == 0 else slice((n - 1) * Q_BLOCK, (n + 1) * Q_BLOCK)
                bias = b16a_ref[...] if n == 0 else b16b_ref[...]
                where.append((r, rows))
                blocks.append((qb[r, rows, :], kb[r, keys, :], vb[r, keys, :], bias, None))
        for (r, rows), (m, l, acc) in zip(where, _softmax_blocks(blocks)):
            m_sc[r, rows, :] = m
            l_sc[r, rows, :] = l
            acc_sc[r, rows, :] = acc
        return carry

    lax.fori_loop(0, CLASSES // classes_per_iter, d16_body, 0)

    n_per_iter = BLOCKS_PER_ITER // 4

    def d4_body(it, carry):
        where, blocks = [], []
        for u in range(n_per_iter):
            n = it * n_per_iter + u
            q0 = pl.multiple_of(n * 32, 32)
            k0 = pl.multiple_of(jnp.maximum(n - 1, 0) * 32, 32)
            bias = b4_ref[jnp.minimum(n, 1)]
            for r4 in range(4):
                cls = [r4 + 4 * c for c in range(4)]
                cat = lambda ref, start, size, cls=cls: jnp.concatenate(
                    [ref[r, pl.ds(start, size), :] for r in cls], axis=0)
                state = (cat(m_sc, q0, 32), cat(l_sc, q0, 32), cat(acc_sc, q0, 32))
                where.append((cls, q0))
                blocks.append((cat(qb, q0, 32), cat(kb, k0, 64), cat(vb, k0, 64), bias, state))
        for (cls, q0), (m, l, acc) in zip(where, _softmax_blocks(blocks)):
            for c, r in enumerate(cls):
                m_sc[r, pl.ds(q0, 32), :] = m[32 * c:32 * (c + 1)]
                l_sc[r, pl.ds(q0, 32), :] = l[32 * c:32 * (c + 1)]
                acc_sc[r, pl.ds(q0, 32), :] = acc[32 * c:32 * (c + 1)]
        return carry

    lax.fori_loop(0, i_len // 32 // n_per_iter, d4_body, 0)

    def d1_body(it, carry):
        where, blocks = [], []
        for u in range(BLOCKS_PER_ITER):
            mb = it * BLOCKS_PER_ITER + u
            q0 = pl.multiple_of(mb * SUBLANES, SUBLANES)
            q = q_ref[0, :, pl.ds(q0, SUBLANES), :].reshape(Q_BLOCK, HEAD_DIM).astype(BF16)
            if u % 2 == 1:
                k0 = pl.multiple_of((mb - 1) * SUBLANES, 16)
                k, v = kb[:, pl.ds(k0, 16), :], vb[:, pl.ds(k0, 16), :]
            else:
                k0 = pl.multiple_of(jnp.maximum(mb - 2, 0) * SUBLANES, 16)
                k, v = kb8[:, pl.ds(k0, 16), :], vb8[:, pl.ds(k0, 16), :]
                if u == 0:
                    k = jnp.where(mb == 0, kb[:, 0:16, :], k)
                    v = jnp.where(mb == 0, vb[:, 0:16, :], v)
            flat = lambda ref, q0=q0: ref[:, pl.ds(q0, SUBLANES), :].reshape(Q_BLOCK, LANES)
            where.append(q0)
            blocks.append((q, k.reshape(2 * Q_BLOCK, HEAD_DIM), v.reshape(2 * Q_BLOCK, HEAD_DIM),
                           b1_ref[jnp.minimum(mb, 1)], (flat(m_sc), flat(l_sc), flat(acc_sc))))
        for q0, (m, l, acc) in zip(where, _softmax_blocks(blocks)):
            m_sc[:, pl.ds(q0, SUBLANES), :] = m.reshape(CLASSES, SUBLANES, LANES)
            l_sc[:, pl.ds(q0, SUBLANES), :] = l.reshape(CLASSES, SUBLANES, LANES)
            acc_sc[:, pl.ds(q0, SUBLANES), :] = acc.reshape(CLASSES, SUBLANES, LANES)
        return carry

    lax.fori_loop(0, i_len // SUBLANES // BLOCKS_PER_ITER, d1_body, 0)

    hh, bb = pl.program_id(0), pl.program_id(1)
    step = hh * pl.num_programs(1) + bb
    n_steps = pl.num_programs(0) * pl.num_programs(1)
    col = pl.multiple_of(hh * HEAD_DIM, HEAD_DIM)

    def out_copy(r, b_idx, col0):
        return pltpu.make_async_copy(o_sc.at[r], o_hbm.at[b_idx, :, r, pl.ds(col0, HEAD_DIM)], o_sem)

    @pl.when(step > 0)
    def _():
        for r in range(CLASSES):
            out_copy(r, 0, 0).wait()

    o_sc[...] = acc_sc[...] / l_sc[...]
    for r in range(CLASSES):
        out_copy(r, bb, col).start()

    @pl.when(step == n_steps - 1)
    def _():
        for r in range(CLASSES):
            out_copy(r, 0, 0).wait()


def _dilated_attention(qkv_hm, batch):
    n3, _, bi, _ = qkv_hm.shape
    h = n3 // 3
    i_len = bi // batch
    assert CLASSES * SUBLANES == Q_BLOCK and i_len % Q_BLOCK == 0 and BLOCKS_PER_ITER % (i_len // Q_BLOCK) == 0
    biases = [jnp.asarray(a) for a in _attn_biases()]
    blk = (1, CLASSES, i_len, HEAD_DIM)
    const = lambda a: pl.BlockSpec(a.shape, lambda hh, bb: (0,) * a.ndim)
    operand = lambda: pltpu.VMEM((CLASSES, i_len, HEAD_DIM), BF16)
    state = lambda: pltpu.VMEM((CLASSES, i_len, LANES), F32)
    return pl.pallas_call(
        _attn_kernel,
        grid=(h, batch),
        in_specs=[pl.BlockSpec(blk, lambda hh, bb: (hh, 0, bb, 0)),
                  pl.BlockSpec(blk, lambda hh, bb: (h + hh, 0, bb, 0)),
                  pl.BlockSpec(blk, lambda hh, bb: (2 * h + hh, 0, bb, 0))] + [const(a) for a in biases],
        out_specs=pl.BlockSpec(memory_space=pl.ANY),
        out_shape=jax.ShapeDtypeStruct((batch, i_len, CLASSES, h * HEAD_DIM), F32),
        scratch_shapes=[operand(), operand(), operand(), operand(), operand(), state(), state(), state(),
                        pltpu.VMEM((CLASSES, i_len, HEAD_DIM), F32), pltpu.SemaphoreType.DMA(())],
        compiler_params=_cparams(("arbitrary", "arbitrary")),
        name="dilated_attn",
    )(qkv_hm, qkv_hm, qkv_hm, *biases)


LOG2_E = math.log2(math.e)
SEG_LEN = 4
SCAN_ROWS = SUBLANES * SEG_LEN


def _softplus(z):
    return jnp.maximum(z, 0.0) + jnp.log1p(jnp.exp(-jnp.abs(z)))


def _sigmoid(z):
    return 1.0 / (1.0 + jnp.exp2(z * (-LOG2_E)))


def _lru_kernel(xr_ref, xg_ref, cw_ref, cb_ref, wa_ref, ba_ref, wx_ref, bx_ref, lam_ref,
                o_ref, h_sc, tail_sc, a_sc, u_sc, hs_sc):
    ts = xr_ref.shape[1]
    cw = xr_ref.shape[2]
    nv = ts // SUBLANES

    @pl.when(pl.program_id(2) == 0)
    def _():
        h_sc[...] = jnp.zeros_like(h_sc)
        tail_sc[...] = jnp.zeros_like(tail_sc)

    seg = lax.broadcasted_iota(jnp.int32, (SUBLANES, LANES), 0)
    for g in range(cw // LANES):
        sl = slice(g * LANES, (g + 1) * LANES)
        xr = xr_ref[0, :, sl].astype(F32)
        xfull = jnp.concatenate([tail_sc[:, sl], xr], axis=0)
        taps = cw_ref[:, sl]
        xc = xr * taps[3:4]
        for j in range(1, CONV_WIDTH):
            shifted = pltpu.roll(xfull, j, axis=0)[SUBLANES:]
            xc = xc + shifted * taps[3 - j:4 - j]
        xc = xc + cb_ref[:, sl]
        tail_sc[:, sl] = xr[ts - SUBLANES:]

        xcb = xc.astype(BF16)
        gate_a = jnp.dot(xcb, wa_ref[g], preferred_element_type=F32) + ba_ref[:, sl]
        gate_x = jnp.dot(xcb, wx_ref[g], preferred_element_type=F32) + bx_ref[:, sl]
        r = _sigmoid(gate_a)
        i = _sigmoid(gate_x)
        a = jnp.exp2(((-LRU_C * LOG2_E) * _softplus(-lam_ref[:, sl])) * r)
        y = 1.0 - a * a
        u = jnp.where(y > 0.0, y * lax.rsqrt(y), 0.0) * (i * xc)

        a_sc[g] = a
        u_sc[g] = u
        h = h_sc[:, sl]
        for run in range(ts // SCAN_ROWS):
            rows = lambda v: pl.ds(run * SCAN_ROWS + v, SUBLANES, stride=SEG_LEN)
            av = [a_sc[g, rows(v), :] for v in range(SEG_LEN)]
            uv = [u_sc[g, rows(v), :] for v in range(SEG_LEN)]
            local, decay = [uv[0]], [av[0]]
            for v in range(1, SEG_LEN):
                local.append(av[v] * local[-1] + uv[v])
                decay.append(av[v] * decay[-1])
            p, e = decay[-1], local[-1]
            for k in (1, 2, 4):
                e = p * jnp.where(seg >= k, pltpu.roll(e, k, axis=0), 0.0) + e
                p = p * jnp.where(seg >= k, pltpu.roll(p, k, axis=0), 1.0)
            p_in = jnp.where(seg >= 1, pltpu.roll(p, 1, axis=0), 1.0)
            e_in = jnp.where(seg >= 1, pltpu.roll(e, 1, axis=0), 0.0)
            start = e_in + p_in * h
            for v in range(SEG_LEN):
                hs_sc[g, rows(v), :] = local[v] + decay[v] * start
            h = (e + p * h)[SUBLANES - 1:SUBLANES]
        h_sc[:, sl] = h
        o_ref[0, :, sl] = (hs_sc[g] * jax.nn.gelu(xg_ref[0, :, sl].astype(F32))).astype(o_ref.dtype)


def _rglru(rest3, conv_w, conv_b, wa, ba, wx, bx, lam, *, d, ts, cw):
    b, s, _ = rest3.shape
    ncw = d // cw
    gpb = cw // LRU_BLOCK_DIM
    vec = lambda: pl.BlockSpec((1, cw), lambda bb, c, t: (0, c))
    return pl.pallas_call(
        _lru_kernel,
        grid=(b, ncw, s // ts),
        in_specs=[pl.BlockSpec((1, ts, cw), lambda bb, c, t: (bb, t, c)),
                  pl.BlockSpec((1, ts, cw), lambda bb, c, t: (bb, t, ncw + c)),
                  pl.BlockSpec((CONV_WIDTH, cw), lambda bb, c, t: (0, c)),
                  vec(),
                  pl.BlockSpec((gpb, LRU_BLOCK_DIM, LRU_BLOCK_DIM), lambda bb, c, t: (c, 0, 0)),
                  vec(),
                  pl.BlockSpec((gpb, LRU_BLOCK_DIM, LRU_BLOCK_DIM), lambda bb, c, t: (c, 0, 0)),
                  vec(), vec()],
        out_specs=pl.BlockSpec((1, ts, cw), lambda bb, c, t: (bb, t, c)),
        out_shape=jax.ShapeDtypeStruct((b, s, d), BF16),
        scratch_shapes=[pltpu.VMEM((1, cw), F32), pltpu.VMEM((SUBLANES, cw), F32)]
        + [pltpu.VMEM((gpb, ts, LANES), F32)] * 3,
        compiler_params=_cparams(("parallel", "parallel", "arbitrary")),
        name="rglru",
    )(rest3, rest3, conv_w, conv_b, wa, ba, wx, bx, lam)


def _merge_kernel(a_ref, h_ref, ga_ref, gl_ref, wa_ref, wl_ref, o_ref, a_sc):
    @pl.when(pl.program_id(1) == 0)
    def _():
        a_sc[...] = a_ref[...].astype(BF16)

    y_attn = jnp.dot(a_sc[...], wa_ref[...], preferred_element_type=F32)
    y_lru = jnp.dot(h_ref[...], wl_ref[...], preferred_element_type=F32)
    g_a = _sigmoid(ga_ref[...].astype(F32))
    g_l = _sigmoid(gl_ref[...].astype(F32))
    o_ref[...] = (g_a * y_attn + g_l * y_lru).astype(o_ref.dtype)


def _merge(attn, hg, rest, w_o_attn, w_o_lru, *, tm, tn):
    t, aw = attn.shape
    d = w_o_attn.shape[1]
    nj = d // tn
    return pl.pallas_call(
        _merge_kernel,
        grid=(t // tm, nj),
        in_specs=[pl.BlockSpec((tm, aw), lambda i, j: (i, 0)),
                  pl.BlockSpec((tm, d), lambda i, j: (i, 0)),
                  pl.BlockSpec((tm, tn), lambda i, j: (i, 2 * nj + j)),
                  pl.BlockSpec((tm, tn), lambda i, j: (i, 3 * nj + j)),
                  pl.BlockSpec((aw, tn), lambda i, j: (0, j)),
                  pl.BlockSpec((d, tn), lambda i, j: (0, j))],
        out_specs=pl.BlockSpec((tm, tn), lambda i, j: (i, j)),
        out_shape=jax.ShapeDtypeStruct((t, d), BF16),
        scratch_shapes=[pltpu.VMEM((tm, aw), BF16)],
        compiler_params=_cparams(("parallel", "arbitrary")),
        name="merge",
    )(attn, hg, rest, rest, w_o_attn, w_o_lru)


def _layer_norm(z, g, b):
    mu = jnp.mean(z, axis=-1, keepdims=True)
    zc = z - mu
    var = jnp.mean(zc * zc, axis=-1, keepdims=True)
    return zc * lax.rsqrt(var + LN_EPS) * g + b


def _out_ln_route_kernel(m_ref, x_ref, w_ref, g_ref, b_ref, rwh_ref, rwl_ref, rb_ref,
                         x1_ref, route_ref):
    y = jnp.dot(m_ref[...], w_ref[...], preferred_element_type=F32)
    z = DEEPNORM_ALPHA * x_ref[...] + y
    x1 = _layer_norm(z, g_ref[...], b_ref[...])
    x1_ref[...] = x1

    hi = x1.astype(BF16)
    lo = (x1 - hi.astype(F32)).astype(BF16)
    hi_both = jnp.dot(hi, jnp.concatenate([rwh_ref[...], rwl_ref[...]], axis=1), preferred_element_type=F32)
    lg = (hi_both[:, :ROUTER_LANES] + jnp.dot(lo, rwh_ref[...], preferred_element_type=F32)
          + hi_both[:, ROUTER_LANES:]) + rb_ref[...]

    lane = lax.broadcasted_iota(jnp.int32, lg.shape, 1).astype(F32)
    first = lambda hit: jnp.min(jnp.where(hit, lane, float(ROUTER_LANES)), axis=1, keepdims=True)
    in_grp = lane < N_GROUPS
    lgg = jnp.where(in_grp, lg, MASK_BIAS)
    mg = jnp.max(lgg, axis=1, keepdims=True)
    g_idx = first(lgg == mg)
    p_grp = 1.0 / jnp.sum(jnp.where(in_grp, jnp.exp(lg - mg), 0.0), axis=1, keepdims=True)
    lo_lane = N_GROUPS + EXPERTS_PER_GROUP * g_idx
    in_exp = (lane >= lo_lane) & (lane < lo_lane + EXPERTS_PER_GROUP)
    le = jnp.where(in_exp, lg, MASK_BIAS)
    v1 = jnp.max(le, axis=1, keepdims=True)
    i1 = first((le == v1) & in_exp)
    rest = in_exp & (lane != i1)
    le2 = jnp.where(rest, lg, MASK_BIAS)
    v2 = jnp.max(le2, axis=1, keepdims=True)
    i2 = first((le2 == v2) & rest)
    t = jnp.exp(v2 - v1)
    w1 = p_grp / (1.0 + t)
    w2 = p_grp * t / (1.0 + t)
    route_ref[...] = jnp.where(lane == 0.0, i1 - N_GROUPS,
                     jnp.where(lane == 1.0, i2 - N_GROUPS,
                     jnp.where(lane == 2.0, w1,
                     jnp.where(lane == 3.0, w2, 0.0))))


def _out_ln_route(merged, x2d, w_out, ln_g, ln_b, rw_hi, rw_lo, rb, *, tm):
    t, d = merged.shape
    const = lambda a: pl.BlockSpec(a.shape, lambda i: (0,) * a.ndim)
    return pl.pallas_call(
        _out_ln_route_kernel,
        grid=(t // tm,),
        in_specs=[pl.BlockSpec((tm, d), lambda i: (i, 0)),
                  pl.BlockSpec((tm, d), lambda i: (i, 0)),
                  const(w_out), const(ln_g), const(ln_b), const(rw_hi), const(rw_lo), const(rb)],
        out_specs=[pl.BlockSpec((tm, d), lambda i: (i, 0)),
                   pl.BlockSpec((tm, ROUTER_LANES), lambda i: (i, 0))],
        out_shape=[jax.ShapeDtypeStruct((t, d), F32),
                   jax.ShapeDtypeStruct((t, ROUTER_LANES), F32)],
        compiler_params=_cparams(("parallel",)),
        name="out_ln_route",
    )(merged, x2d, w_out, ln_g, ln_b, rw_hi, rw_lo, rb)


def _moe_kernel(te_ref, nu_ref, g0_ref, gn_ref, dst_ref, x_hbm, wg_ref, wu_ref, wd_ref,
                y_hbm, xbuf, ybuf, wg_sc, wu_sc, wd_sc, gsem, ssem):
    i = pl.program_id(0)
    last = pl.num_programs(0) - 1
    n_used = nu_ref[0]
    tm = xbuf.shape[1]

    def gather_copy(tok, j, s):
        return pltpu.make_async_copy(x_hbm.at[pl.ds(tok, 1)], xbuf.at[s, pl.ds(j, 1)], gsem.at[s])

    def scatter_copy(row, j, s):
        return pltpu.make_async_copy(ybuf.at[s, pl.ds(j, 1)], y_hbm.at[pl.ds(row, 1)], ssem.at[s])

    @pl.when(i == 0)
    def _():
        for j in range(tm):
            gather_copy(g0_ref[0, 0, j], j, 0).start()

    @pl.when((i < n_used) & ((i == 0) | (te_ref[i] != te_ref[jnp.maximum(i - 1, 0)])))
    def _():
        wg_sc[...] = wg_ref[0].astype(BF16)
        wu_sc[...] = wu_ref[0].astype(BF16)
        wd_sc[...] = wd_ref[0].astype(BF16)

    def step(slot):
        nxt = 1 - slot

        @pl.when(i < n_used)
        def _():
            for j in range(tm):
                gather_copy(0, j, slot).wait()

            @pl.when(i + 1 < n_used)
            def _():
                for j in range(tm):
                    gather_copy(gn_ref[0, 0, j], j, nxt).start()

            xb = xbuf[slot].astype(BF16)
            gate = jnp.dot(xb, wg_sc[...], preferred_element_type=F32)
            up = jnp.dot(xb, wu_sc[...], preferred_element_type=F32)
            he = (gate * _sigmoid(gate) * up).astype(BF16)
            ybuf[slot] = jnp.dot(he, wd_sc[...], preferred_element_type=F32)

        for j in range(tm):
            scatter_copy(dst_ref[0, 0, j], j, slot).start()

        @pl.when(i > 0)
        def _():
            for j in range(tm):
                scatter_copy(0, j, nxt).wait()

        @pl.when(i == last)
        def _():
            for j in range(tm):
                scatter_copy(0, j, slot).wait()

    for parity in range(2):
        pl.when(lax.rem(i, 2) == parity)(functools.partial(step, parity))


def _moe_experts(x1, tile_expert, n_used, gidx, dest, wg, wu, wd):
    t, d = x1.shape
    nt = gidx.shape[0]
    f = wg.shape[2]
    idx_spec = lambda fn: pl.BlockSpec((1, 1, MOE_TILE), fn, memory_space=pltpu.SMEM)
    grid_spec = pltpu.PrefetchScalarGridSpec(
        num_scalar_prefetch=2,
        grid=(nt,),
        in_specs=[idx_spec(lambda i, te, nu: (0, 0, 0)),
                  idx_spec(lambda i, te, nu: (jnp.minimum(i + 1, nt - 1), 0, 0)),
                  idx_spec(lambda i, te, nu: (i, 0, 0)),
                  pl.BlockSpec(memory_space=pl.ANY),
                  pl.BlockSpec((1, d, f), lambda i, te, nu: (te[i], 0, 0)),
                  pl.BlockSpec((1, d, f), lambda i, te, nu: (te[i], 0, 0)),
                  pl.BlockSpec((1, f, d), lambda i, te, nu: (te[i], 0, 0))],
        out_specs=pl.BlockSpec(memory_space=pl.ANY),
        scratch_shapes=[pltpu.VMEM((2, MOE_TILE, d), F32),
                        pltpu.VMEM((2, MOE_TILE, d), F32),
                        pltpu.VMEM((d, f), BF16),
                        pltpu.VMEM((d, f), BF16),
                        pltpu.VMEM((f, d), BF16),
                        pltpu.SemaphoreType.DMA((2,)),
                        pltpu.SemaphoreType.DMA((2,))],
    )
    return pl.pallas_call(
        _moe_kernel,
        grid_spec=grid_spec,
        out_shape=jax.ShapeDtypeStruct((nt * MOE_TILE, d), F32),
        compiler_params=_cparams(("arbitrary",)),
        name="moe_experts",
    )(tile_expert, n_used, gidx, gidx, dest, x1, wg, wu, wd)


KEY_IDX_BITS = 16


def _moe_plan(route, n_tiles):
    t = route.shape[0]
    p = 2 * t
    assert p <= 1 << KEY_IDX_BITS and n_tiles * MOE_TILE == p + N_EXPERTS * MOE_TILE
    e_flat = jnp.concatenate([route[:, 0], route[:, 1]]).astype(jnp.int32)
    experts = jnp.arange(N_EXPERTS, dtype=jnp.int32)
    counts = jnp.sum((e_flat[:, None] == experts[None, :]).astype(jnp.int32), axis=0)
    n_pad = (-counts) % MOE_TILE
    k = jnp.arange(MOE_TILE, dtype=jnp.int32)
    pad_expert = jnp.where(k[None, :] < n_pad[:, None], experts[:, None], N_EXPERTS)
    pad_id = experts[:, None] * MOE_TILE + k[None, :]
    flag = 1 << KEY_IDX_BITS
    keys = jnp.concatenate([e_flat * (2 * flag) + jnp.arange(p, dtype=jnp.int32),
                            (pad_expert * (2 * flag) + flag + pad_id).reshape(-1)])
    keys = jnp.sort(keys)
    is_pad = (keys & flag) != 0
    idx = keys & (flag - 1)
    dest = jnp.where(is_pad, p + idx, idx)
    gidx = jnp.where(is_pad, 0, jnp.where(idx >= t, idx - t, idx))
    first_expert = keys.reshape(n_tiles, MOE_TILE)[:, 0] // (2 * flag)
    n_used = jnp.sum((first_expert < N_EXPERTS).astype(jnp.int32)).reshape(1)
    tile_expert = jnp.minimum(first_expert, N_EXPERTS - 1)
    shape3 = (n_tiles, 1, MOE_TILE)
    return tile_expert, n_used, gidx.reshape(shape3), dest.reshape(shape3)


def _final_kernel(x1_ref, y0_ref, y1_ref, route_ref, g_ref, b_ref, o_ref):
    y = route_ref[:, 2:3] * y0_ref[...] + route_ref[:, 3:4] * y1_ref[...]
    z = DEEPNORM_ALPHA * x1_ref[...] + y
    o_ref[...] = _layer_norm(z, g_ref[...], b_ref[...])


def _final_ln(x1, y_rows, route, ln_g, ln_b, *, tm):
    t, d = x1.shape
    nb = t // tm
    const = lambda a: pl.BlockSpec(a.shape, lambda i: (0,) * a.ndim)
    return pl.pallas_call(
        _final_kernel,
        grid=(nb,),
        in_specs=[pl.BlockSpec((tm, d), lambda i: (i, 0)),
                  pl.BlockSpec((tm, d), lambda i: (i, 0)),
                  pl.BlockSpec((tm, d), lambda i: (nb + i, 0)),
                  pl.BlockSpec((tm, ROUTER_LANES), lambda i: (i, 0)),
                  const(ln_g), const(ln_b)],
        out_specs=pl.BlockSpec((tm, d), lambda i: (i, 0)),
        out_shape=jax.ShapeDtypeStruct((t, d), F32),
        compiler_params=_cparams(("parallel",)),
        name="final_ln",
    )(x1, y_rows, y_rows, route, ln_g, ln_b)


def _router_weights(router_group_w, router_group_b, router_expert_w, router_expert_b):
    d = router_group_w.shape[0]
    pad = ROUTER_LANES - N_GROUPS - N_EXPERTS
    w = jnp.concatenate([router_group_w, router_expert_w, jnp.zeros((d, pad), F32)], axis=1)
    b = jnp.concatenate([router_group_b, router_expert_b, jnp.zeros((pad,), F32)]).reshape(1, ROUTER_LANES)
    w_hi = w.astype(BF16)
    w_lo = (w - w_hi.astype(F32)).astype(BF16)
    return w_hi, w_lo, b


def kernel(x, w_in, conv_w, conv_b, lru_wa, lru_ba, lru_wx, lru_bx, lru_lambda, w_o_attn, w_o_lru, w_out, ln1_g, ln1_b, router_group_w, router_group_b, router_expert_w, router_expert_b, w_e_gate, w_e_up, w_e_down, ln2_g, ln2_b):
    batch, seq, d = x.shape
    i_len = seq // CLASSES
    t = batch * seq
    attn_w = N_HEADS * HEAD_DIM
    for l in range(DEPTH):
        x2d = x.reshape(t, d)
        x_bf = x2d.astype(BF16)
        x_cls = _to_class_order(x.reshape(batch, i_len, CLASSES, d)).reshape(CLASSES, batch * i_len, d)
        row = lambda a: a.reshape(1, -1)

        qkv_hm = _qkv_proj(x_cls, w_in[l], n=3 * attn_w, tn=PROJ_COLS)
        rest = _rest_proj(x_bf, w_in[l], col0=3 * attn_w, tm=PROJ_ROWS, tn=PROJ_COLS)
        attn = _dilated_attention(qkv_hm, batch).reshape(t, attn_w)
        hg = _rglru(rest.reshape(batch, seq, 4 * d), conv_w[l], row(conv_b[l]),
                    lru_wa[l].astype(BF16), row(lru_ba[l]), lru_wx[l].astype(BF16), row(lru_bx[l]),
                    row(lru_lambda[l]), d=d, ts=LRU_ROWS, cw=LRU_COLS)
        merged = _merge(attn, hg.reshape(t, d), rest, w_o_attn[l].astype(BF16), w_o_lru[l].astype(BF16),
                        tm=PROJ_ROWS, tn=MERGE_COLS)
        rw_hi, rw_lo, rb = _router_weights(router_group_w[l], router_group_b[l],
                                           router_expert_w[l], router_expert_b[l])
        x1, route = _out_ln_route(merged, x2d, w_out[l].astype(BF16),
                                  row(ln1_g[l]), row(ln1_b[l]), rw_hi, rw_lo, rb, tm=NORM_ROWS)

        n_tiles = 2 * t // MOE_TILE + N_EXPERTS
        tile_expert, n_used, gidx, dest = _moe_plan(route, n_tiles)
        y_rows = _moe_experts(x1, tile_expert, n_used, gidx, dest, w_e_gate[l], w_e_up[l], w_e_down[l])
        x = _final_ln(x1, y_rows, route, row(ln2_g[l]), row(ln2_b[l]), tm=NORM_ROWS)
        x = x.reshape(batch, seq, d)
    return x
```

```python
import functools
import math

import numpy as np
import jax
import jax.numpy as jnp
from jax import lax
from jax.experimental import pallas as pl
from jax.experimental.pallas import tpu as pltpu

F32 = jnp.float32
BF16 = jnp.bfloat16

N_HEADS = 16
HEAD_DIM = 128
N_BACK = 128
CLASSES = 16
CONV_WIDTH = 4
LRU_BLOCK_DIM = 128
LRU_C = 8.0
N_GROUPS = 4
EXPERTS_PER_GROUP = 8
N_EXPERTS = N_GROUPS * EXPERTS_PER_GROUP
DEPTH = 1
DEEPNORM_ALPHA = (2.0 * DEPTH) ** 0.25
LN_EPS = 1e-5
ATTN_SCALE_LOG2 = HEAD_DIM ** -0.5 * math.log2(math.e)

LANES = 128
SUBLANES = 8
VMEM_LIMIT_BYTES = 56 * 1024 * 1024

MASK_BIAS = -1e30
MOE_TILE = 256
ROUTER_LANES = LANES

PROJ_ROWS, PROJ_COLS = 1024, 1024
MERGE_COLS = 512
NORM_ROWS = 512
LRU_ROWS, LRU_COLS = 512, 1024


def _cparams(sem):
    return pltpu.CompilerParams(dimension_semantics=sem, vmem_limit_bytes=VMEM_LIMIT_BYTES)


def _class_order_kernel(x_hbm, o_ref, buf, sem):
    nb = pl.num_programs(1)
    n = pl.program_id(0) * nb + pl.program_id(1)
    slot = lax.rem(n, 2)

    def copy(step, buf_slot):
        return pltpu.make_async_copy(x_hbm.at[lax.rem(step, nb), :, step // nb, :], buf.at[buf_slot],
                                     sem.at[buf_slot])

    @pl.when(n == 0)
    def _():
        copy(n, slot).start()

    @pl.when(n + 1 < pl.num_programs(0) * nb)
    def _():
        copy(n + 1, 1 - slot).start()

    copy(n, slot).wait()
    o_ref[0, 0] = buf[slot].astype(o_ref.dtype)


def _to_class_order(x4):
    b, i, c, d = x4.shape
    return pl.pallas_call(
        _class_order_kernel,
        grid=(c, b),
        in_specs=[pl.BlockSpec(memory_space=pl.ANY)],
        out_specs=pl.BlockSpec((1, 1, i, d), lambda r, bb: (r, bb, 0, 0)),
        out_shape=jax.ShapeDtypeStruct((c, b, i, d), BF16),
        scratch_shapes=[pltpu.VMEM((2, i, d), F32), pltpu.SemaphoreType.DMA((2,))],
        compiler_params=_cparams(("arbitrary", "arbitrary")),
        name="class_order",
    )(x4)


def _qkv_kernel(x_ref, w_ref, o_ref, w_sc):
    @pl.when(pl.program_id(1) == 0)
    def _():
        w_sc[...] = w_ref[...].astype(BF16)

    res = jnp.dot(x_ref[0], w_sc[...], preferred_element_type=F32).astype(o_ref.dtype)
    for hh in range(o_ref.shape[0]):
        o_ref[hh, 0] = res[:, hh * HEAD_DIM:(hh + 1) * HEAD_DIM]


def _qkv_proj(x_cls, w_in, *, n, tn):
    _, rows, d = x_cls.shape
    heads_per_step = tn // HEAD_DIM
    return pl.pallas_call(
        _qkv_kernel,
        grid=(n // tn, CLASSES),
        in_specs=[pl.BlockSpec((1, rows, d), lambda j, r: (r, 0, 0)),
                  pl.BlockSpec((d, tn), lambda j, r: (0, j))],
        out_specs=pl.BlockSpec((heads_per_step, 1, rows, HEAD_DIM), lambda j, r: (j, r, 0, 0)),
        out_shape=jax.ShapeDtypeStruct((n // HEAD_DIM, CLASSES, rows, HEAD_DIM), F32),
        scratch_shapes=[pltpu.VMEM((d, tn), BF16)],
        compiler_params=_cparams(("parallel", "arbitrary")),
        name="qkv_proj",
    )(x_cls, w_in)


def _rest_kernel(x_ref, w_ref, o_ref, w_sc):
    @pl.when(pl.program_id(1) == 0)
    def _():
        w_sc[...] = w_ref[...].astype(BF16)

    o_ref[...] = jnp.dot(x_ref[...], w_sc[...], preferred_element_type=F32).astype(o_ref.dtype)


def _rest_proj(x2d, w_in, *, col0, tm, tn):
    t, d = x2d.shape
    n = w_in.shape[1] - col0
    j0 = col0 // tn
    return pl.pallas_call(
        _rest_kernel,
        grid=(n // tn, t // tm),
        in_specs=[pl.BlockSpec((tm, d), lambda j, i: (i, 0)),
                  pl.BlockSpec((d, tn), lambda j, i: (0, j0 + j))],
        out_specs=pl.BlockSpec((tm, tn), lambda j, i: (i, j)),
        out_shape=jax.ShapeDtypeStruct((t, n), BF16),
        scratch_shapes=[pltpu.VMEM((d, tn), BF16)],
        compiler_params=_cparams(("parallel", "arbitrary")),
        name="rest_proj",
    )(x2d, w_in)


def _band_bias(dist):
    return np.where((dist >= 0) & (dist <= N_BACK), 0.0, MASK_BIAS).astype(np.float32)


Q_BLOCK = 128


def _attn_biases():
    lq = np.arange(Q_BLOCK)[:, None]
    b16_first = _band_bias(lq - np.arange(Q_BLOCK)[None, :])
    b16_next = _band_bias(Q_BLOCK + lq - np.arange(2 * Q_BLOCK)[None, :])
    cq = np.repeat(np.arange(4), 32)[:, None]
    lq = np.tile(np.arange(32), 4)[:, None]
    ck = np.repeat(np.arange(4), 64)[None, :]
    lk = np.tile(np.arange(64), 4)[None, :]
    b4 = np.stack([_band_bias(4 * (lq - lk) + (cq - ck)),
                   _band_bias(4 * (32 + lq - lk) + (cq - ck))])
    rq = np.repeat(np.arange(16), 8)[:, None]
    lq = np.tile(np.arange(8), 16)[:, None]
    rk = np.repeat(np.arange(16), 16)[None, :]
    lk = np.tile(np.arange(16), 16)[None, :]
    b1 = np.stack([_band_bias(16 * (lq - lk) + (rq - rk)),
                   _band_bias(16 * (8 + lq - lk) + (rq - rk))])
    return b16_first, b16_next, b4, b1


def _softmax_blocks(blocks):
    scores = [lax.dot_general(q, k, (((1,), (1,)), ((), ())), preferred_element_type=F32) * ATTN_SCALE_LOG2
              + bias for q, k, _, bias, _ in blocks]
    mids = []
    for s, (_, _, _, _, state) in zip(scores, blocks):
        rows, n = s.shape
        chunks = [s[:, c * LANES:(c + 1) * LANES] for c in range(n // LANES)]
        mx = functools.reduce(jnp.maximum, chunks)
        m_blk = jnp.broadcast_to(jnp.max(mx, axis=1, keepdims=True), (rows, LANES))
        m_new = m_blk if state is None else jnp.maximum(state[0], m_blk)
        ps = [jnp.exp2(c - m_new) for c in chunks]
        l_blk = jnp.broadcast_to(jnp.sum(functools.reduce(jnp.add, ps), axis=1, keepdims=True), (rows, LANES))
        mids.append((m_new, l_blk, jnp.concatenate(ps, axis=1).astype(BF16)))
    out = []
    for (m_new, l_blk, p), (_, _, v, _, state) in zip(mids, blocks):
        pv = jnp.dot(p, v, preferred_element_type=F32)
        if state is None:
            out.append((m_new, l_blk, pv))
        else:
            alpha = jnp.exp2(state[0] - m_new)
            out.append((m_new, alpha * state[1] + l_blk, alpha * state[2] + pv))
    return out


BLOCKS_PER_ITER = 16


def _attn_kernel(q_ref, k_ref, v_ref, b16a_ref, b16b_ref, b4_ref, b1_ref, o_hbm,
                 qb, kb, vb, kb8, vb8, m_sc, l_sc, acc_sc, o_sc, o_sem):
    i_len = q_ref.shape[2]
    blocks_per_class = i_len // Q_BLOCK

    qb[...] = q_ref[0].astype(BF16)
    kb[...] = k_ref[0].astype(BF16)
    vb[...] = v_ref[0].astype(BF16)
    kb8[:, :i_len - SUBLANES, :] = k_ref[0, :, SUBLANES:, :].astype(BF16)
    vb8[:, :i_len - SUBLANES, :] = v_ref[0, :, SUBLANES:, :].astype(BF16)

    classes_per_iter = BLOCKS_PER_ITER // blocks_per_class

    def d16_body(it, carry):
        where, blocks = [], []
        for u in range(classes_per_iter):
            r = it * classes_per_iter + u
            for n in range(blocks_per_class):
                rows = slice(n * Q_BLOCK, (n + 1) * Q_BLOCK)
                keys = slice(0, Q_BLOCK) if n == 0 else slice((n - 1) * Q_BLOCK, (n + 1) * Q_BLOCK)
                bias = b16a_ref[...] if n == 0 else b16b_ref[...]
                where.append((r, rows))
                blocks.append((qb[r, rows, :], kb[r, keys, :], vb[r, keys, :], bias, None))
        for (r, rows), (m, l, acc) in zip(where, _softmax_blocks(blocks)):
            m_sc[r, rows, :] = m
            l_sc[r, rows, :] = l
            acc_sc[r, rows, :] = acc
        return carry

    lax.fori_loop(0, CLASSES // classes_per_iter, d16_body, 0)

    n_per_iter = BLOCKS_PER_ITER // 4

    def d4_body(it, carry):
        where, blocks = [], []
        for u in range(n_per_iter):
            n = it * n_per_iter + u
            q0 = pl.multiple_of(n * 32, 32)
            k0 = pl.multiple_of(jnp.maximum(n - 1, 0) * 32, 32)
            bias = b4_ref[jnp.minimum(n, 1)]
            for r4 in range(4):
                cls = [r4 + 4 * c for c in range(4)]
                cat = lambda ref, start, size, cls=cls: jnp.concatenate(
                    [ref[r, pl.ds(start, size), :] for r in cls], axis=0)
                state = (cat(m_sc, q0, 32), cat(l_sc, q0, 32), cat(acc_sc, q0, 32))
                where.append((cls, q0))
                blocks.append((cat(qb, q0, 32), cat(kb, k0, 64), cat(vb, k0, 64), bias, state))
        for (cls, q0), (m, l, acc) in zip(where, _softmax_blocks(blocks)):
            for c, r in enumerate(cls):
                m_sc[r, pl.ds(q0, 32), :] = m[32 * c:32 * (c + 1)]
                l_sc[r, pl.ds(q0, 32), :] = l[32 * c:32 * (c + 1)]
                acc_sc[r, pl.ds(q0, 32), :] = acc[32 * c:32 * (c + 1)]
        return carry

    lax.fori_loop(0, i_len // 32 // n_per_iter, d4_body, 0)

    def d1_body(it, carry):
        where, blocks = [], []
        for u in range(BLOCKS_PER_ITER):
            mb = it * BLOCKS_PER_ITER + u
            q0 = pl.multiple_of(mb * SUBLANES, SUBLANES)
            q = q_ref[0, :, pl.ds(q0, SUBLANES), :].reshape(Q_BLOCK, HEAD_DIM).astype(BF16)
            if u % 2 == 1:
                k0 = pl.multiple_of((mb - 1) * SUBLANES, 16)
                k, v = kb[:, pl.ds(k0, 16), :], vb[:, pl.ds(k0, 16), :]
            else:
                k0 = pl.multiple_of(jnp.maximum(mb - 2, 0) * SUBLANES, 16)
                k, v = kb8[:, pl.ds(k0, 16), :], vb8[:, pl.ds(k0, 16), :]
                if u == 0:
                    k = jnp.where(mb == 0, kb[:, 0:16, :], k)
                    v = jnp.where(mb == 0, vb[:, 0:16, :], v)
            flat = lambda ref, q0=q0: ref[:, pl.ds(q0, SUBLANES), :].reshape(Q_BLOCK, LANES)
            where.append(q0)
            blocks.append((q, k.reshape(2 * Q_BLOCK, HEAD_DIM), v.reshape(2 * Q_BLOCK, HEAD_DIM),
                           b1_ref[jnp.minimum(mb, 1)], (flat(m_sc), flat(l_sc), flat(acc_sc))))
        for q0, (m, l, acc) in zip(where, _softmax_blocks(blocks)):
            m_sc[:, pl.ds(q0, SUBLANES), :] = m.reshape(CLASSES, SUBLANES, LANES)
            l_sc[:, pl.ds(q0, SUBLANES), :] = l.reshape(CLASSES, SUBLANES, LANES)
            acc_sc[:, pl.ds(q0, SUBLANES), :] = acc.reshape(CLASSES, SUBLANES, LANES)
        return carry

    lax.fori_loop(0, i_len // SUBLANES // BLOCKS_PER_ITER, d1_body, 0)

    hh, bb = pl.program_id(0), pl.program_id(1)
    step = hh * pl.num_programs(1) + bb
    n_steps = pl.num_programs(0) * pl.num_programs(1)
    col = pl.multiple_of(hh * HEAD_DIM, HEAD_DIM)

    def out_copy(r, b_idx, col0):
        return pltpu.make_async_copy(o_sc.at[r], o_hbm.at[b_idx, :, r, pl.ds(col0, HEAD_DIM)], o_sem)

    @pl.when(step > 0)
    def _():
        for r in range(CLASSES):
            out_copy(r, 0, 0).wait()

    o_sc[...] = acc_sc[...] / l_sc[...]
    for r in range(CLASSES):
        out_copy(r, bb, col).start()

    @pl.when(step == n_steps - 1)
    def _():
        for r in range(CLASSES):
            out_copy(r, 0, 0).wait()


def _dilated_attention(qkv_hm, batch):
    n3, _, bi, _ = qkv_hm.shape
    h = n3 // 3
    i_len = bi // batch
    assert CLASSES * SUBLANES == Q_BLOCK and i_len % Q_BLOCK == 0 and BLOCKS_PER_ITER % (i_len // Q_BLOCK) == 0
    biases = [jnp.asarray(a) for a in _attn_biases()]
    blk = (1, CLASSES, i_len, HEAD_DIM)
    const = lambda a: pl.BlockSpec(a.shape, lambda hh, bb: (0,) * a.ndim)
    operand = lambda: pltpu.VMEM((CLASSES, i_len, HEAD_DIM), BF16)
    state = lambda: pltpu.VMEM((CLASSES, i_len, LANES), F32)
    return pl.pallas_call(
        _attn_kernel,
        grid=(h, batch),
        in_specs=[pl.BlockSpec(blk, lambda hh, bb: (hh, 0, bb, 0)),
                  pl.BlockSpec(blk, lambda hh, bb: (h + hh, 0, bb, 0)),
                  pl.BlockSpec(blk, lambda hh, bb: (2 * h + hh, 0, bb, 0))] + [const(a) for a in biases],
        out_specs=pl.BlockSpec(memory_space=pl.ANY),
        out_shape=jax.ShapeDtypeStruct((batch, i_len, CLASSES, h * HEAD_DIM), F32),
        scratch_shapes=[operand(), operand(), operand(), operand(), operand(), state(), state(), state(),
                        pltpu.VMEM((CLASSES, i_len, HEAD_DIM), F32), pltpu.SemaphoreType.DMA(())],
        compiler_params=_cparams(("arbitrary", "arbitrary")),
        name="dilated_attn",
    )(qkv_hm, qkv_hm, qkv_hm, *biases)


LOG2_E = math.log2(math.e)
SEG_LEN = 4
SCAN_ROWS = SUBLANES * SEG_LEN


def _softplus(z):
    return jnp.maximum(z, 0.0) + jnp.log1p(jnp.exp(-jnp.abs(z)))


def _sigmoid(z):
    return 1.0 / (1.0 + jnp.exp2(z * (-LOG2_E)))


def _lru_kernel(xr_ref, xg_ref, cw_ref, cb_ref, wa_ref, ba_ref, wx_ref, bx_ref, lam_ref,
                o_ref, h_sc, tail_sc, a_sc, u_sc, hs_sc):
    ts = xr_ref.shape[1]
    cw = xr_ref.shape[2]
    nv = ts // SUBLANES

    @pl.when(pl.program_id(2) == 0)
    def _():
        h_sc[...] = jnp.zeros_like(h_sc)
        tail_sc[...] = jnp.zeros_like(tail_sc)

    seg = lax.broadcasted_iota(jnp.int32, (SUBLANES, LANES), 0)
    for g in range(cw // LANES):
        sl = slice(g * LANES, (g + 1) * LANES)
        xr = xr_ref[0, :, sl].astype(F32)
        xfull = jnp.concatenate([tail_sc[:, sl], xr], axis=0)
        taps = cw_ref[:, sl]
        xc = xr * taps[3:4]
        for j in range(1, CONV_WIDTH):
            shifted = pltpu.roll(xfull, j, axis=0)[SUBLANES:]
            xc = xc + shifted * taps[3 - j:4 - j]
        xc = xc + cb_ref[:, sl]
        tail_sc[:, sl] = xr[ts - SUBLANES:]

        xcb = xc.astype(BF16)
        gate_a = jnp.dot(xcb, wa_ref[g], preferred_element_type=F32) + ba_ref[:, sl]
        gate_x = jnp.dot(xcb, wx_ref[g], preferred_element_type=F32) + bx_ref[:, sl]
        r = _sigmoid(gate_a)
        i = _sigmoid(gate_x)
        a = jnp.exp2(((-LRU_C * LOG2_E) * _softplus(-lam_ref[:, sl])) * r)
        y = 1.0 - a * a
        u = jnp.where(y > 0.0, y * lax.rsqrt(y), 0.0) * (i * xc)

        a_sc[g] = a
        u_sc[g] = u
        h = h_sc[:, sl]
        for run in range(ts // SCAN_ROWS):
            rows = lambda v: pl.ds(run * SCAN_ROWS + v, SUBLANES, stride=SEG_LEN)
            av = [a_sc[g, rows(v), :] for v in range(SEG_LEN)]
            uv = [u_sc[g, rows(v), :] for v in range(SEG_LEN)]
            local, decay = [uv[0]], [av[0]]
            for v in range(1, SEG_LEN):
                local.append(av[v] * local[-1] + uv[v])
                decay.append(av[v] * decay[-1])
            p, e = decay[-1], local[-1]
            for k in (1, 2, 4):
                e = p * jnp.where(seg >= k, pltpu.roll(e, k, axis=0), 0.0) + e
                p = p * jnp.where(seg >= k, pltpu.roll(p, k, axis=0), 1.0)
            p_in = jnp.where(seg >= 1, pltpu.roll(p, 1, axis=0), 1.0)
            e_in = jnp.where(seg >= 1, pltpu.roll(e, 1, axis=0), 0.0)
            start = e_in + p_in * h
            for v in range(SEG_LEN):
                hs_sc[g, rows(v), :] = local[v] + decay[v] * start
            h = (e + p * h)[SUBLANES - 1:SUBLANES]
        h_sc[:, sl] = h
        o_ref[0, :, sl] = (hs_sc[g] * jax.nn.gelu(xg_ref[0, :, sl].astype(F32))).astype(o_ref.dtype)


def _rglru(rest3, conv_w, conv_b, wa, ba, wx, bx, lam, *, d, ts, cw):
    b, s, _ = rest3.shape
    ncw = d // cw
    gpb = cw // LRU_BLOCK_DIM
    vec = lambda: pl.BlockSpec((1, cw), lambda bb, c, t: (0, c))
    return pl.pallas_call(
        _lru_kernel,
        grid=(b, ncw, s // ts),
        in_specs=[pl.BlockSpec((1, ts, cw), lambda bb, c, t: (bb, t, c)),
                  pl.BlockSpec((1, ts, cw), lambda bb, c, t: (bb, t, ncw + c)),
                  pl.BlockSpec((CONV_WIDTH, cw), lambda bb, c, t: (0, c)),
                  vec(),
                  pl.BlockSpec((gpb, LRU_BLOCK_DIM, LRU_BLOCK_DIM), lambda bb, c, t: (c, 0, 0)),
                  vec(),
                  pl.BlockSpec((gpb, LRU_BLOCK_DIM, LRU_BLOCK_DIM), lambda bb, c, t: (c, 0, 0)),
                  vec(), vec()],
        out_specs=pl.BlockSpec((1, ts, cw), lambda bb, c, t: (bb, t, c)),
        out_shape=jax.ShapeDtypeStruct((b, s, d), BF16),
        scratch_shapes=[pltpu.VMEM((1, cw), F32), pltpu.VMEM((SUBLANES, cw), F32)]
        + [pltpu.VMEM((gpb, ts, LANES), F32)] * 3,
        compiler_params=_cparams(("parallel", "parallel", "arbitrary")),
        name="rglru",
    )(rest3, rest3, conv_w, conv_b, wa, ba, wx, bx, lam)


def _merge_kernel(a_ref, h_ref, ga_ref, gl_ref, wa_ref, wl_ref, o_ref, a_sc):
    @pl.when(pl.program_id(1) == 0)
    def _():
        a_sc[...] = a_ref[...].astype(BF16)

    y_attn = jnp.dot(a_sc[...], wa_ref[...], preferred_element_type=F32)
    y_lru = jnp.dot(h_ref[...], wl_ref[...], preferred_element_type=F32)
    g_a = _sigmoid(ga_ref[...].astype(F32))
    g_l = _sigmoid(gl_ref[...].astype(F32))
    o_ref[...] = (g_a * y_attn + g_l * y_lru).astype(o_ref.dtype)


def _merge(attn, hg, rest, w_o_attn, w_o_lru, *, tm, tn):
    t, aw = attn.shape
    d = w_o_attn.shape[1]
    nj = d // tn
    return pl.pallas_call(
        _merge_kernel,
        grid=(t // tm, nj),
        in_specs=[pl.BlockSpec((tm, aw), lambda i, j: (i, 0)),
                  pl.BlockSpec((tm, d), lambda i, j: (i, 0)),
                  pl.BlockSpec((tm, tn), lambda i, j: (i, 2 * nj + j)),
                  pl.BlockSpec((tm, tn), lambda i, j: (i, 3 * nj + j)),
                  pl.BlockSpec((aw, tn), lambda i, j: (0, j)),
                  pl.BlockSpec((d, tn), lambda i, j: (0, j))],
        out_specs=pl.BlockSpec((tm, tn), lambda i, j: (i, j)),
        out_shape=jax.ShapeDtypeStruct((t, d), BF16),
        scratch_shapes=[pltpu.VMEM((tm, aw), BF16)],
        compiler_params=_cparams(("parallel", "arbitrary")),
        name="merge",
    )(attn, hg, rest, rest, w_o_attn, w_o_lru)


def _layer_norm(z, g, b):
    mu = jnp.mean(z, axis=-1, keepdims=True)
    zc = z - mu
    var = jnp.mean(zc * zc, axis=-1, keepdims=True)
    return zc * lax.rsqrt(var + LN_EPS) * g + b


def _out_ln_route_kernel(m_ref, x_ref, w_ref, g_ref, b_ref, rwh_ref, rwl_ref, rb_ref,
                         x1_ref, route_ref):
    y = jnp.dot(m_ref[...], w_ref[...], preferred_element_type=F32)
    z = DEEPNORM_ALPHA * x_ref[...] + y
    x1 = _layer_norm(z, g_ref[...], b_ref[...])
    x1_ref[...] = x1

    hi = x1.astype(BF16)
    lo = (x1 - hi.astype(F32)).astype(BF16)
    hi_both = jnp.dot(hi, jnp.concatenate([rwh_ref[...], rwl_ref[...]], axis=1), preferred_element_type=F32)
    lg = (hi_both[:, :ROUTER_LANES] + jnp.dot(lo, rwh_ref[...], preferred_element_type=F32)
          + hi_both[:, ROUTER_LANES:]) + rb_ref[...]

    lane = lax.broadcasted_iota(jnp.int32, lg.shape, 1).astype(F32)
    first = lambda hit: jnp.min(jnp.where(hit, lane, float(ROUTER_LANES)), axis=1, keepdims=True)
    in_grp = lane < N_GROUPS
    lgg = jnp.where(in_grp, lg, MASK_BIAS)
    mg = jnp.max(lgg, axis=1, keepdims=True)
    g_idx = first(lgg == mg)
    p_grp = 1.0 / jnp.sum(jnp.where(in_grp, jnp.exp(lg - mg), 0.0), axis=1, keepdims=True)
    lo_lane = N_GROUPS + EXPERTS_PER_GROUP * g_idx
    in_exp = (lane >= lo_lane) & (lane < lo_lane + EXPERTS_PER_GROUP)
    le = jnp.where(in_exp, lg, MASK_BIAS)
    v1 = jnp.max(le, axis=1, keepdims=True)
    i1 = first((le == v1) & in_exp)
    rest = in_exp & (lane != i1)
    le2 = jnp.where(rest, lg, MASK_BIAS)
    v2 = jnp.max(le2, axis=1, keepdims=True)
    i2 = first((le2 == v2) & rest)
    t = jnp.exp(v2 - v1)
    w1 = p_grp / (1.0 + t)
    w2 = p_grp * t / (1.0 + t)
    route_ref[...] = jnp.where(lane == 0.0, i1 - N_GROUPS,
                     jnp.where(lane == 1.0, i2 - N_GROUPS,
                     jnp.where(lane == 2.0, w1,
                     jnp.where(lane == 3.0, w2, 0.0))))


def _out_ln_route(merged, x2d, w_out, ln_g, ln_b, rw_hi, rw_lo, rb, *, tm):
    t, d = merged.shape
    const = lambda a: pl.BlockSpec(a.shape, lambda i: (0,) * a.ndim)
    return pl.pallas_call(
        _out_ln_route_kernel,
        grid=(t // tm,),
        in_specs=[pl.BlockSpec((tm, d), lambda i: (i, 0)),
                  pl.BlockSpec((tm, d), lambda i: (i, 0)),
                  const(w_out), const(ln_g), const(ln_b), const(rw_hi), const(rw_lo), const(rb)],
        out_specs=[pl.BlockSpec((tm, d), lambda i: (i, 0)),
                   pl.BlockSpec((tm, ROUTER_LANES), lambda i: (i, 0))],
        out_shape=[jax.ShapeDtypeStruct((t, d), F32),
                   jax.ShapeDtypeStruct((t, ROUTER_LANES), F32)],
        compiler_params=_cparams(("parallel",)),
        name="out_ln_route",
    )(merged, x2d, w_out, ln_g, ln_b, rw_hi, rw_lo, rb)


def _moe_kernel(te_ref, nu_ref, g0_ref, gn_ref, dst_ref, x_hbm, wg_ref, wu_ref, wd_ref,
                y_hbm, xbuf, ybuf, wg_sc, wu_sc, wd_sc, gsem, ssem):
    i = pl.program_id(0)
    last = pl.num_programs(0) - 1
    n_used = nu_ref[0]
    tm = xbuf.shape[1]

    def gather_copy(tok, j, s):
        return pltpu.make_async_copy(x_hbm.at[pl.ds(tok, 1)], xbuf.at[s, pl.ds(j, 1)], gsem.at[s])

    def scatter_copy(row, j, s):
        return pltpu.make_async_copy(ybuf.at[s, pl.ds(j, 1)], y_hbm.at[pl.ds(row, 1)], ssem.at[s])

    @pl.when(i == 0)
    def _():
        for j in range(tm):
            gather_copy(g0_ref[0, 0, j], j, 0).start()

    @pl.when((i < n_used) & ((i == 0) | (te_ref[i] != te_ref[jnp.maximum(i - 1, 0)])))
    def _():
        wg_sc[...] = wg_ref[0].astype(BF16)
        wu_sc[...] = wu_ref[0].astype(BF16)
        wd_sc[...] = wd_ref[0].astype(BF16)

    def step(slot):
        nxt = 1 - slot

        @pl.when(i < n_used)
        def _():
            for j in range(tm):
                gather_copy(0, j, slot).wait()

            @pl.when(i + 1 < n_used)
            def _():
                for j in range(tm):
                    gather_copy(gn_ref[0, 0, j], j, nxt).start()

            xb = xbuf[slot].astype(BF16)
            gate = jnp.dot(xb, wg_sc[...], preferred_element_type=F32)
            up = jnp.dot(xb, wu_sc[...], preferred_element_type=F32)
            he = (gate * _sigmoid(gate) * up).astype(BF16)
            ybuf[slot] = jnp.dot(he, wd_sc[...], preferred_element_type=F32)

        for j in range(tm):
            scatter_copy(dst_ref[0, 0, j], j, slot).start()

        @pl.when(i > 0)
        def _():
            for j in range(tm):
                scatter_copy(0, j, nxt).wait()

        @pl.when(i == last)
        def _():
            for j in range(tm):
                scatter_copy(0, j, slot).wait()

    for parity in range(2):
        pl.when(lax.rem(i, 2) == parity)(functools.partial(step, parity))


def _moe_experts(x1, tile_expert, n_used, gidx, dest, wg, wu, wd):
    t, d = x1.shape
    nt = gidx.shape[0]
    f = wg.shape[2]
    idx_spec = lambda fn: pl.BlockSpec((1, 1, MOE_TILE), fn, memory_space=pltpu.SMEM)
    grid_spec = pltpu.PrefetchScalarGridSpec(
        num_scalar_prefetch=2,
        grid=(nt,),
        in_specs=[idx_spec(lambda i, te, nu: (0, 0, 0)),
                  idx_spec(lambda i, te, nu: (jnp.minimum(i + 1, nt - 1), 0, 0)),
                  idx_spec(lambda i, te, nu: (i, 0, 0)),
                  pl.BlockSpec(memory_space=pl.ANY),
                  pl.BlockSpec((1, d, f), lambda i, te, nu: (te[i], 0, 0)),
                  pl.BlockSpec((1, d, f), lambda i, te, nu: (te[i], 0, 0)),
                  pl.BlockSpec((1, f, d), lambda i, te, nu: (te[i], 0, 0))],
        out_specs=pl.BlockSpec(memory_space=pl.ANY),
        scratch_shapes=[pltpu.VMEM((2, MOE_TILE, d), F32),
                        pltpu.VMEM((2, MOE_TILE, d), F32),
                        pltpu.VMEM((d, f), BF16),
                        pltpu.VMEM((d, f), BF16),
                        pltpu.VMEM((f, d), BF16),
                        pltpu.SemaphoreType.DMA((2,)),
                        pltpu.SemaphoreType.DMA((2,))],
    )
    return pl.pallas_call(
        _moe_kernel,
        grid_spec=grid_spec,
        out_shape=jax.ShapeDtypeStruct((nt * MOE_TILE, d), F32),
        compiler_params=_cparams(("arbitrary",)),
        name="moe_experts",
    )(tile_expert, n_used, gidx, gidx, dest, x1, wg, wu, wd)


KEY_IDX_BITS = 16


def _moe_plan(route, n_tiles):
    t = route.shape[0]
    p = 2 * t
    assert p <= 1 << KEY_IDX_BITS and n_tiles * MOE_TILE == p + N_EXPERTS * MOE_TILE
    e_flat = jnp.concatenate([route[:, 0], route[:, 1]]).astype(jnp.int32)
    experts = jnp.arange(N_EXPERTS, dtype=jnp.int32)
    counts = jnp.sum((e_flat[:, None] == experts[None, :]).astype(jnp.int32), axis=0)
    n_pad = (-counts) % MOE_TILE
    k = jnp.arange(MOE_TILE, dtype=jnp.int32)
    pad_expert = jnp.where(k[None, :] < n_pad[:, None], experts[:, None], N_EXPERTS)
    pad_id = experts[:, None] * MOE_TILE + k[None, :]
    flag = 1 << KEY_IDX_BITS
    keys = jnp.concatenate([e_flat * (2 * flag) + jnp.arange(p, dtype=jnp.int32),
                            (pad_expert * (2 * flag) + flag + pad_id).reshape(-1)])
    keys = jnp.sort(keys)
    is_pad = (keys & flag) != 0
    idx = keys & (flag - 1)
    dest = jnp.where(is_pad, p + idx, idx)
    gidx = jnp.where(is_pad, 0, jnp.where(idx >= t, idx - t, idx))
    first_expert = keys.reshape(n_tiles, MOE_TILE)[:, 0] // (2 * flag)
    n_used = jnp.sum((first_expert < N_EXPERTS).astype(jnp.int32)).reshape(1)
    tile_expert = jnp.minimum(first_expert, N_EXPERTS - 1)
    shape3 = (n_tiles, 1, MOE_TILE)
    return tile_expert, n_used, gidx.reshape(shape3), dest.reshape(shape3)


def _final_kernel(x1_ref, y0_ref, y1_ref, route_ref, g_ref, b_ref, o_ref):
    y = route_ref[:, 2:3] * y0_ref[...] + route_ref[:, 3:4] * y1_ref[...]
    z = DEEPNORM_ALPHA * x1_ref[...] + y
    o_ref[...] = _layer_norm(z, g_ref[...], b_ref[...])


def _final_ln(x1, y_rows, route, ln_g, ln_b, *, tm):
    t, d = x1.shape
    nb = t // tm
    const = lambda a: pl.BlockSpec(a.shape, lambda i: (0,) * a.ndim)
    return pl.pallas_call(
        _final_kernel,
        grid=(nb,),
        in_specs=[pl.BlockSpec((tm, d), lambda i: (i, 0)),
                  pl.BlockSpec((tm, d), lambda i: (i, 0)),
                  pl.BlockSpec((tm, d), lambda i: (nb + i, 0)),
                  pl.BlockSpec((tm, ROUTER_LANES), lambda i: (i, 0)),
                  const(ln_g), const(ln_b)],
        out_specs=pl.BlockSpec((tm, d), lambda i: (i, 0)),
        out_shape=jax.ShapeDtypeStruct((t, d), F32),
        compiler_params=_cparams(("parallel",)),
        name="final_ln",
    )(x1, y_rows, y_rows, route, ln_g, ln_b)


def _router_weights(router_group_w, router_group_b, router_expert_w, router_expert_b):
    d = router_group_w.shape[0]
    pad = ROUTER_LANES - N_GROUPS - N_EXPERTS
    w = jnp.concatenate([router_group_w, router_expert_w, jnp.zeros((d, pad), F32)], axis=1)
    b = jnp.concatenate([router_group_b, router_expert_b, jnp.zeros((pad,), F32)]).reshape(1, ROUTER_LANES)
    w_hi = w.astype(BF16)
    w_lo = (w - w_hi.astype(F32)).astype(BF16)
    return w_hi, w_lo, b


def kernel(x, w_in, conv_w, conv_b, lru_wa, lru_ba, lru_wx, lru_bx, lru_lambda, w_o_attn, w_o_lru, w_out, ln1_g, ln1_b, router_group_w, router_group_b, router_expert_w, router_expert_b, w_e_gate, w_e_up, w_e_down, ln2_g, ln2_b):
    batch, seq, d = x.shape
    i_len = seq // CLASSES
    t = batch * seq
    attn_w = N_HEADS * HEAD_DIM
    for l in range(DEPTH):
        x2d = x.reshape(t, d)
        x_bf = x2d.astype(BF16)
        x_cls = _to_class_order(x.reshape(batch, i_len, CLASSES, d)).reshape(CLASSES, batch * i_len, d)
        row = lambda a: a.reshape(1, -1)

        qkv_hm = _qkv_proj(x_cls, w_in[l], n=3 * attn_w, tn=PROJ_COLS)
        rest = _rest_proj(x_bf, w_in[l], col0=3 * attn_w, tm=PROJ_ROWS, tn=PROJ_COLS)
        attn = _dilated_attention(qkv_hm, batch).reshape(t, attn_w)
        hg = _rglru(rest.reshape(batch, seq, 4 * d), conv_w[l], row(conv_b[l]),
                    lru_wa[l].astype(BF16), row(lru_ba[l]), lru_wx[l].astype(BF16), row(lru_bx[l]),
                    row(lru_lambda[l]), d=d, ts=LRU_ROWS, cw=LRU_COLS)
        merged = _merge(attn, hg.reshape(t, d), rest, w_o_attn[l].astype(BF16), w_o_lru[l].astype(BF16),
                        tm=PROJ_ROWS, tn=MERGE_COLS)
        rw_hi, rw_lo, rb = _router_weights(router_group_w[l], router_group_b[l],
                                           router_expert_w[l], router_expert_b[l])
        x1, route = _out_ln_route(merged, x2d, w_out[l].astype(BF16),
                                  row(ln1_g[l]), row(ln1_b[l]), rw_hi, rw_lo, rb, tm=NORM_ROWS)

        n_tiles = 2 * t // MOE_TILE + N_EXPERTS
        tile_expert, n_used, gidx, dest = _moe_plan(route, n_tiles)
        y_rows = _moe_experts(x1, tile_expert, n_used, gidx, dest, w_e_gate[l], w_e_up[l], w_e_down[l])
        x = _final_ln(x1, y_rows, route, row(ln2_g[l]), row(ln2_b[l]), tm=NORM_ROWS)
        x = x.reshape(batch, seq, d)
    return x
```
